```python
import jax, jax.numpy as jnp
from jax import lax
import numpy as np

D_MODEL = 1024
BATCH = 32
SEQ = 256
DEPTH = 2
DEC_BATCH = 8
DEC_SEQ = 1024
PAST_LEN = 256

GRID_W = 64
HEAD_DIM = 64
NA_HEADS = 4
NA_WIN_H = 8
NA_WIN_W = 16
NA_QBLOCK = 16
NA_KBLOCK = NA_QBLOCK + NA_WIN_W
RW_HEADS = 4
RW_DIM = RW_HEADS * HEAD_DIM
DECAY_LORA = 64
ICLR_LORA = 64
GATE_LORA = 128
GN_EPS = 64e-5
MLA_HEADS = 4
Q_LORA = 256
KV_LORA = 128
NOPE_DIM = 64
ROPE_DIM = 32
V_DIM = 64
MLA_SCALE = (NOPE_DIM + ROPE_DIM) ** -0.5
ROPE_BASE = 10000.0
N_BRANCH = 3
BRANCH_DIM = 256
D_FF = 2816
N_MOD = 9
NORM_EPS = 1e-6
ATTN_QBLOCK = 128
NEG_INF = -1e30
IN_SIZES = (NA_HEADS * HEAD_DIM,) * 3 + (RW_DIM,) * 3 + (DECAY_LORA, DECAY_LORA, ICLR_LORA, ICLR_LORA, GATE_LORA, Q_LORA, KV_LORA, ROPE_DIM, N_BRANCH * D_MODEL)
IN_COLS = sum(IN_SIZES)

kernel_name = 'hybrid_na_rwkv7_mla_flow_step'


def _rmsnorm(x, g):
    xf = x.astype(jnp.float32)
    y = xf * lax.rsqrt(jnp.mean(xf * xf, axis=-1, keepdims=True) + NORM_EPS)
    return (y * g.astype(jnp.float32)).astype(x.dtype)


def _swiglu(h, wg, wu, wd):
    return (jax.nn.silu(h @ wg) * (h @ wu)) @ wd


def _heads(z, n):
    b, t, _ = z.shape
    return z.reshape(b, t, n, -1).transpose(0, 2, 1, 3)


def _merge(z):
    b, h, t, d = z.shape
    return z.transpose(0, 2, 1, 3).reshape(b, t, h * d)


def _split_cols(z):
    idx, acc = [], 0
    for s in IN_SIZES[:-1]:
        acc += s
        idx.append(acc)
    return jnp.split(z, idx, axis=-1)


def _axial_rope(t_len, dim):
    n_freq = dim // 4
    inv = 1.0 / (ROPE_BASE ** (np.arange(n_freq) / n_freq))
    pos = np.arange(t_len)
    ang = np.concatenate([(pos // GRID_W)[:, None] * inv, (pos % GRID_W)[:, None] * inv], axis=-1)
    return jnp.asarray(np.cos(ang), jnp.float32), jnp.asarray(np.sin(ang), jnp.float32)


def _rope(x, cos, sin):
    half = x.shape[-1] // 2
    x1, x2 = x[..., :half], x[..., half:]
    cos = cos.astype(x.dtype)
    sin = sin.astype(x.dtype)
    return jnp.concatenate([x1 * cos - x2 * sin, x1 * sin + x2 * cos], axis=-1)


def _block_attention(q, k, v, scale):
    b, h, tq, dk = q.shape
    qbs = min(ATTN_QBLOCK, tq)
    nblk = tq // qbs
    qb = (q * scale).reshape(b, h, nblk, qbs, dk).transpose(2, 0, 1, 3, 4)

    def one(qi):
        s = jnp.einsum('bhqd,bhkd->bhqk', qi, k).astype(jnp.float32)
        p = jax.nn.softmax(s, axis=-1).astype(v.dtype)
        return jnp.einsum('bhqk,bhkd->bhqd', p, v)

    o = lax.map(one, qb)
    return o.transpose(1, 2, 0, 3, 4).reshape(b, h, tq, v.shape[-1])


def _na_latent(q, k, v, k_ctx, v_ctx, rpb):
    b, h, t, d = q.shape
    rows = t // GRID_W
    kh = min(NA_WIN_H, rows)
    nb = GRID_W // NA_QBLOCK
    qcol = np.arange(GRID_W).reshape(nb, NA_QBLOCK)
    col_lo = np.clip(qcol - NA_WIN_W // 2, 0, GRID_W - NA_WIN_W)
    blk_lo = np.clip(np.arange(nb) * NA_QBLOCK - NA_WIN_W // 2, 0, GRID_W - NA_KBLOCK)
    key_col = blk_lo[:, None] + np.arange(NA_KBLOCK)[None, :]
    kc = key_col[:, None, :]
    col_ok = (kc >= col_lo[..., None]) & (kc < col_lo[..., None] + NA_WIN_W)
    col_idx = np.clip(kc - qcol[..., None] + NA_WIN_W - 1, 0, 2 * NA_WIN_W - 2)
    row_lo = np.clip(np.arange(rows) - NA_WIN_H // 2, 0, rows - kh)
    row_idx = row_lo[:, None] + np.arange(kh)[None, :] - np.arange(rows)[:, None] + NA_WIN_H - 1
    mask = jnp.asarray(col_ok)[:, :, None, :]
    kg = k.reshape(b, h, rows, GRID_W, d)
    vg = v.reshape(b, h, rows, GRID_W, d)
    qr = jnp.moveaxis((q * d ** -0.5).reshape(b, h, rows, nb, NA_QBLOCK, d), 2, 0)
    n_loc = kh * NA_KBLOCK

    def one_row(inp):
        qi, lo, ridx = inp
        k_blk = lax.dynamic_slice_in_dim(kg, lo, kh, axis=2)[:, :, :, key_col, :]
        v_blk = lax.dynamic_slice_in_dim(vg, lo, kh, axis=2)[:, :, :, key_col, :]
        bias = rpb[:, ridx[:, None, None, None], col_idx[None]]
        bias = jnp.transpose(bias, (0, 2, 3, 1, 4)).astype(jnp.float32)
        s_loc = jnp.einsum('bhmqd,bhkmcd->bhmqkc', qi, k_blk).astype(jnp.float32) + bias
        s_loc = jnp.where(mask, s_loc, NEG_INF).reshape(b, h, nb, NA_QBLOCK, n_loc)
        s_ctx = jnp.einsum('bhmqd,bhld->bhmql', qi, k_ctx).astype(jnp.float32)
        p = jax.nn.softmax(jnp.concatenate([s_loc, s_ctx], axis=-1), axis=-1).astype(v.dtype)
        p_loc = p[..., :n_loc].reshape(b, h, nb, NA_QBLOCK, kh, NA_KBLOCK)
        return (jnp.einsum('bhmqkc,bhkmcd->bhmqd', p_loc, v_blk)
                + jnp.einsum('bhmql,bhld->bhmqd', p[..., n_loc:], v_ctx))

    o = lax.map(one_row, (qr, jnp.asarray(row_lo, jnp.int32), jnp.asarray(row_idx, jnp.int32)))
    return jnp.moveaxis(o, 0, 2).reshape(b, h, t, d)


def _rwkv_scan(r, w, k, v, kk, a, s0, reverse):
    def step(S, inp):
        r_t, w_t, k_t, v_t, kk_t, a_t = inp
        sa = jnp.einsum('bhvk,bhk->bhv', S, kk_t)
        S = (S * w_t[:, :, None, :] - sa[..., None] * (kk_t * a_t)[:, :, None, :]
             + v_t[..., None] * k_t[:, :, None, :])
        return S, jnp.einsum('bhvk,bhk->bhv', S, r_t)

    xs = tuple(jnp.moveaxis(z, 1, 0) for z in (r, w, k, v, kk, a))
    s_fin, y = lax.scan(step, s0, xs, reverse=reverse)
    return s_fin, jnp.moveaxis(y, 0, 1)


def _rwkv(r, k, v, w1s, a1s, g1, lp, s0s):
    b, t, _ = r.shape
    dt = r.dtype
    f32 = jnp.float32

    def hd(z):
        return z.reshape(b, t, RW_HEADS, HEAD_DIM).astype(f32)

    rh, kh, vh = hd(r), hd(k), hd(v)
    kk = hd(k * lp['rw_kk'])
    kk = kk / jnp.maximum(jnp.sqrt(jnp.sum(kk * kk, axis=-1, keepdims=True)), 1e-12)
    ka = lp['rw_ka'].reshape(RW_HEADS, HEAD_DIM).astype(f32)
    rk = lp['rw_rk'].astype(f32)
    y_sum, bonus, finals = 0.0, 0.0, []
    for dirn in range(2):
        wlog = hd(lp['rw_w0'][dirn] + jnp.tanh(w1s[dirn]) @ lp['rw_w2'][dirn])
        decay = jnp.exp(-jnp.exp(-jax.nn.softplus(-wlog) - 0.5))
        a = jax.nn.sigmoid(hd(lp['rw_a0'][dirn] + a1s[dirn] @ lp['rw_a2'][dirn]))
        kd = kh * (1.0 + (a - 1.0) * ka)
        s_fin, y = _rwkv_scan(rh, decay, kd, vh, kk, a, s0s[dirn].astype(f32), dirn == 1)
        y_sum = y_sum + y
        bonus = bonus + jnp.sum(rh * kd * rk, axis=-1, keepdims=True) * vh
        finals.append(s_fin.astype(dt))
    mu = jnp.mean(y_sum, axis=-1, keepdims=True)
    var = jnp.mean(jnp.square(y_sum - mu), axis=-1, keepdims=True)
    yn = ((y_sum - mu) * lax.rsqrt(var + GN_EPS)).reshape(b, t, RW_DIM)
    yn = yn * lp['rw_gn_w'].astype(f32) + lp['rw_gn_b'].astype(f32)
    g = jax.nn.sigmoid(g1) @ lp['rw_g2']
    out = ((yn + bonus.reshape(b, t, RW_DIM)) * g.astype(f32)).astype(dt)
    return out, finals[0], finals[1]


def _mla(dq, dkv, kr, lp, cache):
    b, t, _ = dq.shape
    cq = _rmsnorm(dq, lp['mla_qn_g'])
    ckv = _rmsnorm(dkv, lp['mla_kvn_g'])
    q = (cq @ lp['mla_w_uq']).reshape(b, t, MLA_HEADS, NOPE_DIM + ROPE_DIM).transpose(0, 2, 1, 3)

    def expand_kv(c_kv, k_pe):
        kv = (c_kv @ lp['mla_w_ukv']).reshape(b, -1, MLA_HEADS, NOPE_DIM + V_DIM).transpose(0, 2, 1, 3)
        k_pe = jnp.broadcast_to(k_pe[:, None], (b, MLA_HEADS) + k_pe.shape[1:])
        return jnp.concatenate([kv[..., :NOPE_DIM], k_pe], axis=-1), kv[..., NOPE_DIM:]

    if cache is None:
        k, v = expand_kv(ckv, kr)
        o = _block_attention(q, k, v, MLA_SCALE)
        new = (ckv, kr)
    else:
        ckv_ctx, kr_ctx = cache
        cos, sin = _axial_rope(t, ROPE_DIM)
        q = jnp.concatenate([q[..., :NOPE_DIM], _rope(q[..., NOPE_DIM:], cos, sin)], axis=-1)
        k_lat, v_lat = expand_kv(ckv, _rope(kr, cos, sin))
        k_ctx, v_ctx = expand_kv(ckv_ctx, kr_ctx)
        o = _block_attention(q, jnp.concatenate([k_ctx, k_lat], axis=2),
                             jnp.concatenate([v_ctx, v_lat], axis=2), MLA_SCALE)
        new = None
    return _merge(o), new


def _mixer(h, lp, cache):
    b, t, _ = h.shape
    (na_q, na_k, na_v, rw_r, rw_k, rw_v, w1f, w1b, a1f, a1b, g1,
     mla_dq, mla_dkv, mla_kr, gate) = _split_cols(h @ lp['w_in'])
    q, k, v = _heads(na_q, NA_HEADS), _heads(na_k, NA_HEADS), _heads(na_v, NA_HEADS)
    if cache is None:
        o_na = _block_attention(q, k, v, HEAD_DIM ** -0.5)
        zero = jnp.zeros((b, RW_HEADS, HEAD_DIM, HEAD_DIM), jnp.float32)
        s0s = (zero, zero)
        mla_cache = None
    else:
        na_kc, na_vc, ckv_c, kr_c, s_f, s_b = cache
        o_na = _na_latent(q, k, v, na_kc, na_vc, lp['na_rpb'])
        s0s = (s_f, s_b)
        mla_cache = (ckv_c, kr_c)
    o_rw, s_f_new, s_b_new = _rwkv(rw_r, rw_k, rw_v, (w1f, w1b), (a1f, a1b), g1, lp, s0s)
    o_mla, mla_new = _mla(mla_dq, mla_dkv, mla_kr, lp, mla_cache)
    o = jnp.stack([_merge(o_na), o_rw, o_mla], axis=2)
    gates = jax.nn.sigmoid(gate.reshape(b, t, N_BRANCH, D_MODEL))
    out = jnp.sum(gates * jnp.einsum('btnc,ncd->btnd', o, lp['w_br']), axis=2) @ lp['w_o']
    if cache is None:
        return out, (k, v, mla_new[0], mla_new[1], s_f_new, s_b_new)
    return out, None


def _layer(x, cond, lp, cache):
    mod = (jax.nn.silu(cond) @ lp['ada_w'] + lp['ada_b'])[:, None, :]
    sh1, sc1, g1, shm, scm, gm, sh2, sc2, g2 = jnp.split(mod, N_MOD, axis=-1)
    h = _rmsnorm(x, lp['norm_g'][0]) * (1 + sc1) + sh1
    x = x + 0.5 * g1 * _swiglu(h, lp['ffn_wg'][0], lp['ffn_wu'][0], lp['ffn_wd'][0])
    h = _rmsnorm(x, lp['norm_g'][1]) * (1 + scm) + shm
    mix, new_cache = _mixer(h, lp, cache)
    x = x + gm * mix
    h = _rmsnorm(x, lp['norm_g'][2]) * (1 + sc2) + sh2
    x = x + 0.5 * g2 * _swiglu(h, lp['ffn_wg'][1], lp['ffn_wu'][1], lp['ffn_wd'][1])
    return x, new_cache


def setup_inputs(seed: int = 0) -> dict:
    key = jax.random.key(seed)
    ks = iter(jax.random.split(key, 48))
    d = D_MODEL

    def nrm(shape, scale):
        return jax.random.normal(next(ks), shape, jnp.float32) * scale

    return {
        'x_prompt': nrm((BATCH, SEQ, d), 1.0),
        'x_sample': nrm((DEC_BATCH, DEC_SEQ, d), 1.0),
        'c': nrm((DEC_BATCH, d), 1.0),
        'cache_na_k': nrm((DEC_BATCH, DEPTH, NA_HEADS, PAST_LEN, HEAD_DIM), 1.0),
        'cache_na_v': nrm((DEC_BATCH, DEPTH, NA_HEADS, PAST_LEN, HEAD_DIM), 1.0),
        'cache_mla_ckv': nrm((DEC_BATCH, DEPTH, PAST_LEN, KV_LORA), 1.0),
        'cache_mla_krope': nrm((DEC_BATCH, DEPTH, PAST_LEN, ROPE_DIM), 1.0),
        'state_rwkv_fwd': nrm((DEC_BATCH, DEPTH, RW_HEADS, HEAD_DIM, HEAD_DIM), 0.5),
        'state_rwkv_bwd': nrm((DEC_BATCH, DEPTH, RW_HEADS, HEAD_DIM, HEAD_DIM), 0.5),
        'c_ctx': nrm((d,), 1.0),
        'ada_w': nrm((DEPTH, d, N_MOD * d), 0.5 * d ** -0.5),
        'ada_b': nrm((DEPTH, N_MOD * d), 0.02),
        'norm_g': 1.0 + nrm((DEPTH, 3, d), 0.02),
        'ffn_wg': nrm((DEPTH, 2, d, D_FF), d ** -0.5),
        'ffn_wu': nrm((DEPTH, 2, d, D_FF), d ** -0.5),
        'ffn_wd': nrm((DEPTH, 2, D_FF, d), D_FF ** -0.5),
        'w_in': nrm((DEPTH, d, IN_COLS), d ** -0.5),
        'na_rpb': nrm((DEPTH, NA_HEADS, 2 * NA_WIN_H - 1, 2 * NA_WIN_W - 1), 0.1),
        'rw_w0': jax.random.uniform(next(ks), (DEPTH, 2, RW_DIM), jnp.float32, -6.0, -1.0),
        'rw_w2': nrm((DEPTH, 2, DECAY_LORA, RW_DIM), 0.1),
        'rw_a0': nrm((DEPTH, 2, RW_DIM), 0.1),
        'rw_a2': nrm((DEPTH, 2, ICLR_LORA, RW_DIM), 0.5 * ICLR_LORA ** -0.5),
        'rw_g2': nrm((DEPTH, GATE_LORA, RW_DIM), GATE_LORA ** -0.5),
        'rw_kk': 1.0 + nrm((DEPTH, RW_DIM), 0.1),
        'rw_ka': 1.0 + nrm((DEPTH, RW_DIM), 0.1),
        'rw_rk': nrm((DEPTH, RW_HEADS, HEAD_DIM), 0.1),
        'rw_gn_w': 1.0 + nrm((DEPTH, RW_DIM), 0.02),
        'rw_gn_b': nrm((DEPTH, RW_DIM), 0.02),
        'mla_qn_g': 1.0 + nrm((DEPTH, Q_LORA), 0.02),
        'mla_kvn_g': 1.0 + nrm((DEPTH, KV_LORA), 0.02),
        'mla_w_uq': nrm((DEPTH, Q_LORA, MLA_HEADS * (NOPE_DIM + ROPE_DIM)), Q_LORA ** -0.5),
        'mla_w_ukv': nrm((DEPTH, KV_LORA, MLA_HEADS * (NOPE_DIM + V_DIM)), KV_LORA ** -0.5),
        'w_br': nrm((DEPTH, N_BRANCH, BRANCH_DIM, d), BRANCH_DIM ** -0.5),
        'w_o': nrm((DEPTH, d, d), d ** -0.5),
        'final_g': 1.0 + nrm((d,), 0.02),
    }


def reference(x_prompt, x_sample, c, cache_na_k, cache_na_v, cache_mla_ckv, cache_mla_krope,
              state_rwkv_fwd, state_rwkv_bwd, c_ctx, ada_w, ada_b, norm_g, ffn_wg, ffn_wu, ffn_wd,
              w_in, na_rpb, rw_w0, rw_w2, rw_a0, rw_a2, rw_g2, rw_kk, rw_ka, rw_rk, rw_gn_w, rw_gn_b,
              mla_qn_g, mla_kvn_g, mla_w_uq, mla_w_ukv, w_br, w_o, final_g):
    xp, xs = x_prompt, x_sample
    ctx_cond = c_ctx[None, :]
    new_caches = []
    for l in range(DEPTH):
        lp = {
            'ada_w': ada_w[l], 'ada_b': ada_b[l], 'norm_g': norm_g[l],
            'ffn_wg': ffn_wg[l], 'ffn_wu': ffn_wu[l], 'ffn_wd': ffn_wd[l],
            'w_in': w_in[l], 'na_rpb': na_rpb[l],
            'rw_w0': rw_w0[l], 'rw_w2': rw_w2[l], 'rw_a0': rw_a0[l], 'rw_a2': rw_a2[l],
            'rw_g2': rw_g2[l], 'rw_kk': rw_kk[l], 'rw_ka': rw_ka[l], 'rw_rk': rw_rk[l],
            'rw_gn_w': rw_gn_w[l], 'rw_gn_b': rw_gn_b[l],
            'mla_qn_g': mla_qn_g[l], 'mla_kvn_g': mla_kvn_g[l],
            'mla_w_uq': mla_w_uq[l], 'mla_w_ukv': mla_w_ukv[l],
            'w_br': w_br[l], 'w_o': w_o[l],
        }
        xp, cache_l = _layer(xp, ctx_cond, lp, None)
        new_caches.append(cache_l)
        cache_in = (cache_na_k[:, l], cache_na_v[:, l], cache_mla_ckv[:, l], cache_mla_krope[:, l],
                    state_rwkv_fwd[:, l], state_rwkv_bwd[:, l])
        xs, _ = _layer(xs, c, lp, cache_in)
    y_prompt = _rmsnorm(xp, final_g)
    y_sample = _rmsnorm(xs, final_g)
    new_na_k = jnp.stack([cl[0] for cl in new_caches], axis=1)
    new_na_v = jnp.stack([cl[1] for cl in new_caches], axis=1)
    new_mla_ckv = jnp.stack([cl[2] for cl in new_caches], axis=1)
    new_mla_krope = jnp.stack([cl[3] for cl in new_caches], axis=1)
    new_rwkv_fwd = jnp.stack([cl[4] for cl in new_caches], axis=1)
    new_rwkv_bwd = jnp.stack([cl[5] for cl in new_caches], axis=1)
    return (y_prompt, y_sample, new_na_k, new_na_v, new_mla_ckv, new_mla_krope, new_rwkv_fwd, new_rwkv_bwd)
```

```python
import functools

import numpy as np
import jax
import jax.numpy as jnp
from jax import lax
from jax.experimental import pallas as pl
from jax.experimental.pallas import tpu as pltpu

F32 = jnp.float32
BF16 = jnp.bfloat16

D_MODEL = 1024
BATCH = 32
SEQ = 256
DEPTH = 2
DEC_BATCH = 8
DEC_SEQ = 1024
PAST_LEN = 256
GRID_W = 64
HEAD_DIM = 64
NA_HEADS = 4
NA_WIN_H = 8
NA_WIN_W = 16
RW_HEADS = 4
RW_DIM = RW_HEADS * HEAD_DIM
DECAY_LORA = 64
ICLR_LORA = 64
GATE_LORA = 128
GN_EPS = 64e-5
MLA_HEADS = 4
Q_LORA = 256
KV_LORA = 128
NOPE_DIM = 64
ROPE_DIM = 32
V_DIM = 64
MLA_SCALE = (NOPE_DIM + ROPE_DIM) ** -0.5
ROPE_BASE = 10000.0
N_BRANCH = 3
BRANCH_DIM = 256
D_FF = 2816
N_MOD = 9
NORM_EPS = 1e-6
NEG_INF = -1e30

N_CTX = BATCH * SEQ
N_LAT = DEC_BATCH * DEC_SEQ
N_TOK = N_CTX + N_LAT
N_COND = 16

LANES = 128
VMEM_LIMIT = 48 * 1024 * 1024

TM = 1024
TF = 256
SCAN_TC = 32


def _cond_row(i, tm):
    return jnp.where(i * tm < N_CTX, 0, 1 + (i * tm - N_CTX) // DEC_SEQ)


def _sigmoid(x):
    return 1.0 / (1.0 + jnp.exp(-x))


def _params(*sem):
    return pltpu.CompilerParams(dimension_semantics=sem, vmem_limit_bytes=VMEM_LIMIT)


def _mm_kernel(a_ref, w_ref, *rest, act, has_bias):
    if has_bias:
        b_ref, o_ref = rest
    else:
        (o_ref,) = rest
    a = a_ref[...]
    if act is not None:
        a = act(a.astype(F32))
    acc = jnp.dot(a.astype(BF16), w_ref[...].astype(BF16), preferred_element_type=F32)
    if has_bias:
        acc = acc + b_ref[...]
    o_ref[...] = acc.astype(o_ref.dtype)


def _mm(a, w, bias=None, act=None, out_dtype=F32, a_cols=None, tn=None, name="mm"):
    m = a.shape[0]
    k, n = w.shape
    kb = 0
    if a_cols is not None:
        assert a_cols[0] == k
        kb = a_cols[1]
    else:
        assert a.shape[1] == k
    tm = min(m, TM)
    tn = n if tn is None else tn
    assert m % tm == 0 and n % tn == 0
    in_specs = [pl.BlockSpec((tm, k), lambda i, j: (i, kb)),
                pl.BlockSpec((k, tn), lambda i, j: (0, j))]
    args = [a, w]
    if bias is not None:
        in_specs.append(pl.BlockSpec((1, tn), lambda i, j: (0, j)))
        args.append(bias.reshape(1, n).astype(F32))
    return pl.pallas_call(
        functools.partial(_mm_kernel, act=act, has_bias=bias is not None),
        out_shape=jax.ShapeDtypeStruct((m, n), out_dtype),
        grid=(m // tm, n // tn),
        in_specs=in_specs,
        out_specs=pl.BlockSpec((tm, tn), lambda i, j: (i, j)),
        compiler_params=_params("parallel", "arbitrary"),
        name=name,
    )(*args)


def _mod_kernel(c_ref, w_ref, b_ref, o_ref):
    c = c_ref[...]
    a = (c * _sigmoid(c)).astype(BF16)
    o_ref[...] = jnp.dot(a, w_ref[...].astype(BF16), preferred_element_type=F32) + b_ref[...]


def _modulation(cond, ada_w, ada_b):
    n = N_MOD * D_MODEL
    tn = D_MODEL
    out = pl.pallas_call(
        _mod_kernel,
        out_shape=jax.ShapeDtypeStruct((DEPTH, N_COND, n), F32),
        grid=(DEPTH, n // tn),
        in_specs=[pl.BlockSpec((N_COND, D_MODEL), lambda l, j: (0, 0)),
                  pl.BlockSpec((None, D_MODEL, tn), lambda l, j: (l, 0, j)),
                  pl.BlockSpec((None, 1, tn), lambda l, j: (l, 0, j))],
        out_specs=pl.BlockSpec((None, N_COND, tn), lambda l, j: (l, 0, j)),
        compiler_params=_params("parallel", "arbitrary"),
        name="modulation",
    )(cond, ada_w, ada_b.reshape(DEPTH, 1, n))
    return out.reshape(DEPTH, N_COND, 1, n)


def _norm_kernel(x_ref, g_ref, *rest, has_mod):
    x = x_ref[...].astype(F32)
    y = x * lax.rsqrt(jnp.mean(x * x, axis=-1, keepdims=True) + NORM_EPS) * g_ref[...]
    if has_mod:
        sh_ref, sc_ref, o_ref = rest
        y = y * (1.0 + sc_ref[...]) + sh_ref[...]
    else:
        (o_ref,) = rest
    o_ref[...] = y.astype(o_ref.dtype)


def _norm(x, g, out_dtype, x_cols=None, mod=None, name="norm"):
    m = x.shape[0]
    width, cb = (x.shape[1], 0) if x_cols is None else x_cols
    tm = min(m, TM)
    in_specs = [pl.BlockSpec((tm, width), lambda i: (i, cb)),
                pl.BlockSpec((1, width), lambda i: (0, 0))]
    args = [x, g.reshape(1, width).astype(F32)]
    if mod is not None:
        mod_l, c_sh, c_sc = mod
        in_specs += [pl.BlockSpec((None, 1, width), lambda i: (_cond_row(i, tm), 0, c_sh)),
                     pl.BlockSpec((None, 1, width), lambda i: (_cond_row(i, tm), 0, c_sc))]
        args += [mod_l, mod_l]
    return pl.pallas_call(
        functools.partial(_norm_kernel, has_mod=mod is not None),
        out_shape=jax.ShapeDtypeStruct((m, width), out_dtype),
        grid=(m // tm,),
        in_specs=in_specs,
        out_specs=pl.BlockSpec((tm, width), lambda i: (i, 0)),
        compiler_params=_params("parallel"),
        name=name,
    )(*args)


def _ffn_kernel(x_ref, g_ref, sh_ref, sc_ref, gt_ref, wg_ref, wu_ref, wd_ref, o_ref, h_scr, acc_scr):
    f = pl.program_id(1)

    @pl.when(f == 0)
    def _():
        x = x_ref[...]
        y = x * lax.rsqrt(jnp.mean(x * x, axis=-1, keepdims=True) + NORM_EPS) * g_ref[...]
        h_scr[...] = (y * (1.0 + sc_ref[...]) + sh_ref[...]).astype(BF16)
        acc_scr[...] = jnp.zeros_like(acc_scr)

    h = h_scr[...]
    gg = jnp.dot(h, wg_ref[...], preferred_element_type=F32)
    uu = jnp.dot(h, wu_ref[...], preferred_element_type=F32)
    a = (gg * _sigmoid(gg)) * uu
    acc_scr[...] += jnp.dot(a.astype(BF16), wd_ref[...], preferred_element_type=F32)

    @pl.when(f == pl.num_programs(1) - 1)
    def _():
        o_ref[...] = x_ref[...] + 0.5 * gt_ref[...] * acc_scr[...]


def _ffn(x, norm_g, mod_l, chunk0, wg, wu, wd):
    def mod_spec(c):
        return pl.BlockSpec((None, 1, D_MODEL), lambda i, f: (_cond_row(i, TM), 0, c))

    return pl.pallas_call(
        _ffn_kernel,
        out_shape=jax.ShapeDtypeStruct((N_TOK, D_MODEL), F32),
        grid=(N_TOK // TM, D_FF // TF),
        in_specs=[pl.BlockSpec((TM, D_MODEL), lambda i, f: (i, 0)),
                  pl.BlockSpec((1, D_MODEL), lambda i, f: (0, 0)),
                  mod_spec(chunk0), mod_spec(chunk0 + 1), mod_spec(chunk0 + 2),
                  pl.BlockSpec((D_MODEL, TF), lambda i, f: (0, f)),
                  pl.BlockSpec((D_MODEL, TF), lambda i, f: (0, f)),
                  pl.BlockSpec((TF, D_MODEL), lambda i, f: (f, 0))],
        out_specs=pl.BlockSpec((TM, D_MODEL), lambda i, f: (i, 0)),
        scratch_shapes=[pltpu.VMEM((TM, D_MODEL), BF16), pltpu.VMEM((TM, D_MODEL), F32)],
        compiler_params=_params("parallel", "arbitrary"),
        name="ffn",
    )(x, norm_g.reshape(1, D_MODEL), mod_l, mod_l, mod_l, wg, wu, wd)


def _attn_kernel(q_ref, k_ref, v_ref, o_ref, *, scale):
    q = q_ref[...].astype(BF16)
    k = k_ref[...].astype(BF16)
    s = lax.dot_general(q, k, (((1,), (1,)), ((), ())), preferred_element_type=F32) * scale
    m = jnp.max(s, axis=-1, keepdims=True)
    p = jnp.exp(s - m)
    l = jnp.sum(p, axis=-1, keepdims=True)
    o = jnp.dot(p.astype(BF16), v_ref[...].astype(BF16), preferred_element_type=F32)
    o_ref[...] = o / l


def _attention(q, k, v, scale, name):
    g, tq, dk = q.shape
    tk, dv = v.shape[1], v.shape[2]
    bq = min(tq, 256)
    return pl.pallas_call(
        functools.partial(_attn_kernel, scale=scale),
        out_shape=jax.ShapeDtypeStruct((g, tq, dv), F32),
        grid=(g, tq // bq),
        in_specs=[pl.BlockSpec((None, bq, dk), lambda i, j: (i, j, 0)),
                  pl.BlockSpec((None, tk, dk), lambda i, j: (i, 0, 0)),
                  pl.BlockSpec((None, tk, dv), lambda i, j: (i, 0, 0))],
        out_specs=pl.BlockSpec((None, bq, dv), lambda i, j: (i, j, 0)),
        compiler_params=_params("parallel", "arbitrary"),
        name=name,
    )(q, k, v)


_LAT_ROWS = DEC_SEQ // GRID_W
_NA_KH = min(NA_WIN_H, _LAT_ROWS)
_NA_LOCAL = _NA_KH * GRID_W


def _na_row_lo(qr):
    return min(max(qr - NA_WIN_H // 2, 0), _LAT_ROWS - _NA_KH)


def _na_lat_kernel(q_ref, k_ref, v_ref, kc_ref, vc_ref, bias_ref, o_ref):
    scale = HEAD_DIM ** -0.5
    kc = kc_ref[...].astype(BF16)
    vc = vc_ref[...].astype(BF16)
    nt = (((1,), (1,)), ((), ()))
    for qr in range(_LAT_ROWS):
        lo = _na_row_lo(qr) * GRID_W
        q = q_ref[qr * GRID_W:(qr + 1) * GRID_W, :].astype(BF16)
        kl = k_ref[lo:lo + _NA_LOCAL, :].astype(BF16)
        vl = v_ref[lo:lo + _NA_LOCAL, :].astype(BF16)
        s_loc = lax.dot_general(q, kl, nt, preferred_element_type=F32) * scale + bias_ref[qr]
        s_ctx = lax.dot_general(q, kc, nt, preferred_element_type=F32) * scale
        m = jnp.maximum(jnp.max(s_loc, axis=-1, keepdims=True), jnp.max(s_ctx, axis=-1, keepdims=True))
        p_loc = jnp.exp(s_loc - m)
        p_ctx = jnp.exp(s_ctx - m)
        l = jnp.sum(p_loc, axis=-1, keepdims=True) + jnp.sum(p_ctx, axis=-1, keepdims=True)
        o = (jnp.dot(p_loc.astype(BF16), vl, preferred_element_type=F32)
             + jnp.dot(p_ctx.astype(BF16), vc, preferred_element_type=F32))
        o_ref[qr * GRID_W:(qr + 1) * GRID_W, :] = o / l


def _na_bias(rpb):
    qc = np.arange(GRID_W)[:, None]
    kc = np.arange(GRID_W)[None, :]
    col_lo = np.clip(qc - NA_WIN_W // 2, 0, GRID_W - NA_WIN_W)
    ok = (kc >= col_lo) & (kc < col_lo + NA_WIN_W)
    cidx = np.clip(kc - qc + NA_WIN_W - 1, 0, 2 * NA_WIN_W - 2)
    row_lo = np.array([_na_row_lo(r) for r in range(_LAT_ROWS)])
    ridx = row_lo[:, None] + np.arange(_NA_KH)[None, :] - np.arange(_LAT_ROWS)[:, None] + NA_WIN_H - 1
    b = rpb[:, ridx[:, None, :, None], cidx[None, :, None, :]]
    b = jnp.where(jnp.asarray(ok)[None, None, :, None, :], b, NEG_INF)
    return b.reshape(NA_HEADS, _LAT_ROWS, GRID_W, _NA_LOCAL).astype(F32)


def _na_latent(q, k, v, kc, vc, bias):
    def hs(t):
        return pl.BlockSpec((None, None, t, HEAD_DIM), lambda b, h: (b, h, 0, 0))

    return pl.pallas_call(
        _na_lat_kernel,
        out_shape=jax.ShapeDtypeStruct((DEC_BATCH, NA_HEADS, DEC_SEQ, HEAD_DIM), F32),
        grid=(DEC_BATCH, NA_HEADS),
        in_specs=[hs(DEC_SEQ), hs(DEC_SEQ), hs(DEC_SEQ), hs(PAST_LEN), hs(PAST_LEN),
                  pl.BlockSpec((None, _LAT_ROWS, GRID_W, _NA_LOCAL), lambda b, h: (h, 0, 0, 0))],
        out_specs=hs(DEC_SEQ),
        compiler_params=_params("parallel", "arbitrary"),
        name="na_latent",
    )(q, k, v, kc, vc, bias)


def _head_sum(x, ones_ref):
    return jnp.dot(x, ones_ref[...], preferred_element_type=F32, precision=lax.Precision.HIGHEST)


def _rw_pre_kernel(r_ref, k_ref, v_ref, wf_ref, wb_ref, af_ref, ab_ref, kkw_ref, ka_ref, rk_ref, ones_ref,
                   dwf_ref, dwb_ref, kdf_ref, kdb_ref, bf_ref, bb_ref, kk_ref, bonus_ref):
    k = k_ref[...]
    kk = k * kkw_ref[...]
    nrm = jnp.sqrt(_head_sum(kk * kk, ones_ref))
    kk = kk / jnp.maximum(nrm, 1e-12)
    kk_ref[...] = kk
    kd_sum = None
    for wl_ref, al_ref, dw_ref, kd_ref, b_ref in ((wf_ref, af_ref, dwf_ref, kdf_ref, bf_ref),
                                                  (wb_ref, ab_ref, dwb_ref, kdb_ref, bb_ref)):
        dw_ref[...] = jnp.exp(-float(np.exp(-0.5)) * _sigmoid(wl_ref[...]))
        a = _sigmoid(al_ref[...])
        kd = k * (1.0 + (a - 1.0) * ka_ref[...])
        kd_ref[...] = kd
        b_ref[...] = kk * a
        kd_sum = kd if kd_sum is None else kd_sum + kd
    bonus_ref[...] = _head_sum(r_ref[...] * rk_ref[...] * kd_sum, ones_ref) * v_ref[...]


def _rw_pre(z_rw, wa, kk_w, ka, rk, ones):
    tm = 512
    def col(c):
        return pl.BlockSpec((tm, RW_DIM), lambda i: (i, c))
    vec = pl.BlockSpec((1, RW_DIM), lambda i: (0, 0))
    out = jax.ShapeDtypeStruct((N_TOK, RW_DIM), F32)
    return pl.pallas_call(
        _rw_pre_kernel,
        out_shape=(out,) * 8,
        grid=(N_TOK // tm,),
        in_specs=[col(0), col(1), col(2), col(0), col(1), col(2), col(3), vec, vec, vec,
                  pl.BlockSpec((RW_DIM, RW_DIM), lambda i: (0, 0))],
        out_specs=(pl.BlockSpec((tm, RW_DIM), lambda i: (i, 0)),) * 8,
        compiler_params=_params("parallel"),
        name="rwkv_pre",
    )(z_rw, z_rw, z_rw, wa, wa, wa, wa, kk_w.reshape(1, RW_DIM), ka.reshape(1, RW_DIM),
      rk.reshape(1, RW_DIM), ones)


def _rw_scan_kernel(w_ref, kk_ref, b_ref, kd_ref, r_ref, v_ref, s0_ref, y_ref, sf_ref, s_scr):
    c = pl.program_id(1)

    @pl.when(c == 0)
    def _():
        s_scr[...] = s0_ref[...]

    def step(t, carry):
        w = w_ref[t]
        kk = kk_ref[t]
        b = b_ref[t]
        kd = kd_ref[t]
        r = r_ref[t]
        for vi in range(HEAD_DIM):
            s = s_scr[vi]
            sa = jnp.sum(s * kk, axis=0, keepdims=True)
            s = s * w - sa * b + v_ref[t, pl.ds(vi, 1), :] * kd
            s_scr[vi] = s
            y_ref[t, pl.ds(vi, 1), :] = jnp.sum(s * r, axis=0, keepdims=True)
        return carry

    lax.fori_loop(0, SCAN_TC, step, 0)

    @pl.when(c == pl.num_programs(1) - 1)
    def _():
        sf_ref[...] = s_scr[...]


def _rw_scan(w, kk, b, kd, r, v, s0):
    t, _, lanes = w.shape
    seq = pl.BlockSpec((SCAN_TC, HEAD_DIM, LANES), lambda g, c: (c, 0, g))
    st = pl.BlockSpec((HEAD_DIM, HEAD_DIM, LANES), lambda g, c: (0, 0, g))
    return pl.pallas_call(
        _rw_scan_kernel,
        out_shape=(jax.ShapeDtypeStruct((t, HEAD_DIM, lanes), F32),
                   jax.ShapeDtypeStruct((HEAD_DIM, HEAD_DIM, lanes), F32)),
        grid=(lanes // LANES, t // SCAN_TC),
        in_specs=[seq] * 6 + [st],
        out_specs=(seq, st),
        scratch_shapes=[pltpu.VMEM((HEAD_DIM, HEAD_DIM, LANES), F32)],
        compiler_params=_params("parallel", "arbitrary"),
        name="rwkv_scan",
    )(w, kk, b, kd, r, v, s0)


def _rw_post_kernel(yf_ref, yb_ref, bonus_ref, g_ref, gw_ref, gb_ref, ones_ref, o_ref):
    y = yf_ref[...] + yb_ref[...]
    inv = 1.0 / HEAD_DIM
    mu = _head_sum(y, ones_ref) * inv
    d = y - mu
    var = _head_sum(d * d, ones_ref) * inv
    yn = d * lax.rsqrt(var + GN_EPS) * gw_ref[...] + gb_ref[...]
    o_ref[...] = (yn + bonus_ref[...]) * g_ref[...]


def _rw_post(y_f, y_b, bonus, g, gn_w, gn_b, ones):
    tm = 1024
    row = pl.BlockSpec((tm, RW_DIM), lambda i: (i, 0))
    vec = pl.BlockSpec((1, RW_DIM), lambda i: (0, 0))
    return pl.pallas_call(
        _rw_post_kernel,
        out_shape=jax.ShapeDtypeStruct((N_TOK, RW_DIM), F32),
        grid=(N_TOK // tm,),
        in_specs=[row, row, row, row, vec, vec, pl.BlockSpec((RW_DIM, RW_DIM), lambda i: (0, 0))],
        out_specs=row,
        compiler_params=_params("parallel"),
        name="rwkv_post",
    )(y_f, y_b, bonus, g, gn_w.reshape(1, RW_DIM), gn_b.reshape(1, RW_DIM), ones)


def _to_scan(x, nb, t, reverse):
    x = x.reshape(nb, t, RW_HEADS, HEAD_DIM)
    if reverse:
        x = x[:, ::-1]
    return x.transpose(1, 3, 0, 2).reshape(t, HEAD_DIM, nb * RW_HEADS)


def _from_scan(y, nb, t, reverse):
    y = y.reshape(t, HEAD_DIM, nb, RW_HEADS).transpose(2, 0, 3, 1)
    if reverse:
        y = y[:, ::-1]
    return y.reshape(nb * t, RW_DIM)


def _rw_scan_part(ops, nb, t, s0_f, s0_b):
    w_f, w_b, kk, b_f, b_b, kd_f, kd_b, r, v = ops
    n = nb * RW_HEADS
    pad = (-2 * n) % LANES

    def both(xf, xb):
        z = jnp.concatenate([_to_scan(xf, nb, t, False), _to_scan(xb, nb, t, True)], axis=-1)
        return jnp.pad(z, ((0, 0), (0, 0), (0, pad))) if pad else z

    def state(s):
        return s.reshape(n, HEAD_DIM, HEAD_DIM).transpose(1, 2, 0)

    s0 = jnp.concatenate([state(s0_f), state(s0_b)], axis=-1)
    if pad:
        s0 = jnp.pad(s0, ((0, 0), (0, 0), (0, pad)))
    y, sf = _rw_scan(both(w_f, w_b), both(kk, kk), both(b_f, b_b), both(kd_f, kd_b), both(r, r), both(v, v), s0)
    y_f = _from_scan(y[:, :, :n], nb, t, False)
    y_b = _from_scan(y[:, :, n:2 * n], nb, t, True)

    def unstate(s):
        return s.transpose(2, 0, 1).reshape(nb, RW_HEADS, HEAD_DIM, HEAD_DIM)

    return y_f, y_b, unstate(sf[:, :, :n]), unstate(sf[:, :, n:2 * n])


def _merge_kernel(x_ref, gm_ref, ona_ref, orw_ref, omla_ref, g0_ref, g1_ref, g2_ref, wbr_ref, wo_ref, o_ref):
    m = None
    for i, (o_b, g_b) in enumerate(((ona_ref, g0_ref), (orw_ref, g1_ref), (omla_ref, g2_ref))):
        br = jnp.dot(o_b[...].astype(BF16), wbr_ref[i], preferred_element_type=F32)
        t = _sigmoid(g_b[...]) * br
        m = t if m is None else m + t
    mix = jnp.dot(m.astype(BF16), wo_ref[...], preferred_element_type=F32)
    o_ref[...] = x_ref[...] + gm_ref[...] * mix


def _merge(x, mod_l, o_na, o_rw, o_mla, gate, w_br, w_o):
    tm = 512
    row = pl.BlockSpec((tm, D_MODEL), lambda i: (i, 0))
    br = pl.BlockSpec((tm, BRANCH_DIM), lambda i: (i, 0))

    def gcol(c):
        return pl.BlockSpec((tm, D_MODEL), lambda i: (i, c))

    return pl.pallas_call(
        _merge_kernel,
        out_shape=jax.ShapeDtypeStruct((N_TOK, D_MODEL), F32),
        grid=(N_TOK // tm,),
        in_specs=[row, pl.BlockSpec((None, 1, D_MODEL), lambda i: (_cond_row(i, tm), 0, 5)),
                  br, br, br, gcol(0), gcol(1), gcol(2),
                  pl.BlockSpec((N_BRANCH, BRANCH_DIM, D_MODEL), lambda i: (0, 0, 0)),
                  pl.BlockSpec((D_MODEL, D_MODEL), lambda i: (0, 0))],
        out_specs=row,
        compiler_params=_params("parallel"),
        name="merge",
    )(x, mod_l, o_na, o_rw, o_mla, gate, gate, gate, w_br, w_o)


def _axial_rope(t_len, dim):
    n_freq = dim // 4
    inv = 1.0 / (ROPE_BASE ** (np.arange(n_freq) / n_freq))
    pos = np.arange(t_len)
    ang = np.concatenate([(pos // GRID_W)[:, None] * inv, (pos % GRID_W)[:, None] * inv], axis=-1)
    return jnp.asarray(np.cos(ang), F32), jnp.asarray(np.sin(ang), F32)


def _rope(x, cos, sin):
    half = x.shape[-1] // 2
    x1, x2 = x[..., :half], x[..., half:]
    return jnp.concatenate([x1 * cos - x2 * sin, x1 * sin + x2 * cos], axis=-1)


def _heads(z, nb, t, nh):
    return z.reshape(nb, t, nh, -1).transpose(0, 2, 1, 3)


def _unheads(o, nb, t):
    return o.transpose(0, 2, 1, 3).reshape(nb * t, -1)


def _lora_act(a):
    col = lax.broadcasted_iota(jnp.int32, a.shape, 1)
    return jnp.where(col < 2 * DECAY_LORA, jnp.tanh(a), a)


def _block_diag(blocks):
    rows = sum(b.shape[0] for b in blocks)
    cols = sum(b.shape[1] for b in blocks)
    out = jnp.zeros((rows, cols), blocks[0].dtype)
    r = c = 0
    for b in blocks:
        out = lax.dynamic_update_slice(out, b, (r, c))
        r += b.shape[0]
        c += b.shape[1]
    return out


def kernel(x_prompt, x_sample, c, cache_na_k, cache_na_v, cache_mla_ckv, cache_mla_krope, state_rwkv_fwd, state_rwkv_bwd, c_ctx, ada_w, ada_b, norm_g, ffn_wg, ffn_wu, ffn_wd, w_in, na_rpb, rw_w0, rw_w2, rw_a0, rw_a2, rw_g2, rw_kk, rw_ka, rw_rk, rw_gn_w, rw_gn_b, mla_qn_g, mla_kvn_g, mla_w_uq, mla_w_ukv, w_br, w_o, final_g):
    x = jnp.concatenate([x_prompt.reshape(N_CTX, D_MODEL), x_sample.reshape(N_LAT, D_MODEL)], axis=0)
    cond = jnp.concatenate([c_ctx[None, :], c, jnp.zeros((N_COND - 1 - DEC_BATCH, D_MODEL), F32)], axis=0)
    mod = _modulation(cond, ada_w, ada_b)

    ones = jnp.asarray(np.kron(np.eye(RW_HEADS), np.ones((HEAD_DIM, HEAD_DIM))), F32)
    cos, sin = _axial_rope(DEC_SEQ, ROPE_DIM)
    zero_state = jnp.zeros((BATCH, RW_HEADS, HEAD_DIM, HEAD_DIM), F32)

    col_rw = 3 * NA_HEADS * HEAD_DIM
    col_mla = col_rw + 3 * RW_DIM + 2 * DECAY_LORA + 2 * ICLR_LORA + GATE_LORA
    col_gate = col_mla + Q_LORA + KV_LORA + ROPE_DIM
    mla_cols = Q_LORA + KV_LORA + ROPE_DIM
    mla_pad = (-mla_cols) % LANES

    caches = []
    for l in range(DEPTH):
        mod_l = mod[l]
        wi = w_in[l].astype(BF16)
        w_na = wi[:, :col_rw]
        w_rw = wi[:, col_rw:col_mla]
        w_mla = jnp.pad(wi[:, col_mla:col_gate], ((0, 0), (0, mla_pad)))
        w_gate = wi[:, col_gate:]

        x = _ffn(x, norm_g[l, 0], mod_l, 0, ffn_wg[l, 0].astype(BF16), ffn_wu[l, 0].astype(BF16),
                 ffn_wd[l, 0].astype(BF16))
        h = _norm(x, norm_g[l, 1], BF16, mod=(mod_l, 3, 4), name="mixer_norm")

        z_na = _mm(h, w_na, name="proj_na")
        q, k, v = (z_na[:, i * 256:(i + 1) * 256] for i in range(3))
        qc, kc_, vc_ = (_heads(t[:N_CTX], BATCH, SEQ, NA_HEADS) for t in (q, k, v))
        g_ctx = BATCH * NA_HEADS
        o_na_ctx = _attention(qc.reshape(g_ctx, SEQ, HEAD_DIM), kc_.reshape(g_ctx, SEQ, HEAD_DIM),
                              vc_.reshape(g_ctx, SEQ, HEAD_DIM), HEAD_DIM ** -0.5, "na_ctx")
        o_na_ctx = _unheads(o_na_ctx.reshape(BATCH, NA_HEADS, SEQ, HEAD_DIM), BATCH, SEQ)
        ql, kl, vl = (_heads(t[N_CTX:], DEC_BATCH, DEC_SEQ, NA_HEADS) for t in (q, k, v))
        o_na_lat = _na_latent(ql, kl, vl, cache_na_k[:, l], cache_na_v[:, l], _na_bias(na_rpb[l]))
        o_na = jnp.concatenate([o_na_ctx, _unheads(o_na_lat, DEC_BATCH, DEC_SEQ)], axis=0)

        z_rw = _mm(h, w_rw, name="proj_rw")
        lora_w = _block_diag([rw_w2[l, 0], rw_w2[l, 1], rw_a2[l, 0], rw_a2[l, 1]]).astype(BF16)
        lora_b = jnp.concatenate([rw_w0[l, 0], rw_w0[l, 1], rw_a0[l, 0], rw_a0[l, 1]])
        wa = _mm(z_rw, lora_w, bias=lora_b, act=_lora_act, a_cols=(256, 3), name="rw_lora")
        g_rw = _mm(z_rw, rw_g2[l].astype(BF16), act=_sigmoid, a_cols=(GATE_LORA, 8), name="rw_gate")
        dw_f, dw_b, kd_f, kd_b, b_f, b_b, kk, bonus = _rw_pre(
            z_rw, wa, rw_kk[l], rw_ka[l], rw_rk[l].reshape(RW_DIM), ones)
        r_rw = z_rw[:, :RW_DIM]
        v_rw = z_rw[:, 2 * RW_DIM:3 * RW_DIM]
        ops = (dw_f, dw_b, kk, b_f, b_b, kd_f, kd_b, r_rw, v_rw)
        yc_f, yc_b, sc_f, sc_b = _rw_scan_part([t[:N_CTX] for t in ops], BATCH, SEQ, zero_state, zero_state)
        yl_f, yl_b, _, _ = _rw_scan_part([t[N_CTX:] for t in ops], DEC_BATCH, DEC_SEQ,
                                         state_rwkv_fwd[:, l], state_rwkv_bwd[:, l])
        o_rwkv = _rw_post(jnp.concatenate([yc_f, yl_f], axis=0), jnp.concatenate([yc_b, yl_b], axis=0),
                          bonus, g_rw, rw_gn_w[l], rw_gn_b[l], ones)

        z_mla = _mm(h, w_mla, name="proj_mla")
        cq = _norm(z_mla, mla_qn_g[l], BF16, x_cols=(Q_LORA, 0), name="mla_qnorm")
        ckv = _norm(z_mla, mla_kvn_g[l], F32, x_cols=(KV_LORA, Q_LORA // KV_LORA), name="mla_kvnorm")
        kr = z_mla[:, Q_LORA + KV_LORA:mla_cols]
        q_m = _mm(cq, mla_w_uq[l].astype(BF16), name="mla_uq")
        w_ukv = mla_w_ukv[l].astype(BF16)
        kv_m = _mm(ckv, w_ukv, name="mla_ukv")
        kv_cache = _mm(cache_mla_ckv[:, l].reshape(DEC_BATCH * PAST_LEN, KV_LORA), w_ukv, name="mla_ukv_cache")

        def split_kv(kv, nb, t):
            kv = _heads(kv, nb, t, MLA_HEADS)
            return kv[..., :NOPE_DIM], kv[..., NOPE_DIM:]

        def with_pe(k_nope, k_pe):
            k_pe = jnp.broadcast_to(k_pe[:, None], k_nope.shape[:3] + (ROPE_DIM,))
            return jnp.concatenate([k_nope, k_pe], axis=-1)

        q_c = _heads(q_m[:N_CTX], BATCH, SEQ, MLA_HEADS)
        kn_c, v_c = split_kv(kv_m[:N_CTX], BATCH, SEQ)
        kr_c = kr[:N_CTX].reshape(BATCH, SEQ, ROPE_DIM)
        k_c = with_pe(kn_c, kr_c)
        gm = BATCH * MLA_HEADS
        o_mla_ctx = _attention(q_c.reshape(gm, SEQ, -1), k_c.reshape(gm, SEQ, -1), v_c.reshape(gm, SEQ, V_DIM),
                               MLA_SCALE, "mla_ctx")
        o_mla_ctx = _unheads(o_mla_ctx.reshape(BATCH, MLA_HEADS, SEQ, V_DIM), BATCH, SEQ)
        q_l = _heads(q_m[N_CTX:], DEC_BATCH, DEC_SEQ, MLA_HEADS)
        q_l = jnp.concatenate([q_l[..., :NOPE_DIM], _rope(q_l[..., NOPE_DIM:], cos, sin)], axis=-1)
        kn_l, v_l = split_kv(kv_m[N_CTX:], DEC_BATCH, DEC_SEQ)
        k_l = with_pe(kn_l, _rope(kr[N_CTX:].reshape(DEC_BATCH, DEC_SEQ, ROPE_DIM), cos, sin))
        kn_p, v_p = split_kv(kv_cache, DEC_BATCH, PAST_LEN)
        k_p = with_pe(kn_p, cache_mla_krope[:, l])
        gl = DEC_BATCH * MLA_HEADS
        tk = PAST_LEN + DEC_SEQ
        o_mla_lat = _attention(q_l.reshape(gl, DEC_SEQ, -1),
                               jnp.concatenate([k_p, k_l], axis=2).reshape(gl, tk, -1),
                               jnp.concatenate([v_p, v_l], axis=2).reshape(gl, tk, V_DIM), MLA_SCALE, "mla_lat")
        o_mla_lat = _unheads(o_mla_lat.reshape(DEC_BATCH, MLA_HEADS, DEC_SEQ, V_DIM), DEC_BATCH, DEC_SEQ)
        o_mla = jnp.concatenate([o_mla_ctx, o_mla_lat], axis=0)

        gate = _mm(h, w_gate, tn=D_MODEL, name="proj_gate")
        x = _merge(x, mod_l, o_na, o_rwkv, o_mla, gate, w_br[l].astype(BF16), w_o[l].astype(BF16))

        x = _ffn(x, norm_g[l, 2], mod_l, 6, ffn_wg[l, 1].astype(BF16), ffn_wu[l, 1].astype(BF16),
                 ffn_wd[l, 1].astype(BF16))

        caches.append((kc_, vc_, ckv[:N_CTX].reshape(BATCH, SEQ, KV_LORA), kr_c, sc_f, sc_b))

    y = _norm(x, final_g, F32, name="final_norm")
    y_prompt = y[:N_CTX].reshape(BATCH, SEQ, D_MODEL)
    y_sample = y[N_CTX:].reshape(DEC_BATCH, DEC_SEQ, D_MODEL)
    outs = [jnp.stack([cl[i] for cl in caches], axis=1) for i in range(6)]
    return (y_prompt, y_sample, *outs)
```

```python
import functools

import numpy as np
import jax
import jax.numpy as jnp
from jax import lax
from jax.experimental import pallas as pl
from jax.experimental.pallas import tpu as pltpu

F32 = jnp.float32
BF16 = jnp.bfloat16

D_MODEL = 1024
BATCH = 32
SEQ = 256
DEPTH = 2
DEC_BATCH = 8
DEC_SEQ = 1024
PAST_LEN = 256
GRID_W = 64
HEAD_DIM = 64
NA_HEADS = 4
NA_WIN_H = 8
NA_WIN_W = 16
RW_HEADS = 4
RW_DIM = RW_HEADS * HEAD_DIM
DECAY_LORA = 64
ICLR_LORA = 64
GATE_LORA = 128
GN_EPS = 64e-5
MLA_HEADS = 4
Q_LORA = 256
KV_LORA = 128
NOPE_DIM = 64
ROPE_DIM = 32
V_DIM = 64
MLA_SCALE = (NOPE_DIM + ROPE_DIM) ** -0.5
ROPE_BASE = 10000.0
N_BRANCH = 3
BRANCH_DIM = 256
D_FF = 2816
N_MOD = 9
NORM_EPS = 1e-6
NEG_INF = -1e30

N_CTX = BATCH * SEQ
N_LAT = DEC_BATCH * DEC_SEQ
N_TOK = N_CTX + N_LAT
N_COND = 16

LANES = 128
VMEM_LIMIT = 48 * 1024 * 1024

TM = 1024
TF = 256
SCAN_TC = 32


def _cond_row(i, tm):
    return jnp.where(i * tm < N_CTX, 0, 1 + (i * tm - N_CTX) // DEC_SEQ)


def _sigmoid(x):
    return 1.0 / (1.0 + jnp.exp(-x))


def _params(*sem):
    return pltpu.CompilerParams(dimension_semantics=sem, vmem_limit_bytes=VMEM_LIMIT)


def _mm_kernel(a_ref, w_ref, *rest, act, has_bias):
    if has_bias:
        b_ref, o_ref = rest
    else:
        (o_ref,) = rest
    a = a_ref[...]
    if act is not None:
        a = act(a.astype(F32))
    acc = jnp.dot(a.astype(BF16), w_ref[...].astype(BF16), preferred_element_type=F32)
    if has_bias:
        acc = acc + b_ref[...]
    o_ref[...] = acc.astype(o_ref.dtype)


def _mm(a, w, bias=None, act=None, out_dtype=F32, a_cols=None, tn=None, name="mm"):
    m = a.shape[0]
    k, n = w.shape
    kb = 0
    if a_cols is not None:
        assert a_cols[0] == k
        kb = a_cols[1]
    else:
        assert a.shape[1] == k
    tm = min(m, TM)
    tn = n if tn is None else tn
    assert m % tm == 0 and n % tn == 0
    in_specs = [pl.BlockSpec((tm, k), lambda i, j: (i, kb)),
                pl.BlockSpec((k, tn), lambda i, j: (0, j))]
    args = [a, w]
    if bias is not None:
        in_specs.append(pl.BlockSpec((1, tn), lambda i, j: (0, j)))
        args.append(bias.reshape(1, n).astype(F32))
    return pl.pallas_call(
        functools.partial(_mm_kernel, act=act, has_bias=bias is not None),
        out_shape=jax.ShapeDtypeStruct((m, n), out_dtype),
        grid=(m // tm, n // tn),
        in_specs=in_specs,
        out_specs=pl.BlockSpec((tm, tn), lambda i, j: (i, j)),
        compiler_params=_params("parallel", "arbitrary"),
        name=name,
    )(*args)


def _mod_kernel(c_ref, w_ref, b_ref, o_ref):
    c = c_ref[...]
    a = (c * _sigmoid(c)).astype(BF16)
    o_ref[...] = jnp.dot(a, w_ref[...].astype(BF16), preferred_element_type=F32) + b_ref[...]


def _modulation(cond, ada_w, ada_b):
    n = N_MOD * D_MODEL
    tn = D_MODEL
    out = pl.pallas_call(
        _mod_kernel,
        out_shape=jax.ShapeDtypeStruct((DEPTH, N_COND, n), F32),
        grid=(DEPTH, n // tn),
        in_specs=[pl.BlockSpec((N_COND, D_MODEL), lambda l, j: (0, 0)),
                  pl.BlockSpec((None, D_MODEL, tn), lambda l, j: (l, 0, j)),
                  pl.BlockSpec((None, 1, tn), lambda l, j: (l, 0, j))],
        out_specs=pl.BlockSpec((None, N_COND, tn), lambda l, j: (l, 0, j)),
        compiler_params=_params("parallel", "arbitrary"),
        name="modulation",
    )(cond, ada_w, ada_b.reshape(DEPTH, 1, n))
    return out.reshape(DEPTH, N_COND, 1, n)


def _rms(x, g):
    return x * lax.rsqrt(jnp.mean(x * x, axis=-1, keepdims=True) + NORM_EPS) * g


def _norm_kernel(x_ref, g_ref, o_ref):
    o_ref[...] = _rms(x_ref[...].astype(F32), g_ref[...]).astype(o_ref.dtype)


def _norm(x, g, out_dtype, x_cols, name):
    m = x.shape[0]
    width, cb = x_cols
    tm = min(m, TM)
    return pl.pallas_call(
        _norm_kernel,
        out_shape=jax.ShapeDtypeStruct((m, width), out_dtype),
        grid=(m // tm,),
        in_specs=[pl.BlockSpec((tm, width), lambda i: (i, cb)),
                  pl.BlockSpec((1, width), lambda i: (0, 0))],
        out_specs=pl.BlockSpec((tm, width), lambda i: (i, 0)),
        compiler_params=_params("parallel"),
        name=name,
    )(x, g.reshape(1, width).astype(F32))


def _ffn_kernel(x_ref, g_ref, sh_ref, sc_ref, gt_ref, wg_ref, wu_ref, wd_ref, *rest, post):
    if post == "mod":
        g2_ref, sh2_ref, sc2_ref, o_ref, h2_ref, h_scr, acc_scr = rest
    elif post == "plain":
        g2_ref, y_ref, h_scr, acc_scr = rest
    else:
        o_ref, h_scr, acc_scr = rest
    f = pl.program_id(1)

    @pl.when(f == 0)
    def _():
        y = _rms(x_ref[...], g_ref[...])
        h_scr[...] = (y * (1.0 + sc_ref[...]) + sh_ref[...]).astype(BF16)
        acc_scr[...] = jnp.zeros_like(acc_scr)

    h = h_scr[...]
    gg = jnp.dot(h, wg_ref[...], preferred_element_type=F32)
    uu = jnp.dot(h, wu_ref[...], preferred_element_type=F32)
    a = (gg * _sigmoid(gg)) * uu
    acc_scr[...] += jnp.dot(a.astype(BF16), wd_ref[...], preferred_element_type=F32)

    @pl.when(f == pl.num_programs(1) - 1)
    def _():
        xn = x_ref[...] + 0.5 * gt_ref[...] * acc_scr[...]
        if post == "plain":
            y_ref[...] = _rms(xn, g2_ref[...])
        else:
            o_ref[...] = xn
        if post == "mod":
            h2_ref[...] = (_rms(xn, g2_ref[...]) * (1.0 + sc2_ref[...]) + sh2_ref[...]).astype(BF16)


def _ffn(x, norm_g, mod_l, chunk0, wg, wu, wd, post=None, post_g=None, post_chunk=None):
    def mod_spec(c):
        return pl.BlockSpec((None, 1, D_MODEL), lambda i, f: (_cond_row(i, TM), 0, c))

    row = pl.BlockSpec((TM, D_MODEL), lambda i, f: (i, 0))
    vec = pl.BlockSpec((1, D_MODEL), lambda i, f: (0, 0))
    in_specs = [row, vec, mod_spec(chunk0), mod_spec(chunk0 + 1), mod_spec(chunk0 + 2),
                pl.BlockSpec((D_MODEL, TF), lambda i, f: (0, f)),
                pl.BlockSpec((D_MODEL, TF), lambda i, f: (0, f)),
                pl.BlockSpec((TF, D_MODEL), lambda i, f: (f, 0))]
    args = [x, norm_g.reshape(1, D_MODEL), mod_l, mod_l, mod_l, wg, wu, wd]
    xs = jax.ShapeDtypeStruct((N_TOK, D_MODEL), F32)
    if post == "mod":
        in_specs += [vec, mod_spec(post_chunk), mod_spec(post_chunk + 1)]
        args += [post_g.reshape(1, D_MODEL), mod_l, mod_l]
        out_shape, out_specs = (xs, jax.ShapeDtypeStruct((N_TOK, D_MODEL), BF16)), (row, row)
    elif post == "plain":
        in_specs += [vec]
        args += [post_g.reshape(1, D_MODEL)]
        out_shape, out_specs = xs, row
    else:
        out_shape, out_specs = xs, row
    return pl.pallas_call(
        functools.partial(_ffn_kernel, post=post),
        out_shape=out_shape,
        grid=(N_TOK // TM, D_FF // TF),
        in_specs=in_specs,
        out_specs=out_specs,
        scratch_shapes=[pltpu.VMEM((TM, D_MODEL), BF16), pltpu.VMEM((TM, D_MODEL), F32)],
        compiler_params=_params("parallel", "arbitrary"),
        name="ffn",
    )(*args)


_NT = (((1,), (1,)), ((), ()))
_LAT_ROWS = DEC_SEQ // GRID_W
_NA_KH = min(NA_WIN_H, _LAT_ROWS)
_NA_LOCAL = _NA_KH * GRID_W
_NA_PAIRS = 2 * NA_WIN_H - 2


def _head_mask(width, h, group):
    lane = lax.broadcasted_iota(jnp.int32, (1, width), 1)
    return (lane >= h * group) & (lane < (h + 1) * group)


def _stack_heads(q, nh, group):
    return jnp.concatenate([jnp.where(_head_mask(q.shape[1], h, group), q, 0.0) for h in range(nh)], axis=0)


def _pick_heads(r, nh, group):
    t = r.shape[0] // nh
    out = r[0:t]
    for h in range(1, nh):
        out = jnp.where(_head_mask(r.shape[1], h, group), r[h * t:(h + 1) * t], out)
    return out


def _softmax2(s_a, s_b):
    m = jnp.maximum(jnp.max(s_a, axis=-1, keepdims=True), jnp.max(s_b, axis=-1, keepdims=True))
    p_a = jnp.exp(s_a - m)
    p_b = jnp.exp(s_b - m)
    return p_a, p_b, jnp.sum(p_a, axis=-1, keepdims=True) + jnp.sum(p_b, axis=-1, keepdims=True)


def _na_ctx_kernel(q_ref, k_ref, v_ref, o_ref):
    q_st = _stack_heads(q_ref[...], NA_HEADS, HEAD_DIM).astype(BF16)
    s = lax.dot_general(q_st, k_ref[...].astype(BF16), _NT, preferred_element_type=F32) * HEAD_DIM ** -0.5
    p = jnp.exp(s - jnp.max(s, axis=-1, keepdims=True))
    l = jnp.sum(p, axis=-1, keepdims=True)
    r = jnp.dot(p.astype(BF16), v_ref[...].astype(BF16), preferred_element_type=F32) / l
    o_ref[...] = _pick_heads(r, NA_HEADS, HEAD_DIM)


def _na_ctx(z_na):
    w = NA_HEADS * HEAD_DIM
    return pl.pallas_call(
        _na_ctx_kernel,
        out_shape=jax.ShapeDtypeStruct((N_CTX, w), F32),
        grid=(BATCH,),
        in_specs=[pl.BlockSpec((SEQ, w), lambda b: (b, 0)),
                  pl.BlockSpec((SEQ, w), lambda b: (b, 1)),
                  pl.BlockSpec((SEQ, w), lambda b: (b, 2))],
        out_specs=pl.BlockSpec((SEQ, w), lambda b: (b, 0)),
        compiler_params=_params("parallel"),
        name="na_ctx",
    )(z_na, z_na, z_na)


def _na_row_lo(qr):
    return min(max(qr - NA_WIN_H // 2, 0), _LAT_ROWS - _NA_KH)


def _na_lat_kernel(rpb_ref, q_ref, k_ref, v_ref, kc_ref, vc_ref, o_ref, bias_scr):
    scale = HEAD_DIM ** -0.5

    @pl.when(pl.program_id(0) == 0)
    def _():
        qc = lax.broadcasted_iota(jnp.int32, (GRID_W, 2 * GRID_W), 0)
        lane = lax.broadcasted_iota(jnp.int32, (GRID_W, 2 * GRID_W), 1)
        kc = lane & (GRID_W - 1)
        col_lo = jnp.clip(qc - NA_WIN_W // 2, 0, GRID_W - NA_WIN_W)
        rel = jnp.where((kc >= col_lo) & (kc < col_lo + NA_WIN_W), kc - qc + NA_WIN_W - 1, -1)
        second = lax.broadcasted_iota(jnp.int32, (1, 2 * GRID_W), 1) >= GRID_W

        def build(idx, carry):
            h = idx // _NA_PAIRS
            row = h * (2 * NA_WIN_H - 1) + (idx - h * _NA_PAIRS)
            tile = jnp.full((GRID_W, 2 * GRID_W), NEG_INF, F32)
            for j in range(2 * NA_WIN_W - 1):
                val = jnp.where(second, rpb_ref[row + 1, j], rpb_ref[row, j])
                tile = jnp.where(rel == j, val, tile)
            bias_scr[idx] = tile
            return carry

        lax.fori_loop(0, NA_HEADS * _NA_PAIRS, build, 0)

    kc = kc_ref[...].astype(BF16)
    vc = vc_ref[...].astype(BF16)
    for qr in range(_LAT_ROWS):
        lo = _na_row_lo(qr)
        dr0 = lo - qr + NA_WIN_H - 1
        q_st = _stack_heads(q_ref[qr * GRID_W:(qr + 1) * GRID_W, :], NA_HEADS, HEAD_DIM).astype(BF16)
        kl = k_ref[lo * GRID_W:lo * GRID_W + _NA_LOCAL, :].astype(BF16)
        vl = v_ref[lo * GRID_W:lo * GRID_W + _NA_LOCAL, :].astype(BF16)
        bias = jnp.concatenate(
            [jnp.concatenate([bias_scr[h * _NA_PAIRS + dr0 + 2 * i] for i in range(_NA_KH // 2)], axis=1)
             for h in range(NA_HEADS)], axis=0)
        s_loc = lax.dot_general(q_st, kl, _NT, preferred_element_type=F32) * scale + bias
        s_ctx = lax.dot_general(q_st, kc, _NT, preferred_element_type=F32) * scale
        p_loc, p_ctx, l = _softmax2(s_loc, s_ctx)
        r = (jnp.dot(p_loc.astype(BF16), vl, preferred_element_type=F32)
             + jnp.dot(p_ctx.astype(BF16), vc, preferred_element_type=F32)) / l
        o_ref[qr * GRID_W:(qr + 1) * GRID_W, :] = _pick_heads(r, NA_HEADS, HEAD_DIM)


def _na_latent(rpb, z_na, kc, vc):
    w = NA_HEADS * HEAD_DIM
    lat0 = N_CTX // DEC_SEQ

    def col(c):
        return pl.BlockSpec((DEC_SEQ, w), lambda b: (lat0 + b, c))

    cache = pl.BlockSpec((None, PAST_LEN, w), lambda b: (b, 0, 0))
    return pl.pallas_call(
        _na_lat_kernel,
        out_shape=jax.ShapeDtypeStruct((N_LAT, w), F32),
        grid=(DEC_BATCH,),
        in_specs=[pl.BlockSpec(memory_space=pltpu.SMEM), col(0), col(1), col(2), cache, cache],
        out_specs=pl.BlockSpec((DEC_SEQ, w), lambda b: (b, 0)),
        scratch_shapes=[pltpu.VMEM((NA_HEADS * _NA_PAIRS, GRID_W, 2 * GRID_W), F32)],
        compiler_params=_params("arbitrary"),
        name="na_latent",
    )(rpb, z_na, z_na, z_na, kc, vc)


_MLA_W = MLA_HEADS * LANES
_ROPE_LO = NOPE_DIM
_ROPE_HALF = ROPE_DIM // 2


def _rope_tables(t_len):
    n_freq = ROPE_DIM // 4
    inv = 1.0 / (ROPE_BASE ** (np.arange(n_freq) / n_freq))
    pos = np.arange(t_len)
    ang = np.concatenate([(pos // GRID_W)[:, None] * inv, (pos % GRID_W)[:, None] * inv], axis=-1)
    cos, sin = np.cos(ang), np.sin(ang)
    c = np.ones((t_len, LANES))
    s1 = np.zeros((t_len, LANES))
    s2 = np.zeros((t_len, LANES))
    a, b, e = _ROPE_LO, _ROPE_LO + _ROPE_HALF, _ROPE_LO + ROPE_DIM
    c[:, a:b] = cos
    c[:, b:e] = cos
    s1[:, a:b] = -sin
    s2[:, b:e] = sin
    return tuple(jnp.asarray(t, F32) for t in (c, s1, s2))


def _apply_rope(x, c, s1, s2):
    up = pltpu.roll(x, LANES - _ROPE_HALF, 1)
    dn = pltpu.roll(x, _ROPE_HALF, 1)
    return x * c + up * s1 + dn * s2


def _mla_ctx_kernel(q_ref, kv_ref, kr_ref, o_ref):
    kr = kr_ref[...]
    v_all = kv_ref[:, _MLA_W:].astype(BF16)
    out = None
    for h in range(MLA_HEADS):
        qh = q_ref[:, h * LANES:(h + 1) * LANES].astype(BF16)
        kh = (kv_ref[:, h * LANES:(h + 1) * LANES] + kr).astype(BF16)
        s = lax.dot_general(qh, kh, _NT, preferred_element_type=F32) * MLA_SCALE
        p = jnp.exp(s - jnp.max(s, axis=-1, keepdims=True))
        l = jnp.sum(p, axis=-1, keepdims=True)
        r = jnp.dot(p.astype(BF16), v_all, preferred_element_type=F32) / l
        out = r if out is None else jnp.where(_head_mask(MLA_HEADS * V_DIM, h, V_DIM), r, out)
    o_ref[...] = out


def _mla_ctx(q_m, kv_m, z_mla):
    wv = MLA_HEADS * V_DIM
    return pl.pallas_call(
        _mla_ctx_kernel,
        out_shape=jax.ShapeDtypeStruct((N_CTX, wv), F32),
        grid=(BATCH,),
        in_specs=[pl.BlockSpec((SEQ, _MLA_W), lambda b: (b, 0)),
                  pl.BlockSpec((SEQ, _MLA_W + wv), lambda b: (b, 0)),
                  pl.BlockSpec((SEQ, LANES), lambda b: (b, 3))],
        out_specs=pl.BlockSpec((SEQ, wv), lambda b: (b, 0)),
        compiler_params=_params("parallel"),
        name="mla_ctx",
    )(q_m, kv_m, z_mla)


_MLA_BQ = 256


def _mla_lat_kernel(q_ref, kv_ref, kr_ref, kvc_ref, krc_ref, cq_ref, s1q_ref, s2q_ref, ck_ref, s1k_ref, s2k_ref,
                    o_ref):
    kr = _apply_rope(kr_ref[...], ck_ref[...], s1k_ref[...], s2k_ref[...])
    krc = krc_ref[...]
    v_lat = kv_ref[:, _MLA_W:].astype(BF16)
    v_ctx = kvc_ref[:, _MLA_W:].astype(BF16)
    out = None
    for h in range(MLA_HEADS):
        blk = slice(h * LANES, (h + 1) * LANES)
        qh = _apply_rope(q_ref[:, blk], cq_ref[...], s1q_ref[...], s2q_ref[...]).astype(BF16)
        kl = (kv_ref[:, blk] + kr).astype(BF16)
        kc = (kvc_ref[:, blk] + krc).astype(BF16)
        s_l = lax.dot_general(qh, kl, _NT, preferred_element_type=F32) * MLA_SCALE
        s_c = lax.dot_general(qh, kc, _NT, preferred_element_type=F32) * MLA_SCALE
        p_l, p_c, l = _softmax2(s_l, s_c)
        r = (jnp.dot(p_l.astype(BF16), v_lat, preferred_element_type=F32)
             + jnp.dot(p_c.astype(BF16), v_ctx, preferred_element_type=F32)) / l
        out = r if out is None else jnp.where(_head_mask(MLA_HEADS * V_DIM, h, V_DIM), r, out)
    o_ref[...] = out


def _mla_latent(q_m, kv_m, z_mla, kv_cache, kr_cache, rope):
    wv = MLA_HEADS * V_DIM
    nq = DEC_SEQ // _MLA_BQ
    q0 = N_CTX // _MLA_BQ
    lat0 = N_CTX // DEC_SEQ
    c, s1, s2 = rope
    tq = pl.BlockSpec((_MLA_BQ, LANES), lambda b, j: (j, 0))
    tk = pl.BlockSpec((DEC_SEQ, LANES), lambda b, j: (0, 0))
    return pl.pallas_call(
        _mla_lat_kernel,
        out_shape=jax.ShapeDtypeStruct((N_LAT, wv), F32),
        grid=(DEC_BATCH, nq),
        in_specs=[pl.BlockSpec((_MLA_BQ, _MLA_W), lambda b, j: (q0 + b * nq + j, 0)),
                  pl.BlockSpec((DEC_SEQ, _MLA_W + wv), lambda b, j: (lat0 + b, 0)),
                  pl.BlockSpec((DEC_SEQ, LANES), lambda b, j: (lat0 + b, 3)),
                  pl.BlockSpec((PAST_LEN, _MLA_W + wv), lambda b, j: (b, 0)),
                  pl.BlockSpec((PAST_LEN, LANES), lambda b, j: (b, 0)),
                  tq, tq, tq, tk, tk, tk],
        out_specs=pl.BlockSpec((_MLA_BQ, wv), lambda b, j: (b * nq + j, 0)),
        compiler_params=_params("parallel", "arbitrary"),
        name="mla_lat",
    )(q_m, kv_m, z_mla, kv_cache, kr_cache, c, s1, s2, c, s1, s2)


def _head_sum(x, ones_ref):
    return jnp.dot(x, ones_ref[...], preferred_element_type=F32, precision=lax.Precision.HIGHEST)


def _rw_pre_kernel(r_ref, k_ref, v_ref, wf_ref, wb_ref, af_ref, ab_ref, kkw_ref, ka_ref, rk_ref, ones_ref,
                   dwf_ref, dwb_ref, kdf_ref, kdb_ref, bf_ref, bb_ref, kk_ref, bonus_ref):
    k = k_ref[...]
    kk = k * kkw_ref[...]
    nrm = jnp.sqrt(_head_sum(kk * kk, ones_ref))
    kk = kk / jnp.maximum(nrm, 1e-12)
    kk_ref[...] = kk
    kd_sum = None
    for wl_ref, al_ref, dw_ref, kd_ref, b_ref in ((wf_ref, af_ref, dwf_ref, kdf_ref, bf_ref),
                                                  (wb_ref, ab_ref, dwb_ref, kdb_ref, bb_ref)):
        dw_ref[...] = jnp.exp(-float(np.exp(-0.5)) * _sigmoid(wl_ref[...]))
        a = _sigmoid(al_ref[...])
        kd = k * (1.0 + (a - 1.0) * ka_ref[...])
        kd_ref[...] = kd
        b_ref[...] = kk * a
        kd_sum = kd if kd_sum is None else kd_sum + kd
    bonus_ref[...] = _head_sum(r_ref[...] * rk_ref[...] * kd_sum, ones_ref) * v_ref[...]


def _rw_pre(z_rw, wa, kk_w, ka, rk, ones):
    tm = 512

    def col(c):
        return pl.BlockSpec((tm, RW_DIM), lambda i: (i, c))

    vec = pl.BlockSpec((1, RW_DIM), lambda i: (0, 0))
    out = jax.ShapeDtypeStruct((N_TOK, RW_DIM), F32)
    return pl.pallas_call(
        _rw_pre_kernel,
        out_shape=(out,) * 8,
        grid=(N_TOK // tm,),
        in_specs=[col(0), col(1), col(2), col(0), col(1), col(2), col(3), vec, vec, vec,
                  pl.BlockSpec((RW_DIM, RW_DIM), lambda i: (0, 0))],
        out_specs=(pl.BlockSpec((tm, RW_DIM), lambda i: (i, 0)),) * 8,
        compiler_params=_params("parallel"),
        name="rwkv_pre",
    )(z_rw, z_rw, z_rw, wa, wa, wa, wa, kk_w.reshape(1, RW_DIM), ka.reshape(1, RW_DIM),
      rk.reshape(1, RW_DIM), ones)


def _rw_scan_kernel(w_ref, kk_ref, b_ref, kd_ref, r_ref, v_ref, s0_ref, y_ref, sf_ref, s_scr):
    c = pl.program_id(1)

    @pl.when(c == 0)
    def _():
        s_scr[...] = s0_ref[...]

    def step(t, carry):
        w = w_ref[t]
        kk = kk_ref[t]
        b = b_ref[t]
        kd = kd_ref[t]
        r = r_ref[t]
        for vi in range(HEAD_DIM):
            s = s_scr[vi]
            sa = jnp.sum(s * kk, axis=0, keepdims=True)
            s = s * w - sa * b + v_ref[t, pl.ds(vi, 1), :] * kd
            s_scr[vi] = s
            y_ref[t, pl.ds(vi, 1), :] = jnp.sum(s * r, axis=0, keepdims=True)
        return carry

    lax.fori_loop(0, SCAN_TC, step, 0)

    @pl.when(c == pl.num_programs(1) - 1)
    def _():
        sf_ref[...] = s_scr[...]


def _rw_scan(w, kk, b, kd, r, v, s0):
    t, _, lanes = w.shape
    seq = pl.BlockSpec((SCAN_TC, HEAD_DIM, LANES), lambda g, c: (c, 0, g))
    st = pl.BlockSpec((HEAD_DIM, HEAD_DIM, LANES), lambda g, c: (0, 0, g))
    return pl.pallas_call(
        _rw_scan_kernel,
        out_shape=(jax.ShapeDtypeStruct((t, HEAD_DIM, lanes), F32),
                   jax.ShapeDtypeStruct((HEAD_DIM, HEAD_DIM, lanes), F32)),
        grid=(lanes // LANES, t // SCAN_TC),
        in_specs=[seq] * 6 + [st],
        out_specs=(seq, st),
        scratch_shapes=[pltpu.VMEM((HEAD_DIM, HEAD_DIM, LANES), F32)],
        compiler_params=_params("parallel", "arbitrary"),
        name="rwkv_scan",
    )(w, kk, b, kd, r, v, s0)


def _rw_post_kernel(yf_ref, yb_ref, bonus_ref, g_ref, gw_ref, gb_ref, ones_ref, o_ref):
    y = yf_ref[...] + yb_ref[...]
    inv = 1.0 / HEAD_DIM
    mu = _head_sum(y, ones_ref) * inv
    d = y - mu
    var = _head_sum(d * d, ones_ref) * inv
    yn = d * lax.rsqrt(var + GN_EPS) * gw_ref[...] + gb_ref[...]
    o_ref[...] = (yn + bonus_ref[...]) * g_ref[...]


def _rw_post(y_f, y_b, bonus, g, gn_w, gn_b, ones):
    tm = 1024
    row = pl.BlockSpec((tm, RW_DIM), lambda i: (i, 0))
    vec = pl.BlockSpec((1, RW_DIM), lambda i: (0, 0))
    return pl.pallas_call(
        _rw_post_kernel,
        out_shape=jax.ShapeDtypeStruct((N_TOK, RW_DIM), F32),
        grid=(N_TOK // tm,),
        in_specs=[row, row, row, row, vec, vec, pl.BlockSpec((RW_DIM, RW_DIM), lambda i: (0, 0))],
        out_specs=row,
        compiler_params=_params("parallel"),
        name="rwkv_post",
    )(y_f, y_b, bonus, g, gn_w.reshape(1, RW_DIM), gn_b.reshape(1, RW_DIM), ones)


def _to_scan(x, nb, t, reverse):
    x = x.reshape(nb, t, RW_HEADS, HEAD_DIM)
    if reverse:
        x = x[:, ::-1]
    return x.transpose(1, 3, 0, 2).reshape(t, HEAD_DIM, nb * RW_HEADS)


def _from_scan(y, nb, t, reverse):
    y = y.reshape(t, HEAD_DIM, nb, RW_HEADS).transpose(2, 0, 3, 1)
    if reverse:
        y = y[:, ::-1]
    return y.reshape(nb * t, RW_DIM)


def _rw_scan_part(ops, nb, t, s0_f, s0_b):
    w_f, w_b, kk, b_f, b_b, kd_f, kd_b, r, v = ops
    n = nb * RW_HEADS
    pad = (-2 * n) % LANES

    def both(xf, xb):
        z = jnp.concatenate([_to_scan(xf, nb, t, False), _to_scan(xb, nb, t, True)], axis=-1)
        return jnp.pad(z, ((0, 0), (0, 0), (0, pad))) if pad else z

    def state(s):
        return s.reshape(n, HEAD_DIM, HEAD_DIM).transpose(1, 2, 0)

    s0 = jnp.concatenate([state(s0_f), state(s0_b)], axis=-1)
    if pad:
        s0 = jnp.pad(s0, ((0, 0), (0, 0), (0, pad)))
    y, sf = _rw_scan(both(w_f, w_b), both(kk, kk), both(b_f, b_b), both(kd_f, kd_b), both(r, r), both(v, v), s0)
    y_f = _from_scan(y[:, :, :n], nb, t, False)
    y_b = _from_scan(y[:, :, n:2 * n], nb, t, True)

    def unstate(s):
        return s.transpose(2, 0, 1).reshape(nb, RW_HEADS, HEAD_DIM, HEAD_DIM)

    return y_f, y_b, unstate(sf[:, :, :n]), unstate(sf[:, :, n:2 * n])


_TM_MERGE = 512


def _merge_kernel(x_ref, gm_ref, h_ref, nac_ref, nal_ref, rw_ref, mlc_ref, mll_ref, wg_ref, wbr_ref, wo_ref, o_ref):
    is_ctx = pl.program_id(0) < N_CTX // _TM_MERGE
    h = h_ref[...]
    branches = (jnp.where(is_ctx, nac_ref[...], nal_ref[...]), rw_ref[...],
                jnp.where(is_ctx, mlc_ref[...], mll_ref[...]))
    m = None
    for i, o_b in enumerate(branches):
        gate = jnp.dot(h, wg_ref[:, i * D_MODEL:(i + 1) * D_MODEL], preferred_element_type=F32)
        br = jnp.dot(o_b.astype(BF16), wbr_ref[i], preferred_element_type=F32)
        t = _sigmoid(gate) * br
        m = t if m is None else m + t
    mix = jnp.dot(m.astype(BF16), wo_ref[...], preferred_element_type=F32)
    o_ref[...] = x_ref[...] + gm_ref[...] * mix


def _merge(x, mod_l, h, na_ctx, na_lat, o_rw, mla_ctx, mla_lat, w_gate, w_br, w_o):
    tm = _TM_MERGE
    n_ctx = N_CTX // tm
    row = pl.BlockSpec((tm, D_MODEL), lambda i: (i, 0))
    br = pl.BlockSpec((tm, BRANCH_DIM), lambda i: (i, 0))
    br_ctx = pl.BlockSpec((tm, BRANCH_DIM), lambda i: (jnp.minimum(i, n_ctx - 1), 0))
    br_lat = pl.BlockSpec((tm, BRANCH_DIM), lambda i: (jnp.maximum(i - n_ctx, 0), 0))
    return pl.pallas_call(
        _merge_kernel,
        out_shape=jax.ShapeDtypeStruct((N_TOK, D_MODEL), F32),
        grid=(N_TOK // tm,),
        in_specs=[row, pl.BlockSpec((None, 1, D_MODEL), lambda i: (_cond_row(i, tm), 0, 5)),
                  row, br_ctx, br_lat, br, br_ctx, br_lat,
                  pl.BlockSpec((D_MODEL, N_BRANCH * D_MODEL), lambda i: (0, 0)),
                  pl.BlockSpec((N_BRANCH, BRANCH_DIM, D_MODEL), lambda i: (0, 0, 0)),
                  pl.BlockSpec((D_MODEL, D_MODEL), lambda i: (0, 0))],
        out_specs=row,
        compiler_params=_params("arbitrary"),
        name="merge",
    )(x, mod_l, h, na_ctx, na_lat, o_rw, mla_ctx, mla_lat, w_gate, w_br, w_o)


def _lora_act(a):
    col = lax.broadcasted_iota(jnp.int32, a.shape, 1)
    return jnp.where(col < 2 * DECAY_LORA, jnp.tanh(a), a)


def _block_diag(blocks):
    rows = sum(b.shape[0] for b in blocks)
    cols = sum(b.shape[1] for b in blocks)
    out = jnp.zeros((rows, cols), blocks[0].dtype)
    r = c = 0
    for b in blocks:
        out = lax.dynamic_update_slice(out, b, (r, c))
        r += b.shape[0]
        c += b.shape[1]
    return out


def _pad_cols(w, left, total):
    return jnp.pad(w, ((0, 0), (left, total - left - w.shape[1])))


def _mla_uq_layout(w_uq):
    per = NOPE_DIM + ROPE_DIM
    return jnp.concatenate([_pad_cols(w_uq[:, h * per:(h + 1) * per], 0, LANES) for h in range(MLA_HEADS)], axis=1)


def _mla_ukv_layout(w_ukv):
    per = NOPE_DIM + V_DIM
    k = [_pad_cols(w_ukv[:, h * per:h * per + NOPE_DIM], 0, LANES) for h in range(MLA_HEADS)]
    v = [w_ukv[:, h * per + NOPE_DIM:(h + 1) * per] for h in range(MLA_HEADS)]
    return jnp.concatenate(k + v, axis=1)


def _heads_on_lanes(cache):
    b, h, t, d = cache.shape
    return cache.transpose(0, 2, 1, 3).reshape(b, t, h * d)


def _heads_major(z, nb, t, nh):
    return z.reshape(nb, t, nh, -1).transpose(0, 2, 1, 3)


def kernel(x_prompt, x_sample, c, cache_na_k, cache_na_v, cache_mla_ckv, cache_mla_krope, state_rwkv_fwd, state_rwkv_bwd, c_ctx, ada_w, ada_b, norm_g, ffn_wg, ffn_wu, ffn_wd, w_in, na_rpb, rw_w0, rw_w2, rw_a0, rw_a2, rw_g2, rw_kk, rw_ka, rw_rk, rw_gn_w, rw_gn_b, mla_qn_g, mla_kvn_g, mla_w_uq, mla_w_ukv, w_br, w_o, final_g):
    x = jnp.concatenate([x_prompt.reshape(N_CTX, D_MODEL), x_sample.reshape(N_LAT, D_MODEL)], axis=0)
    cond = jnp.concatenate([c_ctx[None, :], c, jnp.zeros((N_COND - 1 - DEC_BATCH, D_MODEL), F32)], axis=0)
    mod = _modulation(cond, ada_w, ada_b)

    ones = jnp.asarray(np.kron(np.eye(RW_HEADS), np.ones((HEAD_DIM, HEAD_DIM))), F32)
    rope = _rope_tables(DEC_SEQ)
    zero_state = jnp.zeros((BATCH, RW_HEADS, HEAD_DIM, HEAD_DIM), F32)

    col_rw = 3 * NA_HEADS * HEAD_DIM
    col_mla = col_rw + 3 * RW_DIM + 2 * DECAY_LORA + 2 * ICLR_LORA + GATE_LORA
    col_kr = col_mla + Q_LORA + KV_LORA
    col_gate = col_kr + ROPE_DIM
    kr_lane0 = Q_LORA + KV_LORA + _ROPE_LO

    caches = []
    y = None
    for l in range(DEPTH):
        mod_l = mod[l]
        wi = w_in[l].astype(BF16)
        w_na = wi[:, :col_rw]
        w_rw = wi[:, col_rw:col_mla]
        w_mla = jnp.concatenate([wi[:, col_mla:col_kr], _pad_cols(wi[:, col_kr:col_gate], _ROPE_LO, LANES)], axis=1)
        w_gate = wi[:, col_gate:]

        x, h = _ffn(x, norm_g[l, 0], mod_l, 0, ffn_wg[l, 0].astype(BF16), ffn_wu[l, 0].astype(BF16),
                    ffn_wd[l, 0].astype(BF16), post="mod", post_g=norm_g[l, 1], post_chunk=3)

        z_na = _mm(h, w_na, name="proj_na")
        na_ctx = _na_ctx(z_na)
        na_lat = _na_latent(na_rpb[l].reshape(NA_HEADS * (2 * NA_WIN_H - 1), 2 * NA_WIN_W - 1), z_na,
                            _heads_on_lanes(cache_na_k[:, l]), _heads_on_lanes(cache_na_v[:, l]))

        z_rw = _mm(h, w_rw, name="proj_rw")
        lora_w = _block_diag([rw_w2[l, 0], rw_w2[l, 1], rw_a2[l, 0], rw_a2[l, 1]]).astype(BF16)
        lora_b = jnp.concatenate([rw_w0[l, 0], rw_w0[l, 1], rw_a0[l, 0], rw_a0[l, 1]])
        wa = _mm(z_rw, lora_w, bias=lora_b, act=_lora_act, a_cols=(256, 3), name="rw_lora")
        g_rw = _mm(z_rw, rw_g2[l].astype(BF16), act=_sigmoid, a_cols=(GATE_LORA, 8), name="rw_gate")
        dw_f, dw_b, kd_f, kd_b, b_f, b_b, kk, bonus = _rw_pre(
            z_rw, wa, rw_kk[l], rw_ka[l], rw_rk[l].reshape(RW_DIM), ones)
        r_rw = z_rw[:, :RW_DIM]
        v_rw = z_rw[:, 2 * RW_DIM:3 * RW_DIM]
        ops = (dw_f, dw_b, kk, b_f, b_b, kd_f, kd_b, r_rw, v_rw)
        yc_f, yc_b, sc_f, sc_b = _rw_scan_part([t[:N_CTX] for t in ops], BATCH, SEQ, zero_state, zero_state)
        yl_f, yl_b, _, _ = _rw_scan_part([t[N_CTX:] for t in ops], DEC_BATCH, DEC_SEQ,
                                         state_rwkv_fwd[:, l], state_rwkv_bwd[:, l])
        o_rwkv = _rw_post(jnp.concatenate([yc_f, yl_f], axis=0), jnp.concatenate([yc_b, yl_b], axis=0),
                          bonus, g_rw, rw_gn_w[l], rw_gn_b[l], ones)

        z_mla = _mm(h, w_mla, name="proj_mla")
        cq = _norm(z_mla, mla_qn_g[l], BF16, (Q_LORA, 0), "mla_qnorm")
        ckv = _norm(z_mla, mla_kvn_g[l], F32, (KV_LORA, Q_LORA // KV_LORA), "mla_kvnorm")
        q_m = _mm(cq, _mla_uq_layout(mla_w_uq[l]).astype(BF16), name="mla_uq")
        w_ukv = _mla_ukv_layout(mla_w_ukv[l]).astype(BF16)
        kv_m = _mm(ckv, w_ukv, name="mla_ukv")
        kv_cache = _mm(cache_mla_ckv[:, l].reshape(DEC_BATCH * PAST_LEN, KV_LORA), w_ukv, name="mla_ukv_cache")
        kr_cache = _pad_cols(cache_mla_krope[:, l].reshape(DEC_BATCH * PAST_LEN, ROPE_DIM), _ROPE_LO, LANES)
        mla_ctx = _mla_ctx(q_m, kv_m, z_mla)
        mla_lat = _mla_latent(q_m, kv_m, z_mla, kv_cache, kr_cache, rope)

        x = _merge(x, mod_l, h, na_ctx, na_lat, o_rwkv, mla_ctx, mla_lat, w_gate,
                   w_br[l].astype(BF16), w_o[l].astype(BF16))

        ffn2 = (x, norm_g[l, 2], mod_l, 6, ffn_wg[l, 1].astype(BF16), ffn_wu[l, 1].astype(BF16),
                ffn_wd[l, 1].astype(BF16))
        if l == DEPTH - 1:
            y = _ffn(*ffn2, post="plain", post_g=final_g)
        else:
            x = _ffn(*ffn2)

        w = NA_HEADS * HEAD_DIM
        caches.append((_heads_major(z_na[:N_CTX, w:2 * w], BATCH, SEQ, NA_HEADS),
                       _heads_major(z_na[:N_CTX, 2 * w:3 * w], BATCH, SEQ, NA_HEADS),
                       ckv[:N_CTX].reshape(BATCH, SEQ, KV_LORA),
                       z_mla[:N_CTX, kr_lane0:kr_lane0 + ROPE_DIM].reshape(BATCH, SEQ, ROPE_DIM),
                       sc_f, sc_b))

    y_prompt = y[:N_CTX].reshape(BATCH, SEQ, D_MODEL)
    y_sample = y[N_CTX:].reshape(DEC_BATCH, DEC_SEQ, D_MODEL)
    outs = [jnp.stack([cl[i] for cl in caches], axis=1) for i in range(6)]
    return (y_prompt, y_sample, *outs)
```

```python
import functools

import numpy as np
import jax
import jax.numpy as jnp
from jax import lax
from jax.experimental import pallas as pl
from jax.experimental.pallas import tpu as pltpu

F32 = jnp.float32
BF16 = jnp.bfloat16

D_MODEL = 1024
BATCH = 32
SEQ = 256
DEPTH = 2
DEC_BATCH = 8
DEC_SEQ = 1024
PAST_LEN = 256
GRID_W = 64
HEAD_DIM = 64
NA_HEADS = 4
NA_WIN_H = 8
NA_WIN_W = 16
RW_HEADS = 4
RW_DIM = RW_HEADS * HEAD_DIM
DECAY_LORA = 64
ICLR_LORA = 64
GATE_LORA = 128
GN_EPS = 64e-5
MLA_HEADS = 4
Q_LORA = 256
KV_LORA = 128
NOPE_DIM = 64
ROPE_DIM = 32
V_DIM = 64
MLA_SCALE = (NOPE_DIM + ROPE_DIM) ** -0.5
ROPE_BASE = 10000.0
N_BRANCH = 3
BRANCH_DIM = 256
D_FF = 2816
N_MOD = 9
NORM_EPS = 1e-6
NEG_INF = -1e30

N_CTX = BATCH * SEQ
N_LAT = DEC_BATCH * DEC_SEQ
N_TOK = N_CTX + N_LAT
N_COND = 16

LANES = 128
VMEM_LIMIT = 48 * 1024 * 1024

TM = 1024
TF = 256
SCAN_TC = 32


def _cond_row(i, tm):
    return jnp.where(i * tm < N_CTX, 0, 1 + (i * tm - N_CTX) // DEC_SEQ)


def _sigmoid(x):
    return 1.0 / (1.0 + jnp.exp(-x))


def _params(*sem):
    return pltpu.CompilerParams(dimension_semantics=sem, vmem_limit_bytes=VMEM_LIMIT)


def _mm_kernel(a_ref, w_ref, *rest, act, has_bias):
    if has_bias:
        b_ref, o_ref = rest
    else:
        (o_ref,) = rest
    a = a_ref[...]
    if act is not None:
        a = act(a.astype(F32))
    acc = jnp.dot(a.astype(BF16), w_ref[...].astype(BF16), preferred_element_type=F32)
    if has_bias:
        acc = acc + b_ref[...]
    o_ref[...] = acc.astype(o_ref.dtype)


def _mm(a, w, bias=None, act=None, out_dtype=F32, a_cols=None, a_rows=None, tn=None, name="mm"):
    row0, m = (0, a.shape[0]) if a_rows is None else a_rows
    k, n = w.shape
    kb = 0
    if a_cols is not None:
        assert a_cols[0] == k
        kb = a_cols[1]
    else:
        assert a.shape[1] == k
    tm = min(m, TM)
    tn = n if tn is None else tn
    assert m % tm == 0 and n % tn == 0 and row0 % tm == 0
    i0 = row0 // tm
    in_specs = [pl.BlockSpec((tm, k), lambda i, j: (i0 + i, kb)),
                pl.BlockSpec((k, tn), lambda i, j: (0, j))]
    args = [a, w]
    if bias is not None:
        in_specs.append(pl.BlockSpec((1, tn), lambda i, j: (0, j)))
        args.append(bias.reshape(1, n).astype(F32))
    return pl.pallas_call(
        functools.partial(_mm_kernel, act=act, has_bias=bias is not None),
        out_shape=jax.ShapeDtypeStruct((m, n), out_dtype),
        grid=(m // tm, n // tn),
        in_specs=in_specs,
        out_specs=pl.BlockSpec((tm, tn), lambda i, j: (i, j)),
        compiler_params=_params("parallel", "arbitrary"),
        name=name,
    )(*args)


def _mod_kernel(c_ref, w_ref, b_ref, o_ref):
    c = c_ref[...]
    a = (c * _sigmoid(c)).astype(BF16)
    o_ref[...] = jnp.dot(a, w_ref[...].astype(BF16), preferred_element_type=F32) + b_ref[...]


def _modulation(cond, ada_w, ada_b):
    n = N_MOD * D_MODEL
    tn = D_MODEL
    out = pl.pallas_call(
        _mod_kernel,
        out_shape=jax.ShapeDtypeStruct((DEPTH, N_COND, n), F32),
        grid=(DEPTH, n // tn),
        in_specs=[pl.BlockSpec((N_COND, D_MODEL), lambda l, j: (0, 0)),
                  pl.BlockSpec((None, D_MODEL, tn), lambda l, j: (l, 0, j)),
                  pl.BlockSpec((None, 1, tn), lambda l, j: (l, 0, j))],
        out_specs=pl.BlockSpec((None, N_COND, tn), lambda l, j: (l, 0, j)),
        compiler_params=_params("parallel", "arbitrary"),
        name="modulation",
    )(cond, ada_w, ada_b.reshape(DEPTH, 1, n))
    return out.reshape(DEPTH, N_COND, 1, n)


def _rms(x, g):
    return x * lax.rsqrt(jnp.mean(x * x, axis=-1, keepdims=True) + NORM_EPS) * g


def _norm_kernel(x_ref, g_ref, o_ref):
    o_ref[...] = _rms(x_ref[...].astype(F32), g_ref[...]).astype(o_ref.dtype)


def _norm(x, g, out_dtype, x_cols, name):
    m = x.shape[0]
    width, cb = x_cols
    tm = min(m, TM)
    return pl.pallas_call(
        _norm_kernel,
        out_shape=jax.ShapeDtypeStruct((m, width), out_dtype),
        grid=(m // tm,),
        in_specs=[pl.BlockSpec((tm, width), lambda i: (i, cb)),
                  pl.BlockSpec((1, width), lambda i: (0, 0))],
        out_specs=pl.BlockSpec((tm, width), lambda i: (i, 0)),
        compiler_params=_params("parallel"),
        name=name,
    )(x, g.reshape(1, width).astype(F32))


def _ffn_kernel(x_ref, g_ref, sh_ref, sc_ref, gt_ref, wg_ref, wu_ref, wd_ref, *rest, post):
    if post == "mod":
        g2_ref, sh2_ref, sc2_ref, o_ref, h2_ref, h_scr, acc_scr = rest
    elif post == "plain":
        g2_ref, y_ref, h_scr, acc_scr = rest
    else:
        o_ref, h_scr, acc_scr = rest
    f = pl.program_id(1)

    @pl.when(f == 0)
    def _():
        y = _rms(x_ref[...], g_ref[...])
        h_scr[...] = (y * (1.0 + sc_ref[...]) + sh_ref[...]).astype(BF16)
        acc_scr[...] = jnp.zeros_like(acc_scr)

    h = h_scr[...]
    gg = jnp.dot(h, wg_ref[...], preferred_element_type=F32)
    uu = jnp.dot(h, wu_ref[...], preferred_element_type=F32)
    a = (gg * _sigmoid(gg)) * uu
    acc_scr[...] += jnp.dot(a.astype(BF16), wd_ref[...], preferred_element_type=F32)

    @pl.when(f == pl.num_programs(1) - 1)
    def _():
        xn = x_ref[...] + 0.5 * gt_ref[...] * acc_scr[...]
        if post == "plain":
            y_ref[...] = _rms(xn, g2_ref[...])
        else:
            o_ref[...] = xn
        if post == "mod":
            h2_ref[...] = (_rms(xn, g2_ref[...]) * (1.0 + sc2_ref[...]) + sh2_ref[...]).astype(BF16)


def _ffn(x, norm_g, mod_l, chunk0, wg, wu, wd, post=None, post_g=None, post_chunk=None):
    def mod_spec(c):
        return pl.BlockSpec((None, 1, D_MODEL), lambda i, f: (_cond_row(i, TM), 0, c))

    row = pl.BlockSpec((TM, D_MODEL), lambda i, f: (i, 0))
    vec = pl.BlockSpec((1, D_MODEL), lambda i, f: (0, 0))
    in_specs = [row, vec, mod_spec(chunk0), mod_spec(chunk0 + 1), mod_spec(chunk0 + 2),
                pl.BlockSpec((D_MODEL, TF), lambda i, f: (0, f)),
                pl.BlockSpec((D_MODEL, TF), lambda i, f: (0, f)),
                pl.BlockSpec((TF, D_MODEL), lambda i, f: (f, 0))]
    args = [x, norm_g.reshape(1, D_MODEL), mod_l, mod_l, mod_l, wg, wu, wd]
    xs = jax.ShapeDtypeStruct((N_TOK, D_MODEL), F32)
    if post == "mod":
        in_specs += [vec, mod_spec(post_chunk), mod_spec(post_chunk + 1)]
        args += [post_g.reshape(1, D_MODEL), mod_l, mod_l]
        out_shape, out_specs = (xs, jax.ShapeDtypeStruct((N_TOK, D_MODEL), BF16)), (row, row)
    elif post == "plain":
        in_specs += [vec]
        args += [post_g.reshape(1, D_MODEL)]
        out_shape, out_specs = xs, row
    else:
        out_shape, out_specs = xs, row
    return pl.pallas_call(
        functools.partial(_ffn_kernel, post=post),
        out_shape=out_shape,
        grid=(N_TOK // TM, D_FF // TF),
        in_specs=in_specs,
        out_specs=out_specs,
        scratch_shapes=[pltpu.VMEM((TM, D_MODEL), BF16), pltpu.VMEM((TM, D_MODEL), F32)],
        compiler_params=_params("parallel", "arbitrary"),
        name="ffn",
    )(*args)


_NT = (((1,), (1,)), ((), ()))
_LAT_ROWS = DEC_SEQ // GRID_W
_NA_KH = min(NA_WIN_H, _LAT_ROWS)
_NA_LOCAL = _NA_KH * GRID_W
_NA_PAIRS = 2 * NA_WIN_H - 2


def _head_mask(width, h, group):
    lane = lax.broadcasted_iota(jnp.int32, (1, width), 1)
    return (lane >= h * group) & (lane < (h + 1) * group)


def _stack_heads(q, nh, group):
    return jnp.concatenate([jnp.where(_head_mask(q.shape[1], h, group), q, 0.0) for h in range(nh)], axis=0)


def _pick_heads(r, nh, group):
    t = r.shape[0] // nh
    out = r[0:t]
    for h in range(1, nh):
        out = jnp.where(_head_mask(r.shape[1], h, group), r[h * t:(h + 1) * t], out)
    return out


def _softmax2(s_a, s_b):
    m = jnp.maximum(jnp.max(s_a, axis=-1, keepdims=True), jnp.max(s_b, axis=-1, keepdims=True))
    p_a = jnp.exp(s_a - m)
    p_b = jnp.exp(s_b - m)
    return p_a, p_b, jnp.sum(p_a, axis=-1, keepdims=True) + jnp.sum(p_b, axis=-1, keepdims=True)


def _na_ctx_kernel(q_ref, k_ref, v_ref, o_ref):
    q_st = _stack_heads(q_ref[...], NA_HEADS, HEAD_DIM).astype(BF16)
    s = lax.dot_general(q_st, k_ref[...].astype(BF16), _NT, preferred_element_type=F32) * HEAD_DIM ** -0.5
    p = jnp.exp(s - jnp.max(s, axis=-1, keepdims=True))
    l = jnp.sum(p, axis=-1, keepdims=True)
    r = jnp.dot(p.astype(BF16), v_ref[...].astype(BF16), preferred_element_type=F32) / l
    o_ref[...] = _pick_heads(r, NA_HEADS, HEAD_DIM)


def _na_ctx(z_na):
    w = NA_HEADS * HEAD_DIM
    return pl.pallas_call(
        _na_ctx_kernel,
        out_shape=jax.ShapeDtypeStruct((N_CTX, w), F32),
        grid=(BATCH,),
        in_specs=[pl.BlockSpec((SEQ, w), lambda b: (b, 0)),
                  pl.BlockSpec((SEQ, w), lambda b: (b, 1)),
                  pl.BlockSpec((SEQ, w), lambda b: (b, 2))],
        out_specs=pl.BlockSpec((SEQ, w), lambda b: (b, 0)),
        compiler_params=_params("parallel"),
        name="na_ctx",
    )(z_na, z_na, z_na)


def _na_row_lo(qr):
    return min(max(qr - NA_WIN_H // 2, 0), _LAT_ROWS - _NA_KH)


def _na_lat_kernel(rpb_ref, q_ref, k_ref, v_ref, kc_ref, vc_ref, o_ref, bias_scr):
    scale = HEAD_DIM ** -0.5

    @pl.when(pl.program_id(0) == 0)
    def _():
        qc = lax.broadcasted_iota(jnp.int32, (GRID_W, 2 * GRID_W), 0)
        lane = lax.broadcasted_iota(jnp.int32, (GRID_W, 2 * GRID_W), 1)
        kc = lane & (GRID_W - 1)
        col_lo = jnp.clip(qc - NA_WIN_W // 2, 0, GRID_W - NA_WIN_W)
        rel = jnp.where((kc >= col_lo) & (kc < col_lo + NA_WIN_W), kc - qc + NA_WIN_W - 1, -1)
        second = lax.broadcasted_iota(jnp.int32, (1, 2 * GRID_W), 1) >= GRID_W

        def build(idx, carry):
            h = idx // _NA_PAIRS
            row = h * (2 * NA_WIN_H - 1) + (idx - h * _NA_PAIRS)
            tile = jnp.full((GRID_W, 2 * GRID_W), NEG_INF, F32)
            for j in range(2 * NA_WIN_W - 1):
                val = jnp.where(second, rpb_ref[row + 1, j], rpb_ref[row, j])
                tile = jnp.where(rel == j, val, tile)
            bias_scr[idx] = tile
            return carry

        lax.fori_loop(0, NA_HEADS * _NA_PAIRS, build, 0)

    kc = kc_ref[...].astype(BF16)
    vc = vc_ref[...].astype(BF16)
    for qr in range(_LAT_ROWS):
        lo = _na_row_lo(qr)
        dr0 = lo - qr + NA_WIN_H - 1
        q_st = _stack_heads(q_ref[qr * GRID_W:(qr + 1) * GRID_W, :], NA_HEADS, HEAD_DIM).astype(BF16)
        kl = k_ref[lo * GRID_W:lo * GRID_W + _NA_LOCAL, :].astype(BF16)
        vl = v_ref[lo * GRID_W:lo * GRID_W + _NA_LOCAL, :].astype(BF16)
        bias = jnp.concatenate(
            [jnp.concatenate([bias_scr[h * _NA_PAIRS + dr0 + 2 * i] for i in range(_NA_KH // 2)], axis=1)
             for h in range(NA_HEADS)], axis=0)
        s_loc = lax.dot_general(q_st, kl, _NT, preferred_element_type=F32) * scale + bias
        s_ctx = lax.dot_general(q_st, kc, _NT, preferred_element_type=F32) * scale
        p_loc, p_ctx, l = _softmax2(s_loc, s_ctx)
        r = (jnp.dot(p_loc.astype(BF16), vl, preferred_element_type=F32)
             + jnp.dot(p_ctx.astype(BF16), vc, preferred_element_type=F32)) / l
        o_ref[qr * GRID_W:(qr + 1) * GRID_W, :] = _pick_heads(r, NA_HEADS, HEAD_DIM)


def _na_latent(rpb, z_na, kc, vc):
    w = NA_HEADS * HEAD_DIM
    lat0 = N_CTX // DEC_SEQ

    def col(c):
        return pl.BlockSpec((DEC_SEQ, w), lambda b: (lat0 + b, c))

    cache = pl.BlockSpec((None, PAST_LEN, w), lambda b: (b, 0, 0))
    return pl.pallas_call(
        _na_lat_kernel,
        out_shape=jax.ShapeDtypeStruct((N_LAT, w), F32),
        grid=(DEC_BATCH,),
        in_specs=[pl.BlockSpec(memory_space=pltpu.SMEM), col(0), col(1), col(2), cache, cache],
        out_specs=pl.BlockSpec((DEC_SEQ, w), lambda b: (b, 0)),
        scratch_shapes=[pltpu.VMEM((NA_HEADS * _NA_PAIRS, GRID_W, 2 * GRID_W), F32)],
        compiler_params=_params("arbitrary"),
        name="na_latent",
    )(rpb, z_na, z_na, z_na, kc, vc)


_MLA_W = MLA_HEADS * LANES
_ROPE_LO = NOPE_DIM
_ROPE_HALF = ROPE_DIM // 2


def _rope_tables(t_len):
    n_freq = ROPE_DIM // 4
    inv = 1.0 / (ROPE_BASE ** (np.arange(n_freq) / n_freq))
    pos = np.arange(t_len)
    ang = np.concatenate([(pos // GRID_W)[:, None] * inv, (pos % GRID_W)[:, None] * inv], axis=-1)
    cos, sin = np.cos(ang), np.sin(ang)
    c = np.ones((t_len, LANES))
    s1 = np.zeros((t_len, LANES))
    s2 = np.zeros((t_len, LANES))
    a, b, e = _ROPE_LO, _ROPE_LO + _ROPE_HALF, _ROPE_LO + ROPE_DIM
    c[:, a:b] = cos
    c[:, b:e] = cos
    s1[:, a:b] = -sin
    s2[:, b:e] = sin
    return tuple(jnp.asarray(t, F32) for t in (c, s1, s2))


def _apply_rope(x, c, s1, s2):
    up = pltpu.roll(x, LANES - _ROPE_HALF, 1)
    dn = pltpu.roll(x, _ROPE_HALF, 1)
    return x * c + up * s1 + dn * s2


def _mla_ctx_kernel(q_ref, kv_ref, kr_ref, o_ref):
    kr = kr_ref[...]
    v_all = kv_ref[:, _MLA_W:].astype(BF16)
    out = None
    for h in range(MLA_HEADS):
        qh = q_ref[:, h * LANES:(h + 1) * LANES].astype(BF16)
        kh = (kv_ref[:, h * LANES:(h + 1) * LANES] + kr).astype(BF16)
        s = lax.dot_general(qh, kh, _NT, preferred_element_type=F32) * MLA_SCALE
        p = jnp.exp(s - jnp.max(s, axis=-1, keepdims=True))
        l = jnp.sum(p, axis=-1, keepdims=True)
        r = jnp.dot(p.astype(BF16), v_all, preferred_element_type=F32) / l
        out = r if out is None else jnp.where(_head_mask(MLA_HEADS * V_DIM, h, V_DIM), r, out)
    o_ref[...] = out


def _mla_ctx(q_m, kv_m, z_mla):
    wv = MLA_HEADS * V_DIM
    return pl.pallas_call(
        _mla_ctx_kernel,
        out_shape=jax.ShapeDtypeStruct((N_CTX, wv), F32),
        grid=(BATCH,),
        in_specs=[pl.BlockSpec((SEQ, _MLA_W), lambda b: (b, 0)),
                  pl.BlockSpec((SEQ, _MLA_W + wv), lambda b: (b, 0)),
                  pl.BlockSpec((SEQ, LANES), lambda b: (b, 3))],
        out_specs=pl.BlockSpec((SEQ, wv), lambda b: (b, 0)),
        compiler_params=_params("parallel"),
        name="mla_ctx",
    )(q_m, kv_m, z_mla)


_MLA_BQ = 256


def _mla_lat_kernel(q_ref, kv_ref, kr_ref, kvc_ref, krc_ref, cq_ref, s1q_ref, s2q_ref, ck_ref, s1k_ref, s2k_ref,
                    o_ref):
    kr = _apply_rope(kr_ref[...], ck_ref[...], s1k_ref[...], s2k_ref[...])
    krc = krc_ref[...]
    v_lat = kv_ref[:, _MLA_W:].astype(BF16)
    v_ctx = kvc_ref[:, _MLA_W:].astype(BF16)
    out = None
    for h in range(MLA_HEADS):
        blk = slice(h * LANES, (h + 1) * LANES)
        qh = _apply_rope(q_ref[:, blk], cq_ref[...], s1q_ref[...], s2q_ref[...]).astype(BF16)
        kl = (kv_ref[:, blk] + kr).astype(BF16)
        kc = (kvc_ref[:, blk] + krc).astype(BF16)
        s_l = lax.dot_general(qh, kl, _NT, preferred_element_type=F32) * MLA_SCALE
        s_c = lax.dot_general(qh, kc, _NT, preferred_element_type=F32) * MLA_SCALE
        p_l, p_c, l = _softmax2(s_l, s_c)
        r = (jnp.dot(p_l.astype(BF16), v_lat, preferred_element_type=F32)
             + jnp.dot(p_c.astype(BF16), v_ctx, preferred_element_type=F32)) / l
        out = r if out is None else jnp.where(_head_mask(MLA_HEADS * V_DIM, h, V_DIM), r, out)
    o_ref[...] = out


def _mla_latent(q_m, kv_m, z_mla, kv_cache, kr_cache, rope):
    wv = MLA_HEADS * V_DIM
    nq = DEC_SEQ // _MLA_BQ
    q0 = N_CTX // _MLA_BQ
    lat0 = N_CTX // DEC_SEQ
    c, s1, s2 = rope
    tq = pl.BlockSpec((_MLA_BQ, LANES), lambda b, j: (j, 0))
    tk = pl.BlockSpec((DEC_SEQ, LANES), lambda b, j: (0, 0))
    return pl.pallas_call(
        _mla_lat_kernel,
        out_shape=jax.ShapeDtypeStruct((N_LAT, wv), F32),
        grid=(DEC_BATCH, nq),
        in_specs=[pl.BlockSpec((_MLA_BQ, _MLA_W), lambda b, j: (q0 + b * nq + j, 0)),
                  pl.BlockSpec((DEC_SEQ, _MLA_W + wv), lambda b, j: (lat0 + b, 0)),
                  pl.BlockSpec((DEC_SEQ, LANES), lambda b, j: (lat0 + b, 3)),
                  pl.BlockSpec((PAST_LEN, _MLA_W + wv), lambda b, j: (b, 0)),
                  pl.BlockSpec((PAST_LEN, LANES), lambda b, j: (b, 0)),
                  tq, tq, tq, tk, tk, tk],
        out_specs=pl.BlockSpec((_MLA_BQ, wv), lambda b, j: (b * nq + j, 0)),
        compiler_params=_params("parallel", "arbitrary"),
        name="mla_lat",
    )(q_m, kv_m, z_mla, kv_cache, kr_cache, c, s1, s2, c, s1, s2)


def _head_sum(x, ones_ref):
    return jnp.dot(x, ones_ref[...], preferred_element_type=F32, precision=lax.Precision.HIGHEST)


def _rw_pre_kernel(r_ref, k_ref, v_ref, wf_ref, wb_ref, af_ref, ab_ref, kkw_ref, ka_ref, rk_ref, ones_ref,
                   dwf_ref, dwb_ref, kdf_ref, kdb_ref, bf_ref, bb_ref, kk_ref, bonus_ref):
    k = k_ref[...]
    kk = k * kkw_ref[...]
    nrm = jnp.sqrt(_head_sum(kk * kk, ones_ref))
    kk = kk / jnp.maximum(nrm, 1e-12)
    kk_ref[...] = kk
    kd_sum = None
    for wl_ref, al_ref, dw_ref, kd_ref, b_ref in ((wf_ref, af_ref, dwf_ref, kdf_ref, bf_ref),
                                                  (wb_ref, ab_ref, dwb_ref, kdb_ref, bb_ref)):
        dw_ref[...] = jnp.exp(-float(np.exp(-0.5)) * _sigmoid(wl_ref[...]))
        a = _sigmoid(al_ref[...])
        kd = k * (1.0 + (a - 1.0) * ka_ref[...])
        kd_ref[...] = kd
        b_ref[...] = kk * a
        kd_sum = kd if kd_sum is None else kd_sum + kd
    bonus_ref[...] = _head_sum(r_ref[...] * rk_ref[...] * kd_sum, ones_ref) * v_ref[...]


def _rw_pre(z_rw, wa, kk_w, ka, rk, ones):
    tm = 512
    n = z_rw.shape[0]

    def col(c):
        return pl.BlockSpec((tm, RW_DIM), lambda i: (i, c))

    vec = pl.BlockSpec((1, RW_DIM), lambda i: (0, 0))
    out = jax.ShapeDtypeStruct((n, RW_DIM), F32)
    return pl.pallas_call(
        _rw_pre_kernel,
        out_shape=(out,) * 8,
        grid=(n // tm,),
        in_specs=[col(0), col(1), col(2), col(0), col(1), col(2), col(3), vec, vec, vec,
                  pl.BlockSpec((RW_DIM, RW_DIM), lambda i: (0, 0))],
        out_specs=(pl.BlockSpec((tm, RW_DIM), lambda i: (i, 0)),) * 8,
        compiler_params=_params("parallel"),
        name="rwkv_pre",
    )(z_rw, z_rw, z_rw, wa, wa, wa, wa, kk_w.reshape(1, RW_DIM), ka.reshape(1, RW_DIM),
      rk.reshape(1, RW_DIM), ones)


_SCAN_OPS = 6


def _half_swap(x):
    return pltpu.roll(x, LANES // 2, 1)


def _rw_scan_kernel(*refs, n_src, nb, reverse):
    ins = [refs[s * _SCAN_OPS:(s + 1) * _SCAN_OPS] for s in range(n_src)]
    s0_ref = refs[n_src * _SCAN_OPS]
    y_refs = refs[n_src * _SCAN_OPS + 1:n_src * _SCAN_OPS + 1 + n_src]
    sf_ref, ops_scr, y_scr, s_scr = refs[n_src * _SCAN_OPS + 1 + n_src:]
    tc = SCAN_TC
    c = pl.program_id(0)
    used = n_src * RW_HEADS * nb
    left = lax.broadcasted_iota(jnp.int32, (nb, LANES), 1) < LANES // 2

    @pl.when(c == 0)
    def _():
        s_scr[...] = s0_ref[...]

    def times(p, s):
        return (tc - 1 - 2 * p, tc - 2 - 2 * p) if reverse[s] else (2 * p, 2 * p + 1)

    def relayout_in(p, carry):
        for o in range(_SCAN_OPS):
            rows = []
            for s in range(n_src):
                t0, t1 = times(p, s)
                a0 = ins[s][o][:, t0, :]
                a1 = ins[s][o][:, t1, :]
                for h in range(RW_HEADS):
                    j = h // 2
                    p0 = a0[:, LANES * j:LANES * (j + 1)]
                    p1 = a1[:, LANES * j:LANES * (j + 1)]
                    if h % 2 == 0:
                        rows.append(jnp.where(left, p0, _half_swap(p1)))
                    else:
                        rows.append(jnp.where(left, _half_swap(p0), p1))
            if used < LANES:
                rows.append(jnp.zeros((LANES - used, LANES), F32))
            tile = jnp.concatenate(rows, axis=0).T
            ops_scr[o, 2 * p] = tile[0:HEAD_DIM]
            ops_scr[o, 2 * p + 1] = tile[HEAD_DIM:]
        return carry

    lax.fori_loop(0, tc // 2, relayout_in, 0)

    def step(t, carry):
        w = ops_scr[0, t]
        kk = ops_scr[1, t]
        b = ops_scr[2, t]
        kd = ops_scr[3, t]
        r = ops_scr[4, t]
        for vi in range(HEAD_DIM):
            s = s_scr[vi]
            sa = jnp.sum(s * kk, axis=0, keepdims=True)
            s = s * w - sa * b + ops_scr[5, t, pl.ds(vi, 1), :] * kd
            s_scr[vi] = s
            y_scr[t, pl.ds(vi, 1), :] = jnp.sum(s * r, axis=0, keepdims=True)
        return carry

    lax.fori_loop(0, tc, step, 0)

    def relayout_out(p, carry):
        tile = jnp.concatenate([y_scr[2 * p], y_scr[2 * p + 1]], axis=0).T
        for s in range(n_src):
            t0, t1 = times(p, s)
            for j in range(RW_HEADS // 2):
                r0 = (s * RW_HEADS + 2 * j) * nb
                even = tile[r0:r0 + nb]
                odd = tile[r0 + nb:r0 + 2 * nb]
                y_refs[s][t0, :, LANES * j:LANES * (j + 1)] = jnp.where(left, even, _half_swap(odd))
                y_refs[s][t1, :, LANES * j:LANES * (j + 1)] = jnp.where(left, _half_swap(even), odd)
        return carry

    lax.fori_loop(0, tc // 2, relayout_out, 0)

    @pl.when(c == pl.num_programs(0) - 1)
    def _():
        sf_ref[...] = s_scr[...]


def _rw_scan(srcs, s0, nb, t, reverse):
    n_src = len(srcs)
    tc = SCAN_TC
    nc = t // tc

    def tblk(s, c):
        return nc - 1 - c if reverse[s] else c

    in_specs, args = [], []
    for s in range(n_src):
        for arr, cb in srcs[s]:
            in_specs.append(pl.BlockSpec((nb, tc, RW_DIM), lambda c, s=s, cb=cb: (0, tblk(s, c), cb)))
            args.append(arr)
    st = pl.BlockSpec((HEAD_DIM, HEAD_DIM, LANES), lambda c: (0, 0, 0))
    y_specs = tuple(pl.BlockSpec((tc, nb, RW_DIM), lambda c, s=s: (tblk(s, c), 0, 0)) for s in range(n_src))
    out = pl.pallas_call(
        functools.partial(_rw_scan_kernel, n_src=n_src, nb=nb, reverse=tuple(reverse)),
        out_shape=tuple(jax.ShapeDtypeStruct((t, nb, RW_DIM), F32) for _ in range(n_src))
        + (jax.ShapeDtypeStruct((HEAD_DIM, HEAD_DIM, LANES), F32),),
        grid=(nc,),
        in_specs=in_specs + [st],
        out_specs=y_specs + (st,),
        scratch_shapes=[pltpu.VMEM((_SCAN_OPS, tc, HEAD_DIM, LANES), F32),
                        pltpu.VMEM((tc, HEAD_DIM, LANES), F32),
                        pltpu.VMEM((HEAD_DIM, HEAD_DIM, LANES), F32)],
        compiler_params=_params("arbitrary"),
        name="rwkv_scan",
    )(*args, s0)
    return out[:n_src], out[n_src]


def _rw_post_kernel(yf_ref, yb_ref, bonus_ref, g_ref, gw_ref, gb_ref, ones_ref, o_ref, *, nb):
    inv = 1.0 / HEAD_DIM

    def body(b, carry):
        y = yf_ref[:, b, :] + yb_ref[:, b, :]
        mu = _head_sum(y, ones_ref) * inv
        d = y - mu
        var = _head_sum(d * d, ones_ref) * inv
        yn = d * lax.rsqrt(var + GN_EPS) * gw_ref[...] + gb_ref[...]
        o_ref[b] = (yn + bonus_ref[b]) * g_ref[b]
        return carry

    lax.fori_loop(0, nb, body, 0)


def _rw_post(y_f, y_b, bonus, g, gn_w, gn_b, ones, nb, t):
    tt = SCAN_TC
    tok = pl.BlockSpec((nb, tt, RW_DIM), lambda c: (0, c, 0))
    tmaj = pl.BlockSpec((tt, nb, RW_DIM), lambda c: (c, 0, 0))
    vec = pl.BlockSpec((1, RW_DIM), lambda c: (0, 0))
    out = pl.pallas_call(
        functools.partial(_rw_post_kernel, nb=nb),
        out_shape=jax.ShapeDtypeStruct((nb, t, RW_DIM), F32),
        grid=(t // tt,),
        in_specs=[tmaj, tmaj, tok, tok, vec, vec, pl.BlockSpec((RW_DIM, RW_DIM), lambda c: (0, 0))],
        out_specs=tok,
        compiler_params=_params("parallel"),
        name="rwkv_post",
    )(y_f, y_b, bonus.reshape(nb, t, RW_DIM), g.reshape(nb, t, RW_DIM), gn_w.reshape(1, RW_DIM),
      gn_b.reshape(1, RW_DIM), ones)
    return out.reshape(nb * t, RW_DIM)


def _scan_state(states):
    z = jnp.concatenate([s.transpose(2, 3, 1, 0).reshape(HEAD_DIM, HEAD_DIM, -1) for s in states], axis=-1)
    return jnp.pad(z, ((0, 0), (0, 0), (0, LANES - z.shape[-1])))


def _unscan_state(sf, src, nb):
    n = RW_HEADS * nb
    return sf[:, :, src * n:(src + 1) * n].reshape(HEAD_DIM, HEAD_DIM, RW_HEADS, nb).transpose(3, 2, 0, 1)


def _rwkv(h, row0, nb, t, w_rw, lora_w, lora_b, g2, lp, ones, s0_f, s0_b, fused_dirs):
    n = nb * t
    z_rw = _mm(h, w_rw, a_rows=(row0, n), name="proj_rw")
    wa = _mm(z_rw, lora_w, bias=lora_b, act=_lora_act, a_cols=(256, 3), name="rw_lora")
    g_rw = _mm(z_rw, g2, act=_sigmoid, a_cols=(GATE_LORA, 8), name="rw_gate")
    dw_f, dw_b, kd_f, kd_b, b_f, b_b, kk, bonus = (
        a.reshape(nb, t, RW_DIM) for a in _rw_pre(z_rw, wa, lp["kk"], lp["ka"], lp["rk"], ones))
    z3 = z_rw.reshape(nb, t, -1)
    fwd = ((dw_f, 0), (kk, 0), (b_f, 0), (kd_f, 0), (z3, 0), (z3, 2))
    bwd = ((dw_b, 0), (kk, 0), (b_b, 0), (kd_b, 0), (z3, 0), (z3, 2))
    if fused_dirs:
        (y_f, y_b), sf = _rw_scan([fwd, bwd], _scan_state([s0_f, s0_b]), nb, t, [False, True])
        s_f, s_b = _unscan_state(sf, 0, nb), _unscan_state(sf, 1, nb)
    else:
        (y_f,), sf_f = _rw_scan([fwd], _scan_state([s0_f]), nb, t, [False])
        (y_b,), sf_b = _rw_scan([bwd], _scan_state([s0_b]), nb, t, [True])
        s_f, s_b = _unscan_state(sf_f, 0, nb), _unscan_state(sf_b, 0, nb)
    out = _rw_post(y_f, y_b, bonus, g_rw, lp["gn_w"], lp["gn_b"], ones, nb, t)
    return out, s_f, s_b


_TM_MERGE = 512


def _merge_kernel(x_ref, gm_ref, h_ref, nac_ref, nal_ref, rwc_ref, rwl_ref, mlc_ref, mll_ref, wg_ref, wbr_ref, wo_ref,
                  o_ref):
    is_ctx = pl.program_id(0) < N_CTX // _TM_MERGE
    h = h_ref[...]
    branches = tuple(jnp.where(is_ctx, c_ref[...], l_ref[...])
                     for c_ref, l_ref in ((nac_ref, nal_ref), (rwc_ref, rwl_ref), (mlc_ref, mll_ref)))
    m = None
    for i, o_b in enumerate(branches):
        gate = jnp.dot(h, wg_ref[:, i * D_MODEL:(i + 1) * D_MODEL], preferred_element_type=F32)
        br = jnp.dot(o_b.astype(BF16), wbr_ref[i], preferred_element_type=F32)
        t = _sigmoid(gate) * br
        m = t if m is None else m + t
    mix = jnp.dot(m.astype(BF16), wo_ref[...], preferred_element_type=F32)
    o_ref[...] = x_ref[...] + gm_ref[...] * mix


def _merge(x, mod_l, h, na_ctx, na_lat, rw_ctx, rw_lat, mla_ctx, mla_lat, w_gate, w_br, w_o):
    tm = _TM_MERGE
    n_ctx = N_CTX // tm
    row = pl.BlockSpec((tm, D_MODEL), lambda i: (i, 0))
    br_ctx = pl.BlockSpec((tm, BRANCH_DIM), lambda i: (jnp.minimum(i, n_ctx - 1), 0))
    br_lat = pl.BlockSpec((tm, BRANCH_DIM), lambda i: (jnp.maximum(i - n_ctx, 0), 0))
    return pl.pallas_call(
        _merge_kernel,
        out_shape=jax.ShapeDtypeStruct((N_TOK, D_MODEL), F32),
        grid=(N_TOK // tm,),
        in_specs=[row, pl.BlockSpec((None, 1, D_MODEL), lambda i: (_cond_row(i, tm), 0, 5)),
                  row, br_ctx, br_lat, br_ctx, br_lat, br_ctx, br_lat,
                  pl.BlockSpec((D_MODEL, N_BRANCH * D_MODEL), lambda i: (0, 0)),
                  pl.BlockSpec((N_BRANCH, BRANCH_DIM, D_MODEL), lambda i: (0, 0, 0)),
                  pl.BlockSpec((D_MODEL, D_MODEL), lambda i: (0, 0))],
        out_specs=row,
        compiler_params=_params("arbitrary"),
        name="merge",
    )(x, mod_l, h, na_ctx, na_lat, rw_ctx, rw_lat, mla_ctx, mla_lat, w_gate, w_br, w_o)


def _lora_act(a):
    col = lax.broadcasted_iota(jnp.int32, a.shape, 1)
    return jnp.where(col < 2 * DECAY_LORA, jnp.tanh(a), a)


def _block_diag(blocks):
    rows = sum(b.shape[0] for b in blocks)
    cols = sum(b.shape[1] for b in blocks)
    out = jnp.zeros((rows, cols), blocks[0].dtype)
    r = c = 0
    for b in blocks:
        out = lax.dynamic_update_slice(out, b, (r, c))
        r += b.shape[0]
        c += b.shape[1]
    return out


def _pad_cols(w, left, total):
    return jnp.pad(w, ((0, 0), (left, total - left - w.shape[1])))


def _mla_uq_layout(w_uq):
    per = NOPE_DIM + ROPE_DIM
    return jnp.concatenate([_pad_cols(w_uq[:, h * per:(h + 1) * per], 0, LANES) for h in range(MLA_HEADS)], axis=1)


def _mla_ukv_layout(w_ukv):
    per = NOPE_DIM + V_DIM
    k = [_pad_cols(w_ukv[:, h * per:h * per + NOPE_DIM], 0, LANES) for h in range(MLA_HEADS)]
    v = [w_ukv[:, h * per + NOPE_DIM:(h + 1) * per] for h in range(MLA_HEADS)]
    return jnp.concatenate(k + v, axis=1)


def _heads_on_lanes(cache):
    b, h, t, d = cache.shape
    return cache.transpose(0, 2, 1, 3).reshape(b, t, h * d)


def _heads_major(z, nb, t, nh):
    return z.reshape(nb, t, nh, -1).transpose(0, 2, 1, 3)


def kernel(x_prompt, x_sample, c, cache_na_k, cache_na_v, cache_mla_ckv, cache_mla_krope, state_rwkv_fwd, state_rwkv_bwd, c_ctx, ada_w, ada_b, norm_g, ffn_wg, ffn_wu, ffn_wd, w_in, na_rpb, rw_w0, rw_w2, rw_a0, rw_a2, rw_g2, rw_kk, rw_ka, rw_rk, rw_gn_w, rw_gn_b, mla_qn_g, mla_kvn_g, mla_w_uq, mla_w_ukv, w_br, w_o, final_g):
    x = jnp.concatenate([x_prompt.reshape(N_CTX, D_MODEL), x_sample.reshape(N_LAT, D_MODEL)], axis=0)
    cond = jnp.concatenate([c_ctx[None, :], c, jnp.zeros((N_COND - 1 - DEC_BATCH, D_MODEL), F32)], axis=0)
    mod = _modulation(cond, ada_w, ada_b)

    ones = jnp.asarray(np.kron(np.eye(RW_HEADS), np.ones((HEAD_DIM, HEAD_DIM))), F32)
    rope = _rope_tables(DEC_SEQ)
    zero_state = jnp.zeros((BATCH, RW_HEADS, HEAD_DIM, HEAD_DIM), F32)

    col_rw = 3 * NA_HEADS * HEAD_DIM
    col_mla = col_rw + 3 * RW_DIM + 2 * DECAY_LORA + 2 * ICLR_LORA + GATE_LORA
    col_kr = col_mla + Q_LORA + KV_LORA
    col_gate = col_kr + ROPE_DIM
    kr_lane0 = Q_LORA + KV_LORA + _ROPE_LO

    caches = []
    y = None
    for l in range(DEPTH):
        mod_l = mod[l]
        wi = w_in[l].astype(BF16)
        w_na = wi[:, :col_rw]
        w_rw = wi[:, col_rw:col_mla]
        w_mla = jnp.concatenate([wi[:, col_mla:col_kr], _pad_cols(wi[:, col_kr:col_gate], _ROPE_LO, LANES)], axis=1)
        w_gate = wi[:, col_gate:]

        x, h = _ffn(x, norm_g[l, 0], mod_l, 0, ffn_wg[l, 0].astype(BF16), ffn_wu[l, 0].astype(BF16),
                    ffn_wd[l, 0].astype(BF16), post="mod", post_g=norm_g[l, 1], post_chunk=3)

        z_na = _mm(h, w_na, name="proj_na")
        na_ctx = _na_ctx(z_na)
        na_lat = _na_latent(na_rpb[l].reshape(NA_HEADS * (2 * NA_WIN_H - 1), 2 * NA_WIN_W - 1), z_na,
                            _heads_on_lanes(cache_na_k[:, l]), _heads_on_lanes(cache_na_v[:, l]))

        lora_w = _block_diag([rw_w2[l, 0], rw_w2[l, 1], rw_a2[l, 0], rw_a2[l, 1]]).astype(BF16)
        lora_b = jnp.concatenate([rw_w0[l, 0], rw_w0[l, 1], rw_a0[l, 0], rw_a0[l, 1]])
        lp = {"kk": rw_kk[l], "ka": rw_ka[l], "rk": rw_rk[l].reshape(RW_DIM), "gn_w": rw_gn_w[l], "gn_b": rw_gn_b[l]}
        rw_args = (w_rw, lora_w, lora_b, rw_g2[l].astype(BF16), lp, ones)
        rw_ctx, sc_f, sc_b = _rwkv(h, 0, BATCH, SEQ, *rw_args, zero_state, zero_state, fused_dirs=False)
        rw_lat, _, _ = _rwkv(h, N_CTX, DEC_BATCH, DEC_SEQ, *rw_args, state_rwkv_fwd[:, l], state_rwkv_bwd[:, l],
                             fused_dirs=True)

        z_mla = _mm(h, w_mla, name="proj_mla")
        cq = _norm(z_mla, mla_qn_g[l], BF16, (Q_LORA, 0), "mla_qnorm")
        ckv = _norm(z_mla, mla_kvn_g[l], F32, (KV_LORA, Q_LORA // KV_LORA), "mla_kvnorm")
        q_m = _mm(cq, _mla_uq_layout(mla_w_uq[l]).astype(BF16), name="mla_uq")
        w_ukv = _mla_ukv_layout(mla_w_ukv[l]).astype(BF16)
        kv_m = _mm(ckv, w_ukv, name="mla_ukv")
        kv_cache = _mm(cache_mla_ckv[:, l].reshape(DEC_BATCH * PAST_LEN, KV_LORA), w_ukv, name="mla_ukv_cache")
        kr_cache = _pad_cols(cache_mla_krope[:, l].reshape(DEC_BATCH * PAST_LEN, ROPE_DIM), _ROPE_LO, LANES)
        mla_ctx = _mla_ctx(q_m, kv_m, z_mla)
        mla_lat = _mla_latent(q_m, kv_m, z_mla, kv_cache, kr_cache, rope)

        x = _merge(x, mod_l, h, na_ctx, na_lat, rw_ctx, rw_lat, mla_ctx, mla_lat, w_gate,
                   w_br[l].astype(BF16), w_o[l].astype(BF16))

        ffn2 = (x, norm_g[l, 2], mod_l, 6, ffn_wg[l, 1].astype(BF16), ffn_wu[l, 1].astype(BF16),
                ffn_wd[l, 1].astype(BF16))
        if l == DEPTH - 1:
            y = _ffn(*ffn2, post="plain", post_g=final_g)
        else:
            x = _ffn(*ffn2)

        w = NA_HEADS * HEAD_DIM
        caches.append((_heads_major(z_na[:N_CTX, w:2 * w], BATCH, SEQ, NA_HEADS),
                       _heads_major(z_na[:N_CTX, 2 * w:3 * w], BATCH, SEQ, NA_HEADS),
                       ckv[:N_CTX].reshape(BATCH, SEQ, KV_LORA),
                       z_mla[:N_CTX, kr_lane0:kr_lane0 + ROPE_DIM].reshape(BATCH, SEQ, ROPE_DIM),
                       sc_f, sc_b))

    y_prompt = y[:N_CTX].reshape(BATCH, SEQ, D_MODEL)
    y_sample = y[N_CTX:].reshape(DEC_BATCH, DEC_SEQ, D_MODEL)
    outs = [jnp.stack([cl[i] for cl in caches], axis=1) for i in range(6)]
    return (y_prompt, y_sample, *outs)
```

```python
import functools

import numpy as np
import jax
import jax.numpy as jnp
from jax import lax
from jax.experimental import pallas as pl
from jax.experimental.pallas import tpu as pltpu

F32 = jnp.float32
BF16 = jnp.bfloat16

D_MODEL = 1024
BATCH = 32
SEQ = 256
DEPTH = 2
DEC_BATCH = 8
DEC_SEQ = 1024
PAST_LEN = 256
GRID_W = 64
HEAD_DIM = 64
NA_HEADS = 4
NA_WIN_H = 8
NA_WIN_W = 16
RW_HEADS = 4
RW_DIM = RW_HEADS * HEAD_DIM
DECAY_LORA = 64
ICLR_LORA = 64
GATE_LORA = 128
GN_EPS = 64e-5
MLA_HEADS = 4
Q_LORA = 256
KV_LORA = 128
NOPE_DIM = 64
ROPE_DIM = 32
V_DIM = 64
MLA_SCALE = (NOPE_DIM + ROPE_DIM) ** -0.5
ROPE_BASE = 10000.0
N_BRANCH = 3
BRANCH_DIM = 256
D_FF = 2816
N_MOD = 9
NORM_EPS = 1e-6
NEG_INF = -1e30

N_CTX = BATCH * SEQ
N_LAT = DEC_BATCH * DEC_SEQ
N_TOK = N_CTX + N_LAT
N_COND = 16

LANES = 128
VMEM_LIMIT = 48 * 1024 * 1024

TM = 1024
TF = 256
SCAN_TC = 32


def _cond_row(i, tm):
    return jnp.where(i * tm < N_CTX, 0, 1 + (i * tm - N_CTX) // DEC_SEQ)


def _sigmoid(x):
    return 1.0 / (1.0 + jnp.exp(-x))


def _params(*sem):
    return pltpu.CompilerParams(dimension_semantics=sem, vmem_limit_bytes=VMEM_LIMIT)


def _mm_kernel(a_ref, w_ref, *rest, act, has_bias):
    if has_bias:
        b_ref, o_ref = rest
    else:
        (o_ref,) = rest
    a = a_ref[...]
    if act is not None:
        a = act(a.astype(F32))
    acc = jnp.dot(a.astype(BF16), w_ref[...].astype(BF16), preferred_element_type=F32)
    if has_bias:
        acc = acc + b_ref[...]
    o_ref[...] = acc.astype(o_ref.dtype)


def _mm(a, w, bias=None, act=None, out_dtype=F32, a_cols=None, a_rows=None, tn=None, name="mm"):
    row0, m = (0, a.shape[0]) if a_rows is None else a_rows
    k, n = w.shape
    kb = 0
    if a_cols is not None:
        assert a_cols[0] == k
        kb = a_cols[1]
    else:
        assert a.shape[1] == k
    tm = min(m, TM)
    tn = n if tn is None else tn
    assert m % tm == 0 and n % tn == 0 and row0 % tm == 0
    i0 = row0 // tm
    in_specs = [pl.BlockSpec((tm, k), lambda i, j: (i0 + i, kb)),
                pl.BlockSpec((k, tn), lambda i, j: (0, j))]
    args = [a, w]
    if bias is not None:
        in_specs.append(pl.BlockSpec((1, tn), lambda i, j: (0, j)))
        args.append(bias.reshape(1, n).astype(F32))
    return pl.pallas_call(
        functools.partial(_mm_kernel, act=act, has_bias=bias is not None),
        out_shape=jax.ShapeDtypeStruct((m, n), out_dtype),
        grid=(m // tm, n // tn),
        in_specs=in_specs,
        out_specs=pl.BlockSpec((tm, tn), lambda i, j: (i, j)),
        compiler_params=_params("parallel", "arbitrary"),
        name=name,
    )(*args)


def _mod_kernel(c_ref, w_ref, b_ref, o_ref):
    c = c_ref[...]
    a = (c * _sigmoid(c)).astype(BF16)
    o_ref[...] = jnp.dot(a, w_ref[...].astype(BF16), preferred_element_type=F32) + b_ref[...]


def _modulation(cond, ada_w, ada_b):
    n = N_MOD * D_MODEL
    tn = D_MODEL
    out = pl.pallas_call(
        _mod_kernel,
        out_shape=jax.ShapeDtypeStruct((DEPTH, N_COND, n), F32),
        grid=(DEPTH, n // tn),
        in_specs=[pl.BlockSpec((N_COND, D_MODEL), lambda l, j: (0, 0)),
                  pl.BlockSpec((None, D_MODEL, tn), lambda l, j: (l, 0, j)),
                  pl.BlockSpec((None, 1, tn), lambda l, j: (l, 0, j))],
        out_specs=pl.BlockSpec((None, N_COND, tn), lambda l, j: (l, 0, j)),
        compiler_params=_params("parallel", "arbitrary"),
        name="modulation",
    )(cond, ada_w, ada_b.reshape(DEPTH, 1, n))
    return out.reshape(DEPTH, N_COND, 1, n)


def _rms(x, g):
    return x * lax.rsqrt(jnp.mean(x * x, axis=-1, keepdims=True) + NORM_EPS) * g


def _norm_kernel(x_ref, g_ref, o_ref):
    o_ref[...] = _rms(x_ref[...].astype(F32), g_ref[...]).astype(o_ref.dtype)


def _norm(x, g, out_dtype, x_cols, name):
    m = x.shape[0]
    width, cb = x_cols
    tm = min(m, TM)
    return pl.pallas_call(
        _norm_kernel,
        out_shape=jax.ShapeDtypeStruct((m, width), out_dtype),
        grid=(m // tm,),
        in_specs=[pl.BlockSpec((tm, width), lambda i: (i, cb)),
                  pl.BlockSpec((1, width), lambda i: (0, 0))],
        out_specs=pl.BlockSpec((tm, width), lambda i: (i, 0)),
        compiler_params=_params("parallel"),
        name=name,
    )(x, g.reshape(1, width).astype(F32))


def _ffn_kernel(x_ref, g_ref, sh_ref, sc_ref, gt_ref, wg_ref, wu_ref, wd_ref, *rest, post):
    if post == "mod":
        g2_ref, sh2_ref, sc2_ref, o_ref, h2_ref, h_scr, acc_scr = rest
    elif post == "plain":
        g2_ref, y_ref, h_scr, acc_scr = rest
    else:
        o_ref, h_scr, acc_scr = rest
    f = pl.program_id(1)

    @pl.when(f == 0)
    def _():
        y = _rms(x_ref[...], g_ref[...])
        h_scr[...] = (y * (1.0 + sc_ref[...]) + sh_ref[...]).astype(BF16)
        acc_scr[...] = jnp.zeros_like(acc_scr)

    h = h_scr[...]
    gg = jnp.dot(h, wg_ref[...], preferred_element_type=F32)
    uu = jnp.dot(h, wu_ref[...], preferred_element_type=F32)
    a = (gg * _sigmoid(gg)) * uu
    acc_scr[...] += jnp.dot(a.astype(BF16), wd_ref[...], preferred_element_type=F32)

    @pl.when(f == pl.num_programs(1) - 1)
    def _():
        xn = x_ref[...] + 0.5 * gt_ref[...] * acc_scr[...]
        if post == "plain":
            y_ref[...] = _rms(xn, g2_ref[...])
        else:
            o_ref[...] = xn
        if post == "mod":
            h2_ref[...] = (_rms(xn, g2_ref[...]) * (1.0 + sc2_ref[...]) + sh2_ref[...]).astype(BF16)


def _ffn(x, norm_g, mod_l, chunk0, wg, wu, wd, post=None, post_g=None, post_chunk=None):
    def mod_spec(c):
        return pl.BlockSpec((None, 1, D_MODEL), lambda i, f: (_cond_row(i, TM), 0, c))

    row = pl.BlockSpec((TM, D_MODEL), lambda i, f: (i, 0))
    vec = pl.BlockSpec((1, D_MODEL), lambda i, f: (0, 0))
    in_specs = [row, vec, mod_spec(chunk0), mod_spec(chunk0 + 1), mod_spec(chunk0 + 2),
                pl.BlockSpec((D_MODEL, TF), lambda i, f: (0, f)),
                pl.BlockSpec((D_MODEL, TF), lambda i, f: (0, f)),
                pl.BlockSpec((TF, D_MODEL), lambda i, f: (f, 0))]
    args = [x, norm_g.reshape(1, D_MODEL), mod_l, mod_l, mod_l, wg, wu, wd]
    xs = jax.ShapeDtypeStruct((N_TOK, D_MODEL), F32)
    if post == "mod":
        in_specs += [vec, mod_spec(post_chunk), mod_spec(post_chunk + 1)]
        args += [post_g.reshape(1, D_MODEL), mod_l, mod_l]
        out_shape, out_specs = (xs, jax.ShapeDtypeStruct((N_TOK, D_MODEL), BF16)), (row, row)
    elif post == "plain":
        in_specs += [vec]
        args += [post_g.reshape(1, D_MODEL)]
        out_shape, out_specs = xs, row
    else:
        out_shape, out_specs = xs, row
    return pl.pallas_call(
        functools.partial(_ffn_kernel, post=post),
        out_shape=out_shape,
        grid=(N_TOK // TM, D_FF // TF),
        in_specs=in_specs,
        out_specs=out_specs,
        scratch_shapes=[pltpu.VMEM((TM, D_MODEL), BF16), pltpu.VMEM((TM, D_MODEL), F32)],
        compiler_params=_params("parallel", "arbitrary"),
        name="ffn",
    )(*args)


_NT = (((1,), (1,)), ((), ()))
_LAT_ROWS = DEC_SEQ // GRID_W
_NA_KH = min(NA_WIN_H, _LAT_ROWS)
_NA_LOCAL = _NA_KH * GRID_W
_NA_PAIRS = 2 * NA_WIN_H - 2


def _head_mask(width, h, group):
    lane = lax.broadcasted_iota(jnp.int32, (1, width), 1)
    return (lane >= h * group) & (lane < (h + 1) * group)


def _stack_heads(q, nh, group):
    return jnp.concatenate([jnp.where(_head_mask(q.shape[1], h, group), q, 0.0) for h in range(nh)], axis=0)


def _pick_heads(r, nh, group):
    t = r.shape[0] // nh
    out = r[0:t]
    for h in range(1, nh):
        out = jnp.where(_head_mask(r.shape[1], h, group), r[h * t:(h + 1) * t], out)
    return out


def _softmax2(s_a, s_b):
    m = jnp.maximum(jnp.max(s_a, axis=-1, keepdims=True), jnp.max(s_b, axis=-1, keepdims=True))
    p_a = jnp.exp(s_a - m)
    p_b = jnp.exp(s_b - m)
    return p_a, p_b, jnp.sum(p_a, axis=-1, keepdims=True) + jnp.sum(p_b, axis=-1, keepdims=True)


def _na_ctx_kernel(q_ref, k_ref, v_ref, o_ref):
    q_st = _stack_heads(q_ref[...], NA_HEADS, HEAD_DIM).astype(BF16)
    s = lax.dot_general(q_st, k_ref[...].astype(BF16), _NT, preferred_element_type=F32) * HEAD_DIM ** -0.5
    p = jnp.exp(s - jnp.max(s, axis=-1, keepdims=True))
    l = jnp.sum(p, axis=-1, keepdims=True)
    r = jnp.dot(p.astype(BF16), v_ref[...].astype(BF16), preferred_element_type=F32) / l
    o_ref[...] = _pick_heads(r, NA_HEADS, HEAD_DIM)


def _na_ctx(z_na):
    w = NA_HEADS * HEAD_DIM
    return pl.pallas_call(
        _na_ctx_kernel,
        out_shape=jax.ShapeDtypeStruct((N_CTX, w), F32),
        grid=(BATCH,),
        in_specs=[pl.BlockSpec((SEQ, w), lambda b: (b, 0)),
                  pl.BlockSpec((SEQ, w), lambda b: (b, 1)),
                  pl.BlockSpec((SEQ, w), lambda b: (b, 2))],
        out_specs=pl.BlockSpec((SEQ, w), lambda b: (b, 0)),
        compiler_params=_params("parallel"),
        name="na_ctx",
    )(z_na, z_na, z_na)


def _na_row_lo(qr):
    return min(max(qr - NA_WIN_H // 2, 0), _LAT_ROWS - _NA_KH)


def _na_lat_kernel(rpb_ref, q_ref, k_ref, v_ref, kc_ref, vc_ref, o_ref, bias_scr):
    scale = HEAD_DIM ** -0.5

    @pl.when(pl.program_id(0) == 0)
    def _():
        qc = lax.broadcasted_iota(jnp.int32, (GRID_W, 2 * GRID_W), 0)
        lane = lax.broadcasted_iota(jnp.int32, (GRID_W, 2 * GRID_W), 1)
        kc = lane & (GRID_W - 1)
        col_lo = jnp.clip(qc - NA_WIN_W // 2, 0, GRID_W - NA_WIN_W)
        rel = jnp.where((kc >= col_lo) & (kc < col_lo + NA_WIN_W), kc - qc + NA_WIN_W - 1, -1)
        second = lax.broadcasted_iota(jnp.int32, (1, 2 * GRID_W), 1) >= GRID_W

        def build(idx, carry):
            h = idx // _NA_PAIRS
            row = h * (2 * NA_WIN_H - 1) + (idx - h * _NA_PAIRS)
            tile = jnp.full((GRID_W, 2 * GRID_W), NEG_INF, F32)
            for j in range(2 * NA_WIN_W - 1):
                val = jnp.where(second, rpb_ref[row + 1, j], rpb_ref[row, j])
                tile = jnp.where(rel == j, val, tile)
            bias_scr[idx] = tile
            return carry

        lax.fori_loop(0, NA_HEADS * _NA_PAIRS, build, 0)

    kc = kc_ref[...].astype(BF16)
    vc = vc_ref[...].astype(BF16)
    for qr in range(_LAT_ROWS):
        lo = _na_row_lo(qr)
        dr0 = lo - qr + NA_WIN_H - 1
        q_st = _stack_heads(q_ref[qr * GRID_W:(qr + 1) * GRID_W, :], NA_HEADS, HEAD_DIM).astype(BF16)
        kl = k_ref[lo * GRID_W:lo * GRID_W + _NA_LOCAL, :].astype(BF16)
        vl = v_ref[lo * GRID_W:lo * GRID_W + _NA_LOCAL, :].astype(BF16)
        bias = jnp.concatenate(
            [jnp.concatenate([bias_scr[h * _NA_PAIRS + dr0 + 2 * i] for i in range(_NA_KH // 2)], axis=1)
             for h in range(NA_HEADS)], axis=0)
        s_loc = lax.dot_general(q_st, kl, _NT, preferred_element_type=F32) * scale + bias
        s_ctx = lax.dot_general(q_st, kc, _NT, preferred_element_type=F32) * scale
        p_loc, p_ctx, l = _softmax2(s_loc, s_ctx)
        r = (jnp.dot(p_loc.astype(BF16), vl, preferred_element_type=F32)
             + jnp.dot(p_ctx.astype(BF16), vc, preferred_element_type=F32)) / l
        o_ref[qr * GRID_W:(qr + 1) * GRID_W, :] = _pick_heads(r, NA_HEADS, HEAD_DIM)


def _na_latent(rpb, z_na, kc, vc):
    w = NA_HEADS * HEAD_DIM
    lat0 = N_CTX // DEC_SEQ

    def col(c):
        return pl.BlockSpec((DEC_SEQ, w), lambda b: (lat0 + b, c))

    cache = pl.BlockSpec((None, PAST_LEN, w), lambda b: (b, 0, 0))
    return pl.pallas_call(
        _na_lat_kernel,
        out_shape=jax.ShapeDtypeStruct((N_LAT, w), F32),
        grid=(DEC_BATCH,),
        in_specs=[pl.BlockSpec(memory_space=pltpu.SMEM), col(0), col(1), col(2), cache, cache],
        out_specs=pl.BlockSpec((DEC_SEQ, w), lambda b: (b, 0)),
        scratch_shapes=[pltpu.VMEM((NA_HEADS * _NA_PAIRS, GRID_W, 2 * GRID_W), F32)],
        compiler_params=_params("arbitrary"),
        name="na_latent",
    )(rpb, z_na, z_na, z_na, kc, vc)


_MLA_W = MLA_HEADS * LANES
_ROPE_LO = NOPE_DIM
_ROPE_HALF = ROPE_DIM // 2


def _rope_tables(t_len):
    n_freq = ROPE_DIM // 4
    inv = 1.0 / (ROPE_BASE ** (np.arange(n_freq) / n_freq))
    pos = np.arange(t_len)
    ang = np.concatenate([(pos // GRID_W)[:, None] * inv, (pos % GRID_W)[:, None] * inv], axis=-1)
    cos, sin = np.cos(ang), np.sin(ang)
    c = np.ones((t_len, LANES))
    s1 = np.zeros((t_len, LANES))
    s2 = np.zeros((t_len, LANES))
    a, b, e = _ROPE_LO, _ROPE_LO + _ROPE_HALF, _ROPE_LO + ROPE_DIM
    c[:, a:b] = cos
    c[:, b:e] = cos
    s1[:, a:b] = -sin
    s2[:, b:e] = sin
    return tuple(jnp.asarray(t, F32) for t in (c, s1, s2))


def _apply_rope(x, c, s1, s2):
    up = pltpu.roll(x, LANES - _ROPE_HALF, 1)
    dn = pltpu.roll(x, _ROPE_HALF, 1)
    return x * c + up * s1 + dn * s2


def _mla_ctx_kernel(q_ref, kv_ref, kr_ref, o_ref):
    kr = kr_ref[...]
    v_all = kv_ref[:, _MLA_W:].astype(BF16)
    out = None
    for h in range(MLA_HEADS):
        qh = q_ref[:, h * LANES:(h + 1) * LANES].astype(BF16)
        kh = (kv_ref[:, h * LANES:(h + 1) * LANES] + kr).astype(BF16)
        s = lax.dot_general(qh, kh, _NT, preferred_element_type=F32) * MLA_SCALE
        p = jnp.exp(s - jnp.max(s, axis=-1, keepdims=True))
        l = jnp.sum(p, axis=-1, keepdims=True)
        r = jnp.dot(p.astype(BF16), v_all, preferred_element_type=F32) / l
        out = r if out is None else jnp.where(_head_mask(MLA_HEADS * V_DIM, h, V_DIM), r, out)
    o_ref[...] = out


def _mla_ctx(q_m, kv_m, z_mla):
    wv = MLA_HEADS * V_DIM
    return pl.pallas_call(
        _mla_ctx_kernel,
        out_shape=jax.ShapeDtypeStruct((N_CTX, wv), F32),
        grid=(BATCH,),
        in_specs=[pl.BlockSpec((SEQ, _MLA_W), lambda b: (b, 0)),
                  pl.BlockSpec((SEQ, _MLA_W + wv), lambda b: (b, 0)),
                  pl.BlockSpec((SEQ, LANES), lambda b: (b, 3))],
        out_specs=pl.BlockSpec((SEQ, wv), lambda b: (b, 0)),
        compiler_params=_params("parallel"),
        name="mla_ctx",
    )(q_m, kv_m, z_mla)


_MLA_BQ = 256


def _mla_lat_kernel(q_ref, kv_ref, kr_ref, kvc_ref, krc_ref, cq_ref, s1q_ref, s2q_ref, ck_ref, s1k_ref, s2k_ref,
                    o_ref):
    kr = _apply_rope(kr_ref[...], ck_ref[...], s1k_ref[...], s2k_ref[...])
    krc = krc_ref[...]
    v_lat = kv_ref[:, _MLA_W:].astype(BF16)
    v_ctx = kvc_ref[:, _MLA_W:].astype(BF16)
    out = None
    for h in range(MLA_HEADS):
        blk = slice(h * LANES, (h + 1) * LANES)
        qh = _apply_rope(q_ref[:, blk], cq_ref[...], s1q_ref[...], s2q_ref[...]).astype(BF16)
        kl = (kv_ref[:, blk] + kr).astype(BF16)
        kc = (kvc_ref[:, blk] + krc).astype(BF16)
        s_l = lax.dot_general(qh, kl, _NT, preferred_element_type=F32) * MLA_SCALE
        s_c = lax.dot_general(qh, kc, _NT, preferred_element_type=F32) * MLA_SCALE
        p_l, p_c, l = _softmax2(s_l, s_c)
        r = (jnp.dot(p_l.astype(BF16), v_lat, preferred_element_type=F32)
             + jnp.dot(p_c.astype(BF16), v_ctx, preferred_element_type=F32)) / l
        out = r if out is None else jnp.where(_head_mask(MLA_HEADS * V_DIM, h, V_DIM), r, out)
    o_ref[...] = out


def _mla_latent(q_m, kv_m, z_mla, kv_cache, kr_cache, rope):
    wv = MLA_HEADS * V_DIM
    nq = DEC_SEQ // _MLA_BQ
    q0 = N_CTX // _MLA_BQ
    lat0 = N_CTX // DEC_SEQ
    c, s1, s2 = rope
    tq = pl.BlockSpec((_MLA_BQ, LANES), lambda b, j: (j, 0))
    tk = pl.BlockSpec((DEC_SEQ, LANES), lambda b, j: (0, 0))
    return pl.pallas_call(
        _mla_lat_kernel,
        out_shape=jax.ShapeDtypeStruct((N_LAT, wv), F32),
        grid=(DEC_BATCH, nq),
        in_specs=[pl.BlockSpec((_MLA_BQ, _MLA_W), lambda b, j: (q0 + b * nq + j, 0)),
                  pl.BlockSpec((DEC_SEQ, _MLA_W + wv), lambda b, j: (lat0 + b, 0)),
                  pl.BlockSpec((DEC_SEQ, LANES), lambda b, j: (lat0 + b, 3)),
                  pl.BlockSpec((PAST_LEN, _MLA_W + wv), lambda b, j: (b, 0)),
                  pl.BlockSpec((PAST_LEN, LANES), lambda b, j: (b, 0)),
                  tq, tq, tq, tk, tk, tk],
        out_specs=pl.BlockSpec((_MLA_BQ, wv), lambda b, j: (b * nq + j, 0)),
        compiler_params=_params("parallel", "arbitrary"),
        name="mla_lat",
    )(q_m, kv_m, z_mla, kv_cache, kr_cache, c, s1, s2, c, s1, s2)


def _head_sum(x, ones_ref):
    return jnp.dot(x, ones_ref[...], preferred_element_type=F32, precision=lax.Precision.HIGHEST)


def _rw_pre_kernel(r_ref, k_ref, v_ref, wf_ref, wb_ref, af_ref, ab_ref, kkw_ref, ka_ref, rk_ref, ones_ref,
                   dwf_ref, dwb_ref, kdf_ref, kdb_ref, bf_ref, bb_ref, kk_ref, bonus_ref):
    k = k_ref[...]
    kk = k * kkw_ref[...]
    nrm = jnp.sqrt(_head_sum(kk * kk, ones_ref))
    kk = kk / jnp.maximum(nrm, 1e-12)
    kk_ref[...] = kk
    kd_sum = None
    for wl_ref, al_ref, dw_ref, kd_ref, b_ref in ((wf_ref, af_ref, dwf_ref, kdf_ref, bf_ref),
                                                  (wb_ref, ab_ref, dwb_ref, kdb_ref, bb_ref)):
        dw_ref[...] = jnp.exp(-float(np.exp(-0.5)) * _sigmoid(wl_ref[...]))
        a = _sigmoid(al_ref[...])
        kd = k * (1.0 + (a - 1.0) * ka_ref[...])
        kd_ref[...] = kd
        b_ref[...] = kk * a
        kd_sum = kd if kd_sum is None else kd_sum + kd
    bonus_ref[...] = _head_sum(r_ref[...] * rk_ref[...] * kd_sum, ones_ref) * v_ref[...]


def _rw_pre(z_rw, wa, kk_w, ka, rk, ones):
    tm = 512
    n = z_rw.shape[0]

    def col(c):
        return pl.BlockSpec((tm, RW_DIM), lambda i: (i, c))

    vec = pl.BlockSpec((1, RW_DIM), lambda i: (0, 0))
    out = jax.ShapeDtypeStruct((n, RW_DIM), F32)
    return pl.pallas_call(
        _rw_pre_kernel,
        out_shape=(out,) * 8,
        grid=(n // tm,),
        in_specs=[col(0), col(1), col(2), col(0), col(1), col(2), col(3), vec, vec, vec,
                  pl.BlockSpec((RW_DIM, RW_DIM), lambda i: (0, 0))],
        out_specs=(pl.BlockSpec((tm, RW_DIM), lambda i: (i, 0)),) * 8,
        compiler_params=_params("parallel"),
        name="rwkv_pre",
    )(z_rw, z_rw, z_rw, wa, wa, wa, wa, kk_w.reshape(1, RW_DIM), ka.reshape(1, RW_DIM),
      rk.reshape(1, RW_DIM), ones)


_SCAN_OPS = 6


def _half_swap(x):
    return pltpu.roll(x, LANES // 2, 1)


SCAN_NB = LANES // (4 * RW_HEADS)


def _rw_scan_kernel(*refs, n_src, n_keep, period, reverse):
    nb = SCAN_NB
    tc = SCAN_TC
    ins = [refs[s * _SCAN_OPS:(s + 1) * _SCAN_OPS] for s in range(n_src)]
    s0_ref = refs[n_src * _SCAN_OPS]
    y_refs = refs[n_src * _SCAN_OPS + 1:n_src * _SCAN_OPS + 1 + n_src]
    sf_keep_ref, sf_reset_ref, ops_a, ops_b, y_a, y_b, s_scr = refs[n_src * _SCAN_OPS + 1 + n_src:]
    c = pl.program_id(0)
    keep_lanes = n_keep * RW_HEADS * nb
    left = lax.broadcasted_iota(jnp.int32, (nb, LANES), 1) < LANES // 2

    @pl.when(c == 0)
    def _():
        s_scr[...] = s0_ref[...]
        y_b[...] = jnp.zeros_like(y_b)

    @pl.when((c > 0) & (c % period == 0))
    def _():
        keep = lax.broadcasted_iota(jnp.int32, (1, 1, LANES), 2) < keep_lanes
        s_scr[...] = jnp.where(keep, s_scr[...], 0.0)

    def times(q, s):
        return (tc - 1 - 2 * q, tc - 2 - 2 * q) if reverse[s] else (2 * q, 2 * q + 1)

    def relayout_in(q, ops):
        for o in range(_SCAN_OPS):
            rows = []
            for s in range(n_src):
                t0, t1 = times(q, s)
                a0 = ins[s][o][:, t0, :]
                a1 = ins[s][o][:, t1, :]
                for h in range(RW_HEADS):
                    j = h // 2
                    p0 = a0[:, LANES * j:LANES * (j + 1)]
                    p1 = a1[:, LANES * j:LANES * (j + 1)]
                    if h % 2 == 0:
                        rows.append(jnp.where(left, p0, _half_swap(p1)))
                    else:
                        rows.append(jnp.where(left, _half_swap(p0), p1))
            tile = jnp.concatenate(rows, axis=0).T
            ops[o, 0] = tile[0:HEAD_DIM]
            ops[o, 1] = tile[HEAD_DIM:]

    def steps(ops, y):
        for vi in range(HEAD_DIM):
            s = s_scr[vi]
            for t in range(2):
                sa = jnp.sum(s * ops[1, t], axis=0, keepdims=True)
                s = s * ops[0, t] - sa * ops[2, t] + ops[5, t, pl.ds(vi, 1), :] * ops[3, t]
                y[t, pl.ds(vi, 1), :] = jnp.sum(s * ops[4, t], axis=0, keepdims=True)
            s_scr[vi] = s

    def relayout_out(q, y):
        tile = jnp.concatenate([y[0], y[1]], axis=0).T
        for s in range(n_src):
            t0, t1 = times(q, s)
            for j in range(RW_HEADS // 2):
                r0 = (s * RW_HEADS + 2 * j) * nb
                even = tile[r0:r0 + nb]
                odd = tile[r0 + nb:r0 + 2 * nb]
                y_refs[s][t0, :, LANES * j:LANES * (j + 1)] = jnp.where(left, even, _half_swap(odd))
                y_refs[s][t1, :, LANES * j:LANES * (j + 1)] = jnp.where(left, _half_swap(even), odd)

    n_pairs = tc // 2
    relayout_in(0, ops_a)

    def body(i, carry):
        relayout_in(2 * i + 1, ops_b)
        steps(ops_a, y_a)
        relayout_out(jnp.maximum(2 * i - 1, 0), y_b)
        relayout_in(jnp.minimum(2 * i + 2, n_pairs - 1), ops_a)
        steps(ops_b, y_b)
        relayout_out(2 * i, y_a)
        return carry

    lax.fori_loop(0, n_pairs // 2, body, 0)
    relayout_out(n_pairs - 1, y_b)

    @pl.when(c % period == period - 1)
    def _():
        sf_reset_ref[...] = s_scr[...]

    @pl.when(c == pl.num_programs(0) - 1)
    def _():
        sf_keep_ref[...] = s_scr[...]


def _rw_scan(keep_srcs, reset_srcs, s0, t_keep, t_reset, reverse):
    tc, nb = SCAN_TC, SCAN_NB
    n_keep = len(keep_srcs)
    n_src = n_keep + len(reset_srcs)
    nc = t_keep // tc
    period = t_reset // tc
    n_phase = nc // period

    def idx(s, c):
        if s < n_keep:
            return 0, (nc - 1 - c if reverse[s] else c)
        cc = c % period
        return c // period, (period - 1 - cc if reverse[s] else cc)

    in_specs, args = [], []
    for s, src in enumerate(list(keep_srcs) + list(reset_srcs)):
        for arr, cb in src:
            in_specs.append(pl.BlockSpec((nb, tc, RW_DIM), lambda c, s=s, cb=cb: (*idx(s, c), cb)))
            args.append(arr)
    st = pl.BlockSpec((HEAD_DIM, HEAD_DIM, LANES), lambda c: (0, 0, 0))
    y_specs = tuple(pl.BlockSpec((tc, nb, RW_DIM), lambda c, s=s: (idx(s, c)[1], idx(s, c)[0], 0))
                    for s in range(n_src))
    y_shapes = tuple(jax.ShapeDtypeStruct((t_keep, nb, RW_DIM) if s < n_keep else (t_reset, nb * n_phase, RW_DIM), F32)
                     for s in range(n_src))
    pair = (_SCAN_OPS, 2, HEAD_DIM, LANES)
    out = pl.pallas_call(
        functools.partial(_rw_scan_kernel, n_src=n_src, n_keep=n_keep, period=period, reverse=tuple(reverse)),
        out_shape=y_shapes + (jax.ShapeDtypeStruct((HEAD_DIM, HEAD_DIM, LANES), F32),
                              jax.ShapeDtypeStruct((n_phase, HEAD_DIM, HEAD_DIM, LANES), F32)),
        grid=(nc,),
        in_specs=in_specs + [st],
        out_specs=y_specs + (st, pl.BlockSpec((None, HEAD_DIM, HEAD_DIM, LANES), lambda c: (c // period, 0, 0, 0))),
        scratch_shapes=[pltpu.VMEM(pair, F32), pltpu.VMEM(pair, F32),
                        pltpu.VMEM((2, HEAD_DIM, LANES), F32), pltpu.VMEM((2, HEAD_DIM, LANES), F32),
                        pltpu.VMEM((HEAD_DIM, HEAD_DIM, LANES), F32)],
        compiler_params=_params("arbitrary"),
        name="rwkv_scan",
    )(*args, s0)
    return out[:n_src], out[n_src], out[n_src + 1]


def _rw_post_kernel(yf_ref, yb_ref, bonus_ref, g_ref, gw_ref, gb_ref, ones_ref, o_ref, yn_scr, *, nb):
    tt = SCAN_TC
    inv = 1.0 / HEAD_DIM
    y = (yf_ref[...] + yb_ref[...]).reshape(tt * nb, RW_DIM)
    mu = _head_sum(y, ones_ref) * inv
    d = y - mu
    var = _head_sum(d * d, ones_ref) * inv
    yn = d * lax.rsqrt(var + GN_EPS) * gw_ref[...] + gb_ref[...]
    yn_scr[...] = yn.reshape(tt, nb, RW_DIM)

    def body(b, carry):
        o_ref[b] = (yn_scr[:, b, :] + bonus_ref[b]) * g_ref[b]
        return carry

    lax.fori_loop(0, nb, body, 0)


def _rw_post(y_f, y_b, bonus, g, gn_w, gn_b, ones, nb, t):
    tt = SCAN_TC
    tok = pl.BlockSpec((nb, tt, RW_DIM), lambda c: (0, c, 0))
    tmaj = pl.BlockSpec((tt, nb, RW_DIM), lambda c: (c, 0, 0))
    vec = pl.BlockSpec((1, RW_DIM), lambda c: (0, 0))
    out = pl.pallas_call(
        functools.partial(_rw_post_kernel, nb=nb),
        out_shape=jax.ShapeDtypeStruct((nb, t, RW_DIM), F32),
        grid=(t // tt,),
        in_specs=[tmaj, tmaj, tok, tok, vec, vec, pl.BlockSpec((RW_DIM, RW_DIM), lambda c: (0, 0))],
        out_specs=tok,
        scratch_shapes=[pltpu.VMEM((tt, nb, RW_DIM), F32)],
        compiler_params=_params("parallel"),
        name="rwkv_post",
    )(y_f, y_b, bonus.reshape(nb, t, RW_DIM), g.reshape(nb, t, RW_DIM), gn_w.reshape(1, RW_DIM),
      gn_b.reshape(1, RW_DIM), ones)
    return out.reshape(nb * t, RW_DIM)


def _scan_state(states):
    z = jnp.concatenate([s.transpose(2, 3, 1, 0).reshape(HEAD_DIM, HEAD_DIM, -1) for s in states], axis=-1)
    return jnp.pad(z, ((0, 0), (0, 0), (0, LANES - z.shape[-1])))


def _unscan_reset_state(sf, src):
    n = RW_HEADS * SCAN_NB
    z = sf[:, :, :, src * n:(src + 1) * n].reshape(-1, HEAD_DIM, HEAD_DIM, RW_HEADS, SCAN_NB)
    return z.transpose(0, 4, 3, 1, 2).reshape(-1, RW_HEADS, HEAD_DIM, HEAD_DIM)


def _rwkv_operands(h, row0, nb, t, w_rw, lora_w, lora_b, g2, lp, ones):
    z_rw = _mm(h, w_rw, a_rows=(row0, nb * t), name="proj_rw")
    wa = _mm(z_rw, lora_w, bias=lora_b, act=_lora_act, a_cols=(256, 3), name="rw_lora")
    g_rw = _mm(z_rw, g2, act=_sigmoid, a_cols=(GATE_LORA, 8), name="rw_gate")
    dw_f, dw_b, kd_f, kd_b, b_f, b_b, kk, bonus = (
        a.reshape(nb, t, RW_DIM) for a in _rw_pre(z_rw, wa, lp["kk"], lp["ka"], lp["rk"], ones))
    z3 = z_rw.reshape(nb, t, -1)
    fwd = ((dw_f, 0), (kk, 0), (b_f, 0), (kd_f, 0), (z3, 0), (z3, 2))
    bwd = ((dw_b, 0), (kk, 0), (b_b, 0), (kd_b, 0), (z3, 0), (z3, 2))
    return fwd, bwd, bonus, g_rw


def _rwkv(h, w_rw, lora_w, lora_b, g2, lp, ones, s0_f, s0_b):
    assert DEC_BATCH == SCAN_NB and BATCH * SEQ == DEC_BATCH * DEC_SEQ
    args = (w_rw, lora_w, lora_b, g2, lp, ones)
    c_f, c_b, c_bonus, c_g = _rwkv_operands(h, 0, BATCH, SEQ, *args)
    l_f, l_b, l_bonus, l_g = _rwkv_operands(h, N_CTX, DEC_BATCH, DEC_SEQ, *args)
    (yl_f, yl_b, yc_f, yc_b), _, sf_ctx = _rw_scan([l_f, l_b], [c_f, c_b], _scan_state([s0_f, s0_b]),
                                                   DEC_SEQ, SEQ, [False, True, False, True])
    post = (lp["gn_w"], lp["gn_b"], ones)
    rw_ctx = _rw_post(yc_f, yc_b, c_bonus, c_g, *post, BATCH, SEQ)
    rw_lat = _rw_post(yl_f, yl_b, l_bonus, l_g, *post, DEC_BATCH, DEC_SEQ)
    return rw_ctx, rw_lat, _unscan_reset_state(sf_ctx, 2), _unscan_reset_state(sf_ctx, 3)


_TM_MERGE = 512


def _merge_kernel(x_ref, gm_ref, h_ref, nac_ref, nal_ref, rwc_ref, rwl_ref, mlc_ref, mll_ref, wg_ref, wbr_ref, wo_ref,
                  o_ref):
    is_ctx = pl.program_id(0) < N_CTX // _TM_MERGE
    h = h_ref[...]
    branches = tuple(jnp.where(is_ctx, c_ref[...], l_ref[...])
                     for c_ref, l_ref in ((nac_ref, nal_ref), (rwc_ref, rwl_ref), (mlc_ref, mll_ref)))
    m = None
    for i, o_b in enumerate(branches):
        gate = jnp.dot(h, wg_ref[:, i * D_MODEL:(i + 1) * D_MODEL], preferred_element_type=F32)
        br = jnp.dot(o_b.astype(BF16), wbr_ref[i], preferred_element_type=F32)
        t = _sigmoid(gate) * br
        m = t if m is None else m + t
    mix = jnp.dot(m.astype(BF16), wo_ref[...], preferred_element_type=F32)
    o_ref[...] = x_ref[...] + gm_ref[...] * mix


def _merge(x, mod_l, h, na_ctx, na_lat, rw_ctx, rw_lat, mla_ctx, mla_lat, w_gate, w_br, w_o):
    tm = _TM_MERGE
    n_ctx = N_CTX // tm
    row = pl.BlockSpec((tm, D_MODEL), lambda i: (i, 0))
    br_ctx = pl.BlockSpec((tm, BRANCH_DIM), lambda i: (jnp.minimum(i, n_ctx - 1), 0))
    br_lat = pl.BlockSpec((tm, BRANCH_DIM), lambda i: (jnp.maximum(i - n_ctx, 0), 0))
    return pl.pallas_call(
        _merge_kernel,
        out_shape=jax.ShapeDtypeStruct((N_TOK, D_MODEL), F32),
        grid=(N_TOK // tm,),
        in_specs=[row, pl.BlockSpec((None, 1, D_MODEL), lambda i: (_cond_row(i, tm), 0, 5)),
                  row, br_ctx, br_lat, br_ctx, br_lat, br_ctx, br_lat,
                  pl.BlockSpec((D_MODEL, N_BRANCH * D_MODEL), lambda i: (0, 0)),
                  pl.BlockSpec((N_BRANCH, BRANCH_DIM, D_MODEL), lambda i: (0, 0, 0)),
                  pl.BlockSpec((D_MODEL, D_MODEL), lambda i: (0, 0))],
        out_specs=row,
        compiler_params=_params("arbitrary"),
        name="merge",
    )(x, mod_l, h, na_ctx, na_lat, rw_ctx, rw_lat, mla_ctx, mla_lat, w_gate, w_br, w_o)


def _lora_act(a):
    col = lax.broadcasted_iota(jnp.int32, a.shape, 1)
    return jnp.where(col < 2 * DECAY_LORA, jnp.tanh(a), a)


def _block_diag(blocks):
    rows = sum(b.shape[0] for b in blocks)
    cols = sum(b.shape[1] for b in blocks)
    out = jnp.zeros((rows, cols), blocks[0].dtype)
    r = c = 0
    for b in blocks:
        out = lax.dynamic_update_slice(out, b, (r, c))
        r += b.shape[0]
        c += b.shape[1]
    return out


def _pad_cols(w, left, total):
    return jnp.pad(w, ((0, 0), (left, total - left - w.shape[1])))


def _mla_uq_layout(w_uq):
    per = NOPE_DIM + ROPE_DIM
    return jnp.concatenate([_pad_cols(w_uq[:, h * per:(h + 1) * per], 0, LANES) for h in range(MLA_HEADS)], axis=1)


def _mla_ukv_layout(w_ukv):
    per = NOPE_DIM + V_DIM
    k = [_pad_cols(w_ukv[:, h * per:h * per + NOPE_DIM], 0, LANES) for h in range(MLA_HEADS)]
    v = [w_ukv[:, h * per + NOPE_DIM:(h + 1) * per] for h in range(MLA_HEADS)]
    return jnp.concatenate(k + v, axis=1)


def _heads_on_lanes(cache):
    b, h, t, d = cache.shape
    return cache.transpose(0, 2, 1, 3).reshape(b, t, h * d)


def _heads_major(z, nb, t, nh):
    return z.reshape(nb, t, nh, -1).transpose(0, 2, 1, 3)


def kernel(x_prompt, x_sample, c, cache_na_k, cache_na_v, cache_mla_ckv, cache_mla_krope, state_rwkv_fwd, state_rwkv_bwd, c_ctx, ada_w, ada_b, norm_g, ffn_wg, ffn_wu, ffn_wd, w_in, na_rpb, rw_w0, rw_w2, rw_a0, rw_a2, rw_g2, rw_kk, rw_ka, rw_rk, rw_gn_w, rw_gn_b, mla_qn_g, mla_kvn_g, mla_w_uq, mla_w_ukv, w_br, w_o, final_g):
    x = jnp.concatenate([x_prompt.reshape(N_CTX, D_MODEL), x_sample.reshape(N_LAT, D_MODEL)], axis=0)
    cond = jnp.concatenate([c_ctx[None, :], c, jnp.zeros((N_COND - 1 - DEC_BATCH, D_MODEL), F32)], axis=0)
    mod = _modulation(cond, ada_w, ada_b)

    ones = jnp.asarray(np.kron(np.eye(RW_HEADS), np.ones((HEAD_DIM, HEAD_DIM))), F32)
    rope = _rope_tables(DEC_SEQ)
    zero_state = jnp.zeros((BATCH, RW_HEADS, HEAD_DIM, HEAD_DIM), F32)

    col_rw = 3 * NA_HEADS * HEAD_DIM
    col_mla = col_rw + 3 * RW_DIM + 2 * DECAY_LORA + 2 * ICLR_LORA + GATE_LORA
    col_kr = col_mla + Q_LORA + KV_LORA
    col_gate = col_kr + ROPE_DIM
    kr_lane0 = Q_LORA + KV_LORA + _ROPE_LO

    caches = []
    y = None
    for l in range(DEPTH):
        mod_l = mod[l]
        wi = w_in[l].astype(BF16)
        w_na = wi[:, :col_rw]
        w_rw = wi[:, col_rw:col_mla]
        w_mla = jnp.concatenate([wi[:, col_mla:col_kr], _pad_cols(wi[:, col_kr:col_gate], _ROPE_LO, LANES)], axis=1)
        w_gate = wi[:, col_gate:]

        x, h = _ffn(x, norm_g[l, 0], mod_l, 0, ffn_wg[l, 0].astype(BF16), ffn_wu[l, 0].astype(BF16),
                    ffn_wd[l, 0].astype(BF16), post="mod", post_g=norm_g[l, 1], post_chunk=3)

        z_na = _mm(h, w_na, name="proj_na")
        na_ctx = _na_ctx(z_na)
        na_lat = _na_latent(na_rpb[l].reshape(NA_HEADS * (2 * NA_WIN_H - 1), 2 * NA_WIN_W - 1), z_na,
                            _heads_on_lanes(cache_na_k[:, l]), _heads_on_lanes(cache_na_v[:, l]))

        lora_w = _block_diag([rw_w2[l, 0], rw_w2[l, 1], rw_a2[l, 0], rw_a2[l, 1]]).astype(BF16)
        lora_b = jnp.concatenate([rw_w0[l, 0], rw_w0[l, 1], rw_a0[l, 0], rw_a0[l, 1]])
        lp = {"kk": rw_kk[l], "ka": rw_ka[l], "rk": rw_rk[l].reshape(RW_DIM), "gn_w": rw_gn_w[l], "gn_b": rw_gn_b[l]}
        rw_ctx, rw_lat, sc_f, sc_b = _rwkv(h, w_rw, lora_w, lora_b, rw_g2[l].astype(BF16), lp, ones,
                                           state_rwkv_fwd[:, l], state_rwkv_bwd[:, l])

        z_mla = _mm(h, w_mla, name="proj_mla")
        cq = _norm(z_mla, mla_qn_g[l], BF16, (Q_LORA, 0), "mla_qnorm")
        ckv = _norm(z_mla, mla_kvn_g[l], F32, (KV_LORA, Q_LORA // KV_LORA), "mla_kvnorm")
        q_m = _mm(cq, _mla_uq_layout(mla_w_uq[l]).astype(BF16), name="mla_uq")
        w_ukv = _mla_ukv_layout(mla_w_ukv[l]).astype(BF16)
        kv_m = _mm(ckv, w_ukv, name="mla_ukv")
        kv_cache = _mm(cache_mla_ckv[:, l].reshape(DEC_BATCH * PAST_LEN, KV_LORA), w_ukv, name="mla_ukv_cache")
        kr_cache = _pad_cols(cache_mla_krope[:, l].reshape(DEC_BATCH * PAST_LEN, ROPE_DIM), _ROPE_LO, LANES)
        mla_ctx = _mla_ctx(q_m, kv_m, z_mla)
        mla_lat = _mla_latent(q_m, kv_m, z_mla, kv_cache, kr_cache, rope)

        x = _merge(x, mod_l, h, na_ctx, na_lat, rw_ctx, rw_lat, mla_ctx, mla_lat, w_gate,
                   w_br[l].astype(BF16), w_o[l].astype(BF16))

        ffn2 = (x, norm_g[l, 2], mod_l, 6, ffn_wg[l, 1].astype(BF16), ffn_wu[l, 1].astype(BF16),
                ffn_wd[l, 1].astype(BF16))
        if l == DEPTH - 1:
            y = _ffn(*ffn2, post="plain", post_g=final_g)
        else:
            x = _ffn(*ffn2)

        w = NA_HEADS * HEAD_DIM
        caches.append((_heads_major(z_na[:N_CTX, w:2 * w], BATCH, SEQ, NA_HEADS),
                       _heads_major(z_na[:N_CTX, 2 * w:3 * w], BATCH, SEQ, NA_HEADS),
                       ckv[:N_CTX].reshape(BATCH, SEQ, KV_LORA),
                       z_mla[:N_CTX, kr_lane0:kr_lane0 + ROPE_DIM].reshape(BATCH, SEQ, ROPE_DIM),
                       sc_f, sc_b))

    y_prompt = y[:N_CTX].reshape(BATCH, SEQ, D_MODEL)
    y_sample = y[N_CTX:].reshape(DEC_BATCH, DEC_SEQ, D_MODEL)
    outs = [jnp.stack([cl[i] for cl in caches], axis=1) for i in range(6)]
    return (y_prompt, y_sample, *outs)
```

```python
import functools

import numpy as np
import jax
import jax.numpy as jnp
from jax import lax
from jax.experimental import pallas as pl
from jax.experimental.pallas import tpu as pltpu

F32 = jnp.float32
BF16 = jnp.bfloat16

D_MODEL = 1024
BATCH = 32
SEQ = 256
DEPTH = 2
DEC_BATCH = 8
DEC_SEQ = 1024
PAST_LEN = 256
GRID_W = 64
HEAD_DIM = 64
NA_HEADS = 4
NA_WIN_H = 8
NA_WIN_W = 16
RW_HEADS = 4
RW_DIM = RW_HEADS * HEAD_DIM
DECAY_LORA = 64
ICLR_LORA = 64
GATE_LORA = 128
GN_EPS = 64e-5
MLA_HEADS = 4
Q_LORA = 256
KV_LORA = 128
NOPE_DIM = 64
ROPE_DIM = 32
V_DIM = 64
MLA_SCALE = (NOPE_DIM + ROPE_DIM) ** -0.5
ROPE_BASE = 10000.0
N_BRANCH = 3
BRANCH_DIM = 256
D_FF = 2816
N_MOD = 9
NORM_EPS = 1e-6
NEG_INF = -1e30

N_CTX = BATCH * SEQ
N_LAT = DEC_BATCH * DEC_SEQ
N_TOK = N_CTX + N_LAT
N_COND = 16

LANES = 128
VMEM_LIMIT = 48 * 1024 * 1024

TM = 1024
TF = 256
SCAN_TC = 32


def _cond_row(i, tm):
    return jnp.where(i * tm < N_CTX, 0, 1 + (i * tm - N_CTX) // DEC_SEQ)


def _sigmoid(x):
    return 1.0 / (1.0 + jnp.exp(-x))


def _params(*sem):
    return pltpu.CompilerParams(dimension_semantics=sem, vmem_limit_bytes=VMEM_LIMIT)


def _mm_kernel(a_ref, w_ref, o_ref):
    o_ref[...] = jnp.dot(a_ref[...].astype(BF16), w_ref[...], preferred_element_type=F32)


def _mm(a, w, name):
    m, k = a.shape
    n = w.shape[1]
    tm = min(m, TM)
    assert m % tm == 0 and w.shape[0] == k
    return pl.pallas_call(
        _mm_kernel,
        out_shape=jax.ShapeDtypeStruct((m, n), F32),
        grid=(m // tm,),
        in_specs=[pl.BlockSpec((tm, k), lambda i: (i, 0)), pl.BlockSpec((k, n), lambda i: (0, 0))],
        out_specs=pl.BlockSpec((tm, n), lambda i: (i, 0)),
        compiler_params=_params("parallel"),
        name=name,
    )(a, w)


def _mod_kernel(c_ref, w_ref, b_ref, o_ref):
    c = c_ref[...]
    a = (c * _sigmoid(c)).astype(BF16)
    o_ref[...] = jnp.dot(a, w_ref[...].astype(BF16), preferred_element_type=F32) + b_ref[...]


def _modulation(cond, ada_w, ada_b):
    n = N_MOD * D_MODEL
    tn = D_MODEL
    out = pl.pallas_call(
        _mod_kernel,
        out_shape=jax.ShapeDtypeStruct((DEPTH, N_COND, n), F32),
        grid=(DEPTH, n // tn),
        in_specs=[pl.BlockSpec((N_COND, D_MODEL), lambda l, j: (0, 0)),
                  pl.BlockSpec((None, D_MODEL, tn), lambda l, j: (l, 0, j)),
                  pl.BlockSpec((None, 1, tn), lambda l, j: (l, 0, j))],
        out_specs=pl.BlockSpec((None, N_COND, tn), lambda l, j: (l, 0, j)),
        compiler_params=_params("parallel", "arbitrary"),
        name="modulation",
    )(cond, ada_w, ada_b.reshape(DEPTH, 1, n))
    return out.reshape(DEPTH, N_COND, 1, n)


def _rms(x, g):
    return x * lax.rsqrt(jnp.mean(x * x, axis=-1, keepdims=True) + NORM_EPS) * g


_FFN_CTX_TILES = N_CTX // TM


def _ffn_kernel(*refs, post, split_in):
    n_x = 2 if split_in else 1
    x_refs = refs[:n_x]
    g_ref, sh_ref, sc_ref, gt_ref, wg_ref, wu_ref, wd_ref = refs[n_x:n_x + 7]
    rest = refs[n_x + 7:]
    if post == "mod":
        g2_ref, sh2_ref, sc2_ref, o_ref, h2_ref, h_scr, acc_scr = rest
    elif post == "plain":
        g2_ref, ya_ref, yb_ref, h_scr, acc_scr = rest
    else:
        o_ref, h_scr, acc_scr = rest
    f = pl.program_id(1)
    is_ctx = pl.program_id(0) < _FFN_CTX_TILES

    def load_x():
        if split_in:
            return jnp.where(is_ctx, x_refs[0][...], x_refs[1][...])
        return x_refs[0][...]

    @pl.when(f == 0)
    def _():
        y = _rms(load_x(), g_ref[...])
        h_scr[...] = (y * (1.0 + sc_ref[...]) + sh_ref[...]).astype(BF16)
        acc_scr[...] = jnp.zeros_like(acc_scr)

    h = h_scr[...]
    gg = jnp.dot(h, wg_ref[...], preferred_element_type=F32)
    uu = jnp.dot(h, wu_ref[...], preferred_element_type=F32)
    a = (gg * _sigmoid(gg)) * uu
    acc_scr[...] += jnp.dot(a.astype(BF16), wd_ref[...], preferred_element_type=F32)

    @pl.when(f == pl.num_programs(1) - 1)
    def _():
        xn = load_x() + 0.5 * gt_ref[...] * acc_scr[...]
        if post == "plain":
            y = _rms(xn, g2_ref[...])

            @pl.when(is_ctx)
            def _():
                ya_ref[...] = y

            @pl.when(jnp.logical_not(is_ctx))
            def _():
                yb_ref[...] = y
        else:
            o_ref[...] = xn
        if post == "mod":
            h2_ref[...] = (_rms(xn, g2_ref[...]) * (1.0 + sc2_ref[...]) + sh2_ref[...]).astype(BF16)


def _ffn(x, norm_g, mod_l, chunk0, wg, wu, wd, post=None, post_g=None, post_chunk=None):
    def mod_spec(c):
        return pl.BlockSpec((None, 1, D_MODEL), lambda i, f: (_cond_row(i, TM), 0, c))

    row = pl.BlockSpec((TM, D_MODEL), lambda i, f: (i, 0))
    row_ctx = pl.BlockSpec((TM, D_MODEL), lambda i, f: (jnp.minimum(i, _FFN_CTX_TILES - 1), 0))
    row_lat = pl.BlockSpec((TM, D_MODEL), lambda i, f: (jnp.maximum(i - _FFN_CTX_TILES, 0), 0))
    vec = pl.BlockSpec((1, D_MODEL), lambda i, f: (0, 0))
    split_in = isinstance(x, tuple)
    in_specs = ([row_ctx, row_lat] if split_in else [row]) + [
        vec, mod_spec(chunk0), mod_spec(chunk0 + 1), mod_spec(chunk0 + 2),
        pl.BlockSpec((D_MODEL, TF), lambda i, f: (0, f)),
        pl.BlockSpec((D_MODEL, TF), lambda i, f: (0, f)),
        pl.BlockSpec((TF, D_MODEL), lambda i, f: (f, 0))]
    args = (list(x) if split_in else [x]) + [norm_g.reshape(1, D_MODEL), mod_l, mod_l, mod_l, wg, wu, wd]
    xs = jax.ShapeDtypeStruct((N_TOK, D_MODEL), F32)
    if post == "mod":
        in_specs += [vec, mod_spec(post_chunk), mod_spec(post_chunk + 1)]
        args += [post_g.reshape(1, D_MODEL), mod_l, mod_l]
        out_shape, out_specs = (xs, jax.ShapeDtypeStruct((N_TOK, D_MODEL), BF16)), (row, row)
    elif post == "plain":
        in_specs += [vec]
        args += [post_g.reshape(1, D_MODEL)]
        out_shape = (jax.ShapeDtypeStruct((N_CTX, D_MODEL), F32), jax.ShapeDtypeStruct((N_LAT, D_MODEL), F32))
        out_specs = (row_ctx, row_lat)
    else:
        out_shape, out_specs = xs, row
    return pl.pallas_call(
        functools.partial(_ffn_kernel, post=post, split_in=split_in),
        out_shape=out_shape,
        grid=(N_TOK // TM, D_FF // TF),
        in_specs=in_specs,
        out_specs=out_specs,
        scratch_shapes=[pltpu.VMEM((TM, D_MODEL), BF16), pltpu.VMEM((TM, D_MODEL), F32)],
        compiler_params=_params("arbitrary", "arbitrary"),
        name="ffn",
    )(*args)


_NT = (((1,), (1,)), ((), ()))
_LAT_ROWS = DEC_SEQ // GRID_W
_NA_KH = min(NA_WIN_H, _LAT_ROWS)
_NA_LOCAL = _NA_KH * GRID_W
_NA_PAIRS = 2 * NA_WIN_H - 2


def _head_mask(width, h, group):
    lane = lax.broadcasted_iota(jnp.int32, (1, width), 1)
    return (lane >= h * group) & (lane < (h + 1) * group)


def _stack_heads(q, nh, group):
    return jnp.concatenate([jnp.where(_head_mask(q.shape[1], h, group), q, 0.0) for h in range(nh)], axis=0)


def _pick_heads(r, nh, group):
    t = r.shape[0] // nh
    out = r[0:t]
    for h in range(1, nh):
        out = jnp.where(_head_mask(r.shape[1], h, group), r[h * t:(h + 1) * t], out)
    return out


def _softmax2(s_a, s_b):
    m = jnp.maximum(jnp.max(s_a, axis=-1, keepdims=True), jnp.max(s_b, axis=-1, keepdims=True))
    p_a = jnp.exp(s_a - m)
    p_b = jnp.exp(s_b - m)
    return p_a, p_b, jnp.sum(p_a, axis=-1, keepdims=True) + jnp.sum(p_b, axis=-1, keepdims=True)


def _na_ctx_kernel(q_ref, k_ref, v_ref, o_ref):
    q_st = _stack_heads(q_ref[...] * HEAD_DIM ** -0.5, NA_HEADS, HEAD_DIM).astype(BF16)
    s = lax.dot_general(q_st, k_ref[...].astype(BF16), _NT, preferred_element_type=F32)
    p = jnp.exp(s - jnp.max(s, axis=-1, keepdims=True))
    l = jnp.sum(p, axis=-1, keepdims=True)
    r = jnp.dot(p.astype(BF16), v_ref[...].astype(BF16), preferred_element_type=F32) / l
    o_ref[...] = _pick_heads(r, NA_HEADS, HEAD_DIM)


def _na_ctx(z_na):
    w = NA_HEADS * HEAD_DIM
    return pl.pallas_call(
        _na_ctx_kernel,
        out_shape=jax.ShapeDtypeStruct((N_CTX, w), F32),
        grid=(BATCH,),
        in_specs=[pl.BlockSpec((SEQ, w), lambda b: (b, 0)),
                  pl.BlockSpec((SEQ, w), lambda b: (b, 1)),
                  pl.BlockSpec((SEQ, w), lambda b: (b, 2))],
        out_specs=pl.BlockSpec((SEQ, w), lambda b: (b, 0)),
        compiler_params=_params("parallel"),
        name="na_ctx",
    )(z_na, z_na, z_na)


def _na_row_lo(qr):
    return min(max(qr - NA_WIN_H // 2, 0), _LAT_ROWS - _NA_KH)


def _na_lat_kernel(rpb_ref, q_ref, k_ref, v_ref, kc_ref, vc_ref, o_ref, bias_scr):
    scale = HEAD_DIM ** -0.5

    @pl.when(pl.program_id(0) == 0)
    def _():
        qc = lax.broadcasted_iota(jnp.int32, (GRID_W, 2 * GRID_W), 0)
        lane = lax.broadcasted_iota(jnp.int32, (GRID_W, 2 * GRID_W), 1)
        kc = lane & (GRID_W - 1)
        col_lo = jnp.clip(qc - NA_WIN_W // 2, 0, GRID_W - NA_WIN_W)
        rel = jnp.where((kc >= col_lo) & (kc < col_lo + NA_WIN_W), kc - qc + NA_WIN_W - 1, -1)
        second = lax.broadcasted_iota(jnp.int32, (1, 2 * GRID_W), 1) >= GRID_W

        def build(idx, carry):
            h = idx // _NA_PAIRS
            row = h * (2 * NA_WIN_H - 1) + (idx - h * _NA_PAIRS)
            tile = jnp.full((GRID_W, 2 * GRID_W), NEG_INF, F32)
            for j in range(2 * NA_WIN_W - 1):
                val = jnp.where(second, rpb_ref[row + 1, j], rpb_ref[row, j])
                tile = jnp.where(rel == j, val, tile)
            bias_scr[idx] = tile
            return carry

        lax.fori_loop(0, NA_HEADS * _NA_PAIRS, build, 0)

    kc = kc_ref[...].astype(BF16)
    vc = vc_ref[...].astype(BF16)
    for qr in range(_LAT_ROWS):
        lo = _na_row_lo(qr)
        dr0 = lo - qr + NA_WIN_H - 1
        q_st = _stack_heads(q_ref[qr * GRID_W:(qr + 1) * GRID_W, :] * scale, NA_HEADS, HEAD_DIM).astype(BF16)
        kl = k_ref[lo * GRID_W:lo * GRID_W + _NA_LOCAL, :].astype(BF16)
        vl = v_ref[lo * GRID_W:lo * GRID_W + _NA_LOCAL, :].astype(BF16)
        bias = jnp.concatenate(
            [jnp.concatenate([bias_scr[h * _NA_PAIRS + dr0 + 2 * i] for i in range(_NA_KH // 2)], axis=1)
             for h in range(NA_HEADS)], axis=0)
        s_loc = lax.dot_general(q_st, kl, _NT, preferred_element_type=F32) + bias
        s_ctx = lax.dot_general(q_st, kc, _NT, preferred_element_type=F32)
        p_loc, p_ctx, l = _softmax2(s_loc, s_ctx)
        r = (jnp.dot(p_loc.astype(BF16), vl, preferred_element_type=F32)
             + jnp.dot(p_ctx.astype(BF16), vc, preferred_element_type=F32)) / l
        o_ref[qr * GRID_W:(qr + 1) * GRID_W, :] = _pick_heads(r, NA_HEADS, HEAD_DIM)


def _na_latent(rpb, z_na, kc, vc):
    w = NA_HEADS * HEAD_DIM
    lat0 = N_CTX // DEC_SEQ

    def col(c):
        return pl.BlockSpec((DEC_SEQ, w), lambda b: (lat0 + b, c))

    cache = pl.BlockSpec((None, PAST_LEN, w), lambda b: (b, 0, 0))
    return pl.pallas_call(
        _na_lat_kernel,
        out_shape=jax.ShapeDtypeStruct((N_LAT, w), F32),
        grid=(DEC_BATCH,),
        in_specs=[pl.BlockSpec(memory_space=pltpu.SMEM), col(0), col(1), col(2), cache, cache],
        out_specs=pl.BlockSpec((DEC_SEQ, w), lambda b: (b, 0)),
        scratch_shapes=[pltpu.VMEM((NA_HEADS * _NA_PAIRS, GRID_W, 2 * GRID_W), F32)],
        compiler_params=_params("arbitrary"),
        name="na_latent",
    )(rpb, z_na, z_na, z_na, kc, vc)


_MLA_W = MLA_HEADS * LANES
_ROPE_LO = NOPE_DIM
_ROPE_HALF = ROPE_DIM // 2


def _rope_tables(t_len):
    n_freq = ROPE_DIM // 4
    inv = 1.0 / (ROPE_BASE ** (np.arange(n_freq) / n_freq))
    pos = np.arange(t_len)
    ang = np.concatenate([(pos // GRID_W)[:, None] * inv, (pos % GRID_W)[:, None] * inv], axis=-1)
    cos, sin = np.cos(ang), np.sin(ang)
    c = np.ones((t_len, LANES))
    s1 = np.zeros((t_len, LANES))
    s2 = np.zeros((t_len, LANES))
    a, b, e = _ROPE_LO, _ROPE_LO + _ROPE_HALF, _ROPE_LO + ROPE_DIM
    c[:, a:b] = cos
    c[:, b:e] = cos
    s1[:, a:b] = -sin
    s2[:, b:e] = sin
    return tuple(jnp.asarray(t, F32) for t in (c, s1, s2))


def _apply_rope(x, c, s1, s2):
    up = pltpu.roll(x, LANES - _ROPE_HALF, 1)
    dn = pltpu.roll(x, _ROPE_HALF, 1)
    return x * c + up * s1 + dn * s2


def _mla_prep_kernel(h_ref, w_ref, qg_ref, kvg_ref, wuq_ref, wukv_ref, q_ref, kv_ref, ckv_ref, kr_ref):
    z = jnp.dot(h_ref[...], w_ref[...], preferred_element_type=F32)
    cq = _rms(z[:, :Q_LORA], qg_ref[...])
    q_ref[...] = jnp.dot(cq.astype(BF16), wuq_ref[...], preferred_element_type=F32)
    ckv = _rms(z[:, Q_LORA:Q_LORA + KV_LORA], kvg_ref[...])
    ckv_ref[...] = ckv
    kv_ref[...] = jnp.dot(ckv.astype(BF16), wukv_ref[...], preferred_element_type=F32)
    kr_ref[...] = z[:, Q_LORA + KV_LORA:]


def _mla_prep(h, w_mla, qn_g, kvn_g, w_uq, w_ukv):
    wv = MLA_HEADS * V_DIM
    row = lambda n: pl.BlockSpec((TM, n), lambda i: (i, 0))
    full = lambda a: pl.BlockSpec(a.shape, lambda i: (0, 0))
    qn_g = qn_g.reshape(1, Q_LORA)
    kvn_g = kvn_g.reshape(1, KV_LORA)
    return pl.pallas_call(
        _mla_prep_kernel,
        out_shape=(jax.ShapeDtypeStruct((N_TOK, _MLA_W), F32), jax.ShapeDtypeStruct((N_TOK, _MLA_W + wv), F32),
                   jax.ShapeDtypeStruct((N_TOK, KV_LORA), F32), jax.ShapeDtypeStruct((N_TOK, LANES), F32)),
        grid=(N_TOK // TM,),
        in_specs=[row(D_MODEL), full(w_mla), full(qn_g), full(kvn_g), full(w_uq), full(w_ukv)],
        out_specs=(row(_MLA_W), row(_MLA_W + wv), row(KV_LORA), row(LANES)),
        compiler_params=_params("parallel"),
        name="mla_prep",
    )(h, w_mla, qn_g, kvn_g, w_uq, w_ukv)


def _mla_ctx_kernel(q_ref, kv_ref, kr_ref, o_ref):
    kr = kr_ref[...]
    v_all = kv_ref[:, _MLA_W:].astype(BF16)
    out = None
    for h in range(MLA_HEADS):
        qh = (q_ref[:, h * LANES:(h + 1) * LANES] * MLA_SCALE).astype(BF16)
        kh = (kv_ref[:, h * LANES:(h + 1) * LANES] + kr).astype(BF16)
        s = lax.dot_general(qh, kh, _NT, preferred_element_type=F32)
        p = jnp.exp(s - jnp.max(s, axis=-1, keepdims=True))
        l = jnp.sum(p, axis=-1, keepdims=True)
        r = jnp.dot(p.astype(BF16), v_all, preferred_element_type=F32) / l
        out = r if out is None else jnp.where(_head_mask(MLA_HEADS * V_DIM, h, V_DIM), r, out)
    o_ref[...] = out


def _mla_ctx(q_m, kv_m, kr_m):
    wv = MLA_HEADS * V_DIM
    return pl.pallas_call(
        _mla_ctx_kernel,
        out_shape=jax.ShapeDtypeStruct((N_CTX, wv), F32),
        grid=(BATCH,),
        in_specs=[pl.BlockSpec((SEQ, _MLA_W), lambda b: (b, 0)),
                  pl.BlockSpec((SEQ, _MLA_W + wv), lambda b: (b, 0)),
                  pl.BlockSpec((SEQ, LANES), lambda b: (b, 0))],
        out_specs=pl.BlockSpec((SEQ, wv), lambda b: (b, 0)),
        compiler_params=_params("parallel"),
        name="mla_ctx",
    )(q_m, kv_m, kr_m)


_MLA_BQ = 256


def _mla_lat_kernel(q_ref, kv_ref, kr_ref, kvc_ref, krc_ref, cq_ref, s1q_ref, s2q_ref, ck_ref, s1k_ref, s2k_ref,
                    o_ref):
    kr = _apply_rope(kr_ref[...], ck_ref[...], s1k_ref[...], s2k_ref[...])
    krc = krc_ref[...]
    v_lat = kv_ref[:, _MLA_W:].astype(BF16)
    v_ctx = kvc_ref[:, _MLA_W:].astype(BF16)
    out = None
    for h in range(MLA_HEADS):
        blk = slice(h * LANES, (h + 1) * LANES)
        qh = (_apply_rope(q_ref[:, blk], cq_ref[...], s1q_ref[...], s2q_ref[...]) * MLA_SCALE).astype(BF16)
        kl = (kv_ref[:, blk] + kr).astype(BF16)
        kc = (kvc_ref[:, blk] + krc).astype(BF16)
        s_l = lax.dot_general(qh, kl, _NT, preferred_element_type=F32)
        s_c = lax.dot_general(qh, kc, _NT, preferred_element_type=F32)
        p_l, p_c, l = _softmax2(s_l, s_c)
        r = (jnp.dot(p_l.astype(BF16), v_lat, preferred_element_type=F32)
             + jnp.dot(p_c.astype(BF16), v_ctx, preferred_element_type=F32)) / l
        out = r if out is None else jnp.where(_head_mask(MLA_HEADS * V_DIM, h, V_DIM), r, out)
    o_ref[...] = out


def _mla_latent(q_m, kv_m, kr_m, kv_cache, kr_cache, rope):
    wv = MLA_HEADS * V_DIM
    nq = DEC_SEQ // _MLA_BQ
    q0 = N_CTX // _MLA_BQ
    lat0 = N_CTX // DEC_SEQ
    c, s1, s2 = rope
    tq = pl.BlockSpec((_MLA_BQ, LANES), lambda b, j: (j, 0))
    tk = pl.BlockSpec((DEC_SEQ, LANES), lambda b, j: (0, 0))
    return pl.pallas_call(
        _mla_lat_kernel,
        out_shape=jax.ShapeDtypeStruct((N_LAT, wv), F32),
        grid=(DEC_BATCH, nq),
        in_specs=[pl.BlockSpec((_MLA_BQ, _MLA_W), lambda b, j: (q0 + b * nq + j, 0)),
                  pl.BlockSpec((DEC_SEQ, _MLA_W + wv), lambda b, j: (lat0 + b, 0)),
                  pl.BlockSpec((DEC_SEQ, LANES), lambda b, j: (lat0 + b, 0)),
                  pl.BlockSpec((PAST_LEN, _MLA_W + wv), lambda b, j: (b, 0)),
                  pl.BlockSpec((PAST_LEN, LANES), lambda b, j: (b, 0)),
                  tq, tq, tq, tk, tk, tk],
        out_specs=pl.BlockSpec((_MLA_BQ, wv), lambda b, j: (b * nq + j, 0)),
        compiler_params=_params("parallel", "arbitrary"),
        name="mla_lat",
    )(q_m, kv_m, kr_m, kv_cache, kr_cache, c, s1, s2, c, s1, s2)


def _head_sum(x, ones_ref):
    ones = ones_ref[...]
    hi = x.astype(BF16)
    rest = x - hi.astype(F32)
    mid = rest.astype(BF16)
    lo = (rest - mid.astype(F32)).astype(BF16)
    return (jnp.dot(hi, ones, preferred_element_type=F32) + jnp.dot(mid, ones, preferred_element_type=F32)
            + jnp.dot(lo, ones, preferred_element_type=F32))


SCAN_NB = LANES // (4 * RW_HEADS)
_PREP_ROWS = 512
_PREP_TMAJ = 9


def _lora_act(a):
    col = lax.broadcasted_iota(jnp.int32, a.shape, 1)
    return jnp.where(col < 2 * DECAY_LORA, jnp.tanh(a), a)


def _rw_prep_kernel(h_ref, w_ref, lw_ref, lb_ref, g2_ref, kkw_ref, ka_ref, rk_ref, ones_ref, *rest, nbb, tt):
    tmaj_refs = rest[:_PREP_TMAJ]
    bonus_ref, g_ref, stage_scr = rest[_PREP_TMAJ:]
    z = jnp.dot(h_ref[...].reshape(nbb * tt, D_MODEL), w_ref[...], preferred_element_type=F32)
    r = z[:, 0:RW_DIM]
    k = z[:, RW_DIM:2 * RW_DIM]
    v = z[:, 2 * RW_DIM:3 * RW_DIM]
    lora_in = z[:, 3 * RW_DIM:4 * RW_DIM]
    gate_in = z[:, 4 * RW_DIM:4 * RW_DIM + GATE_LORA]
    wa = jnp.dot(_lora_act(lora_in).astype(BF16), lw_ref[...], preferred_element_type=F32) + lb_ref[...]
    g = jnp.dot(_sigmoid(gate_in).astype(BF16), g2_ref[...], preferred_element_type=F32)
    kk = k * kkw_ref[...]
    kk = kk / jnp.maximum(jnp.sqrt(_head_sum(kk * kk, ones_ref)), 1e-12)
    per_dir = []
    for d in range(2):
        decay = jnp.exp(-float(np.exp(-0.5)) * _sigmoid(wa[:, d * RW_DIM:(d + 1) * RW_DIM]))
        a = _sigmoid(wa[:, (2 + d) * RW_DIM:(3 + d) * RW_DIM])
        per_dir.append((decay, k * (1.0 + (a - 1.0) * ka_ref[...]), kk * a))
    bonus = _head_sum(r * rk_ref[...] * (per_dir[0][1] + per_dir[1][1]), ones_ref) * v
    bonus_ref[...] = bonus.reshape(nbb, tt, RW_DIM)
    g_ref[...] = g.reshape(nbb, tt, RW_DIM)
    tmaj = (per_dir[0][0], per_dir[1][0], per_dir[0][1], per_dir[1][1], per_dir[0][2], per_dir[1][2], kk, r, v)
    for i, val in enumerate(tmaj):
        stage_scr[i] = val.reshape(nbb, tt, RW_DIM)

    def to_time_major(t, carry):
        for i, ref in enumerate(tmaj_refs):
            ref[t] = stage_scr[i, :, t, :]
        return carry

    lax.fori_loop(0, tt, to_time_major, 0)


def _rw_prep(h_seq, seq0, nb, t, w_rw, lora_w, lora_b, g2, lp, ones):
    nbb = nb if nb * 16 <= _PREP_ROWS else SCAN_NB
    tt = _PREP_ROWS // nbb
    assert seq0 % nbb == 0 and nb % nbb == 0 and t % tt == 0 and tt % 16 == 0
    b0 = seq0 // nbb

    def full(shape):
        return pl.BlockSpec(shape, lambda i, c: (0,) * len(shape))

    tmaj = pl.BlockSpec((tt, nbb, RW_DIM), lambda i, c: (c, i, 0))
    tok = pl.BlockSpec((nbb, tt, RW_DIM), lambda i, c: (i, c, 0))
    return pl.pallas_call(
        functools.partial(_rw_prep_kernel, nbb=nbb, tt=tt),
        out_shape=(jax.ShapeDtypeStruct((t, nb, RW_DIM), F32),) * _PREP_TMAJ
        + (jax.ShapeDtypeStruct((nb, t, RW_DIM), F32),) * 2,
        grid=(nb // nbb, t // tt),
        in_specs=[pl.BlockSpec((nbb, tt, D_MODEL), lambda i, c: (b0 + i, c, 0)),
                  full(w_rw.shape), full(lora_w.shape), full((1, 4 * RW_DIM)), full(g2.shape),
                  full((1, RW_DIM)), full((1, RW_DIM)), full((1, RW_DIM)), full((RW_DIM, RW_DIM))],
        out_specs=(tmaj,) * _PREP_TMAJ + (tok, tok),
        scratch_shapes=[pltpu.VMEM((_PREP_TMAJ, nbb, tt, RW_DIM), F32)],
        compiler_params=_params("parallel", "arbitrary"),
        name="rwkv_prep",
    )(h_seq, w_rw, lora_w, lora_b.reshape(1, 4 * RW_DIM), g2, lp["kk"].reshape(1, RW_DIM),
      lp["ka"].reshape(1, RW_DIM), lp["rk"].reshape(1, RW_DIM), ones)


_SCAN_OPS = 6


def _half_swap(x):
    return pltpu.roll(x, LANES // 2, 1)


def _rw_scan_kernel(*refs, n_src, n_keep, period, reverse):
    nb = SCAN_NB
    tc = SCAN_TC
    ins = [refs[s * _SCAN_OPS:(s + 1) * _SCAN_OPS] for s in range(n_src)]
    s0_ref = refs[n_src * _SCAN_OPS]
    y_refs = refs[n_src * _SCAN_OPS + 1:n_src * _SCAN_OPS + 1 + n_src]
    sf_keep_ref, sf_reset_ref, ops_a, ops_b, y_a, y_b, s_scr = refs[n_src * _SCAN_OPS + 1 + n_src:]
    c = pl.program_id(0)
    keep_lanes = n_keep * RW_HEADS * nb
    left = lax.broadcasted_iota(jnp.int32, (nb, LANES), 1) < LANES // 2

    @pl.when(c == 0)
    def _():
        s_scr[...] = s0_ref[...]
        y_b[...] = jnp.zeros_like(y_b)

    @pl.when((c > 0) & (c % period == 0))
    def _():
        keep = lax.broadcasted_iota(jnp.int32, (1, 1, LANES), 2) < keep_lanes
        s_scr[...] = jnp.where(keep, s_scr[...], 0.0)

    def times(q, s):
        return (tc - 1 - 2 * q, tc - 2 - 2 * q) if reverse[s] else (2 * q, 2 * q + 1)

    def relayout_in(q, ops):
        for o in range(_SCAN_OPS):
            rows = []
            for s in range(n_src):
                t0, t1 = times(q, s)
                a0 = ins[s][o][t0]
                a1 = ins[s][o][t1]
                for h in range(RW_HEADS):
                    j = h // 2
                    p0 = a0[:, LANES * j:LANES * (j + 1)]
                    p1 = a1[:, LANES * j:LANES * (j + 1)]
                    if h % 2 == 0:
                        rows.append(jnp.where(left, p0, _half_swap(p1)))
                    else:
                        rows.append(jnp.where(left, _half_swap(p0), p1))
            tile = jnp.concatenate(rows, axis=0).T
            ops[o, 0] = tile[0:HEAD_DIM]
            ops[o, 1] = tile[HEAD_DIM:]

    def steps(ops, y):
        for vi in range(HEAD_DIM):
            s = s_scr[vi]
            for t in range(2):
                sa = jnp.sum(s * ops[1, t], axis=0, keepdims=True)
                s = s * ops[0, t] - sa * ops[2, t] + ops[5, t, pl.ds(vi, 1), :] * ops[3, t]
                y[t, pl.ds(vi, 1), :] = jnp.sum(s * ops[4, t], axis=0, keepdims=True)
            s_scr[vi] = s

    def relayout_out(q, y):
        tile = jnp.concatenate([y[0], y[1]], axis=0).T
        for s in range(n_src):
            t0, t1 = times(q, s)
            for j in range(RW_HEADS // 2):
                r0 = (s * RW_HEADS + 2 * j) * nb
                even = tile[r0:r0 + nb]
                odd = tile[r0 + nb:r0 + 2 * nb]
                y_refs[s][t0, :, LANES * j:LANES * (j + 1)] = jnp.where(left, even, _half_swap(odd))
                y_refs[s][t1, :, LANES * j:LANES * (j + 1)] = jnp.where(left, _half_swap(even), odd)

    n_pairs = tc // 2
    relayout_in(0, ops_a)

    def body(i, carry):
        relayout_in(2 * i + 1, ops_b)
        steps(ops_a, y_a)
        relayout_out(jnp.maximum(2 * i - 1, 0), y_b)
        relayout_in(jnp.minimum(2 * i + 2, n_pairs - 1), ops_a)
        steps(ops_b, y_b)
        relayout_out(2 * i, y_a)
        return carry

    lax.fori_loop(0, n_pairs // 2, body, 0)
    relayout_out(n_pairs - 1, y_b)

    @pl.when(c % period == period - 1)
    def _():
        sf_reset_ref[...] = s_scr[...]

    @pl.when(c == pl.num_programs(0) - 1)
    def _():
        sf_keep_ref[...] = s_scr[...]


def _rw_scan(keep_srcs, reset_srcs, s0, t_keep, t_reset, reverse):
    tc, nb = SCAN_TC, SCAN_NB
    n_keep = len(keep_srcs)
    n_src = n_keep + len(reset_srcs)
    nc = t_keep // tc
    period = t_reset // tc
    n_phase = nc // period

    def idx(s, c):
        if s < n_keep:
            return 0, (nc - 1 - c if reverse[s] else c)
        cc = c % period
        return c // period, (period - 1 - cc if reverse[s] else cc)

    y_specs = tuple(pl.BlockSpec((tc, nb, RW_DIM), lambda c, s=s: (idx(s, c)[1], idx(s, c)[0], 0))
                    for s in range(n_src))
    in_specs, args = [], []
    for s, src in enumerate(list(keep_srcs) + list(reset_srcs)):
        in_specs += [y_specs[s]] * _SCAN_OPS
        args += list(src)
    st = pl.BlockSpec((HEAD_DIM, HEAD_DIM, LANES), lambda c: (0, 0, 0))
    y_shapes = tuple(jax.ShapeDtypeStruct((t_keep, nb, RW_DIM) if s < n_keep else (t_reset, nb * n_phase, RW_DIM), F32)
                     for s in range(n_src))
    pair = (_SCAN_OPS, 2, HEAD_DIM, LANES)
    out = pl.pallas_call(
        functools.partial(_rw_scan_kernel, n_src=n_src, n_keep=n_keep, period=period, reverse=tuple(reverse)),
        out_shape=y_shapes + (jax.ShapeDtypeStruct((HEAD_DIM, HEAD_DIM, LANES), F32),
                              jax.ShapeDtypeStruct((n_phase, HEAD_DIM, HEAD_DIM, LANES), F32)),
        grid=(nc,),
        in_specs=in_specs + [st],
        out_specs=y_specs + (st, pl.BlockSpec((None, HEAD_DIM, HEAD_DIM, LANES), lambda c: (c // period, 0, 0, 0))),
        scratch_shapes=[pltpu.VMEM(pair, F32), pltpu.VMEM(pair, F32),
                        pltpu.VMEM((2, HEAD_DIM, LANES), F32), pltpu.VMEM((2, HEAD_DIM, LANES), F32),
                        pltpu.VMEM((HEAD_DIM, HEAD_DIM, LANES), F32)],
        compiler_params=_params("arbitrary"),
        name="rwkv_scan",
    )(*args, s0)
    return out[:n_src], out[n_src], out[n_src + 1]


def _rw_post_kernel(yf_ref, yb_ref, bonus_ref, g_ref, gw_ref, gb_ref, ones_ref, o_ref, yn_scr, *, nb):
    tt = SCAN_TC
    inv = 1.0 / HEAD_DIM
    y = (yf_ref[...] + yb_ref[...]).reshape(tt * nb, RW_DIM)
    mu = _head_sum(y, ones_ref) * inv
    d = y - mu
    var = _head_sum(d * d, ones_ref) * inv
    yn = d * lax.rsqrt(var + GN_EPS) * gw_ref[...] + gb_ref[...]
    yn_scr[...] = yn.reshape(tt, nb, RW_DIM)

    def body(b, carry):
        o_ref[b] = (yn_scr[:, b, :] + bonus_ref[b]) * g_ref[b]
        return carry

    lax.fori_loop(0, nb, body, 0)


def _rw_post(y_f, y_b, bonus, g, gn_w, gn_b, ones, nb, t):
    tt = SCAN_TC
    tok = pl.BlockSpec((nb, tt, RW_DIM), lambda c: (0, c, 0))
    tmaj = pl.BlockSpec((tt, nb, RW_DIM), lambda c: (c, 0, 0))
    vec = pl.BlockSpec((1, RW_DIM), lambda c: (0, 0))
    out = pl.pallas_call(
        functools.partial(_rw_post_kernel, nb=nb),
        out_shape=jax.ShapeDtypeStruct((nb, t, RW_DIM), F32),
        grid=(t // tt,),
        in_specs=[tmaj, tmaj, tok, tok, vec, vec, pl.BlockSpec((RW_DIM, RW_DIM), lambda c: (0, 0))],
        out_specs=tok,
        scratch_shapes=[pltpu.VMEM((tt, nb, RW_DIM), F32)],
        compiler_params=_params("parallel"),
        name="rwkv_post",
    )(y_f, y_b, bonus.reshape(nb, t, RW_DIM), g.reshape(nb, t, RW_DIM), gn_w.reshape(1, RW_DIM),
      gn_b.reshape(1, RW_DIM), ones)
    return out.reshape(nb * t, RW_DIM)


def _scan_state(states):
    z = jnp.concatenate([s.transpose(2, 3, 1, 0).reshape(HEAD_DIM, HEAD_DIM, -1) for s in states], axis=-1)
    return jnp.pad(z, ((0, 0), (0, 0), (0, LANES - z.shape[-1])))


def _unscan_reset_state(sf, src):
    n = RW_HEADS * SCAN_NB
    z = sf[:, :, :, src * n:(src + 1) * n].reshape(-1, HEAD_DIM, HEAD_DIM, RW_HEADS, SCAN_NB)
    return z.transpose(0, 4, 3, 1, 2).reshape(-1, RW_HEADS, HEAD_DIM, HEAD_DIM)


def _rwkv_operands(h, seq0, nb, t, *args):
    dw_f, dw_b, kd_f, kd_b, b_f, b_b, kk, r, v, bonus, g = _rw_prep(h.reshape(-1, t, D_MODEL), seq0, nb, t, *args)
    return (dw_f, kk, b_f, kd_f, r, v), (dw_b, kk, b_b, kd_b, r, v), bonus, g


def _rwkv(h, w_rw, lora_w, lora_b, g2, lp, ones, s0_f, s0_b):
    assert DEC_BATCH == SCAN_NB and BATCH * SEQ == DEC_BATCH * DEC_SEQ
    args = (w_rw, lora_w, lora_b, g2, lp, ones)
    c_f, c_b, c_bonus, c_g = _rwkv_operands(h, 0, BATCH, SEQ, *args)
    l_f, l_b, l_bonus, l_g = _rwkv_operands(h, N_CTX // DEC_SEQ, DEC_BATCH, DEC_SEQ, *args)
    (yl_f, yl_b, yc_f, yc_b), _, sf_ctx = _rw_scan([l_f, l_b], [c_f, c_b], _scan_state([s0_f, s0_b]),
                                                   DEC_SEQ, SEQ, [False, True, False, True])
    post = (lp["gn_w"], lp["gn_b"], ones)
    rw_ctx = _rw_post(yc_f, yc_b, c_bonus, c_g, *post, BATCH, SEQ)
    rw_lat = _rw_post(yl_f, yl_b, l_bonus, l_g, *post, DEC_BATCH, DEC_SEQ)
    return rw_ctx, rw_lat, _unscan_reset_state(sf_ctx, 2), _unscan_reset_state(sf_ctx, 3)


_TM_MERGE = 512


def _merge_kernel(x_ref, gm_ref, h_ref, nac_ref, nal_ref, rwc_ref, rwl_ref, mlc_ref, mll_ref, wg_ref, wbr_ref, wo_ref,
                  o_ref):
    is_ctx = pl.program_id(0) < N_CTX // _TM_MERGE
    h = h_ref[...]
    branches = tuple(jnp.where(is_ctx, c_ref[...], l_ref[...])
                     for c_ref, l_ref in ((nac_ref, nal_ref), (rwc_ref, rwl_ref), (mlc_ref, mll_ref)))
    m = None
    for i, o_b in enumerate(branches):
        gate = jnp.dot(h, wg_ref[:, i * D_MODEL:(i + 1) * D_MODEL], preferred_element_type=F32)
        br = jnp.dot(o_b.astype(BF16), wbr_ref[i], preferred_element_type=F32)
        t = _sigmoid(gate) * br
        m = t if m is None else m + t
    mix = jnp.dot(m.astype(BF16), wo_ref[...], preferred_element_type=F32)
    o_ref[...] = x_ref[...] + gm_ref[...] * mix


def _merge(x, mod_l, h, na_ctx, na_lat, rw_ctx, rw_lat, mla_ctx, mla_lat, w_gate, w_br, w_o):
    tm = _TM_MERGE
    n_ctx = N_CTX // tm
    row = pl.BlockSpec((tm, D_MODEL), lambda i: (i, 0))
    br_ctx = pl.BlockSpec((tm, BRANCH_DIM), lambda i: (jnp.minimum(i, n_ctx - 1), 0))
    br_lat = pl.BlockSpec((tm, BRANCH_DIM), lambda i: (jnp.maximum(i - n_ctx, 0), 0))
    return pl.pallas_call(
        _merge_kernel,
        out_shape=jax.ShapeDtypeStruct((N_TOK, D_MODEL), F32),
        grid=(N_TOK // tm,),
        in_specs=[row, pl.BlockSpec((None, 1, D_MODEL), lambda i: (_cond_row(i, tm), 0, 5)),
                  row, br_ctx, br_lat, br_ctx, br_lat, br_ctx, br_lat,
                  pl.BlockSpec((D_MODEL, N_BRANCH * D_MODEL), lambda i: (0, 0)),
                  pl.BlockSpec((N_BRANCH, BRANCH_DIM, D_MODEL), lambda i: (0, 0, 0)),
                  pl.BlockSpec((D_MODEL, D_MODEL), lambda i: (0, 0))],
        out_specs=row,
        compiler_params=_params("arbitrary"),
        name="merge",
    )(x, mod_l, h, na_ctx, na_lat, rw_ctx, rw_lat, mla_ctx, mla_lat, w_gate, w_br, w_o)


def _block_diag(blocks):
    rows = sum(b.shape[0] for b in blocks)
    cols = sum(b.shape[1] for b in blocks)
    out = jnp.zeros((rows, cols), blocks[0].dtype)
    r = c = 0
    for b in blocks:
        out = lax.dynamic_update_slice(out, b, (r, c))
        r += b.shape[0]
        c += b.shape[1]
    return out


def _pad_cols(w, left, total):
    return jnp.pad(w, ((0, 0), (left, total - left - w.shape[1])))


def _mla_uq_layout(w_uq):
    per = NOPE_DIM + ROPE_DIM
    return jnp.concatenate([_pad_cols(w_uq[:, h * per:(h + 1) * per], 0, LANES) for h in range(MLA_HEADS)], axis=1)


def _mla_ukv_layout(w_ukv):
    per = NOPE_DIM + V_DIM
    k = [_pad_cols(w_ukv[:, h * per:h * per + NOPE_DIM], 0, LANES) for h in range(MLA_HEADS)]
    v = [w_ukv[:, h * per + NOPE_DIM:(h + 1) * per] for h in range(MLA_HEADS)]
    return jnp.concatenate(k + v, axis=1)


def _heads_on_lanes(cache):
    b, h, t, d = cache.shape
    return cache.transpose(0, 2, 1, 3).reshape(b, t, h * d)


def _heads_major(z, nb, t, nh):
    return z.reshape(nb, t, nh, -1).transpose(0, 2, 1, 3)


def kernel(x_prompt, x_sample, c, cache_na_k, cache_na_v, cache_mla_ckv, cache_mla_krope, state_rwkv_fwd, state_rwkv_bwd, c_ctx, ada_w, ada_b, norm_g, ffn_wg, ffn_wu, ffn_wd, w_in, na_rpb, rw_w0, rw_w2, rw_a0, rw_a2, rw_g2, rw_kk, rw_ka, rw_rk, rw_gn_w, rw_gn_b, mla_qn_g, mla_kvn_g, mla_w_uq, mla_w_ukv, w_br, w_o, final_g):
    x = (x_prompt.reshape(N_CTX, D_MODEL), x_sample.reshape(N_LAT, D_MODEL))
    cond = jnp.concatenate([c_ctx[None, :], c, jnp.zeros((N_COND - 1 - DEC_BATCH, D_MODEL), F32)], axis=0)
    mod = _modulation(cond, ada_w, ada_b)

    ones = jnp.asarray(np.kron(np.eye(RW_HEADS), np.ones((HEAD_DIM, HEAD_DIM))), BF16)
    rope = _rope_tables(DEC_SEQ)

    col_rw = 3 * NA_HEADS * HEAD_DIM
    col_mla = col_rw + 3 * RW_DIM + 2 * DECAY_LORA + 2 * ICLR_LORA + GATE_LORA
    col_kr = col_mla + Q_LORA + KV_LORA
    col_gate = col_kr + ROPE_DIM

    caches = []
    y = None
    for l in range(DEPTH):
        mod_l = mod[l]
        wi = w_in[l].astype(BF16)
        w_na = wi[:, :col_rw]
        w_rw = wi[:, col_rw:col_mla]
        w_mla = jnp.concatenate([wi[:, col_mla:col_kr], _pad_cols(wi[:, col_kr:col_gate], _ROPE_LO, LANES)], axis=1)
        w_gate = wi[:, col_gate:]

        x, h = _ffn(x, norm_g[l, 0], mod_l, 0, ffn_wg[l, 0].astype(BF16), ffn_wu[l, 0].astype(BF16),
                    ffn_wd[l, 0].astype(BF16), post="mod", post_g=norm_g[l, 1], post_chunk=3)

        z_na = _mm(h, w_na, name="proj_na")
        na_ctx = _na_ctx(z_na)
        na_lat = _na_latent(na_rpb[l].reshape(NA_HEADS * (2 * NA_WIN_H - 1), 2 * NA_WIN_W - 1), z_na,
                            _heads_on_lanes(cache_na_k[:, l]), _heads_on_lanes(cache_na_v[:, l]))

        lora_w = _block_diag([rw_w2[l, 0], rw_w2[l, 1], rw_a2[l, 0], rw_a2[l, 1]]).astype(BF16)
        lora_b = jnp.concatenate([rw_w0[l, 0], rw_w0[l, 1], rw_a0[l, 0], rw_a0[l, 1]])
        lp = {"kk": rw_kk[l], "ka": rw_ka[l], "rk": rw_rk[l].reshape(RW_DIM), "gn_w": rw_gn_w[l], "gn_b": rw_gn_b[l]}
        rw_ctx, rw_lat, sc_f, sc_b = _rwkv(h, w_rw, lora_w, lora_b, rw_g2[l].astype(BF16), lp, ones,
                                           state_rwkv_fwd[:, l], state_rwkv_bwd[:, l])

        w_ukv = _mla_ukv_layout(mla_w_ukv[l]).astype(BF16)
        q_m, kv_m, ckv, kr_m = _mla_prep(h, w_mla, mla_qn_g[l], mla_kvn_g[l],
                                         _mla_uq_layout(mla_w_uq[l]).astype(BF16), w_ukv)
        kv_cache = _mm(cache_mla_ckv[:, l].reshape(DEC_BATCH * PAST_LEN, KV_LORA), w_ukv, "mla_ukv_cache")
        kr_cache = _pad_cols(cache_mla_krope[:, l].reshape(DEC_BATCH * PAST_LEN, ROPE_DIM), _ROPE_LO, LANES)
        mla_ctx = _mla_ctx(q_m, kv_m, kr_m)
        mla_lat = _mla_latent(q_m, kv_m, kr_m, kv_cache, kr_cache, rope)

        x = _merge(x, mod_l, h, na_ctx, na_lat, rw_ctx, rw_lat, mla_ctx, mla_lat, w_gate,
                   w_br[l].astype(BF16), w_o[l].astype(BF16))

        ffn2 = (x, norm_g[l, 2], mod_l, 6, ffn_wg[l, 1].astype(BF16), ffn_wu[l, 1].astype(BF16),
                ffn_wd[l, 1].astype(BF16))
        if l == DEPTH - 1:
            y = _ffn(*ffn2, post="plain", post_g=final_g)
        else:
            x = _ffn(*ffn2)

        w = NA_HEADS * HEAD_DIM
        caches.append((_heads_major(z_na[:N_CTX, w:2 * w], BATCH, SEQ, NA_HEADS),
                       _heads_major(z_na[:N_CTX, 2 * w:3 * w], BATCH, SEQ, NA_HEADS),
                       ckv[:N_CTX].reshape(BATCH, SEQ, KV_LORA),
                       kr_m[:N_CTX, _ROPE_LO:_ROPE_LO + ROPE_DIM].reshape(BATCH, SEQ, ROPE_DIM),
                       sc_f, sc_b))

    y_prompt = y[0].reshape(BATCH, SEQ, D_MODEL)
    y_sample = y[1].reshape(DEC_BATCH, DEC_SEQ, D_MODEL)
    outs = [jnp.stack([cl[i] for cl in caches], axis=1) for i in range(6)]
    return (y_prompt, y_sample, *outs)
```

```python
import functools

import numpy as np
import jax
import jax.numpy as jnp
from jax import lax
from jax.experimental import pallas as pl
from jax.experimental.pallas import tpu as pltpu

F32 = jnp.float32
BF16 = jnp.bfloat16

D_MODEL = 1024
BATCH = 32
SEQ = 256
DEPTH = 2
DEC_BATCH = 8
DEC_SEQ = 1024
PAST_LEN = 256
GRID_W = 64
HEAD_DIM = 64
NA_HEADS = 4
NA_WIN_H = 8
NA_WIN_W = 16
RW_HEADS = 4
RW_DIM = RW_HEADS * HEAD_DIM
DECAY_LORA = 64
ICLR_LORA = 64
GATE_LORA = 128
GN_EPS = 64e-5
MLA_HEADS = 4
Q_LORA = 256
KV_LORA = 128
NOPE_DIM = 64
ROPE_DIM = 32
V_DIM = 64
MLA_SCALE = (NOPE_DIM + ROPE_DIM) ** -0.5
ROPE_BASE = 10000.0
N_BRANCH = 3
BRANCH_DIM = 256
D_FF = 2816
N_MOD = 9
NORM_EPS = 1e-6
NEG_INF = -1e30

N_CTX = BATCH * SEQ
N_LAT = DEC_BATCH * DEC_SEQ
N_TOK = N_CTX + N_LAT
N_COND = 16

LANES = 128
VMEM_LIMIT = 48 * 1024 * 1024

TM = 1024
TF = 256
SCAN_TC = 32


def _cond_row(i, tm):
    return jnp.where(i * tm < N_CTX, 0, 1 + (i * tm - N_CTX) // DEC_SEQ)


def _sigmoid(x):
    return 1.0 / (1.0 + jnp.exp(-x))


def _params(*sem):
    return pltpu.CompilerParams(dimension_semantics=sem, vmem_limit_bytes=VMEM_LIMIT)


def _mm_kernel(a_ref, w_ref, o_ref):
    o_ref[...] = jnp.dot(a_ref[...].astype(BF16), w_ref[...], preferred_element_type=F32)


def _mm(a, w, name):
    m, k = a.shape
    n = w.shape[1]
    tm = min(m, TM)
    assert m % tm == 0 and w.shape[0] == k
    return pl.pallas_call(
        _mm_kernel,
        out_shape=jax.ShapeDtypeStruct((m, n), F32),
        grid=(m // tm,),
        in_specs=[pl.BlockSpec((tm, k), lambda i: (i, 0)), pl.BlockSpec((k, n), lambda i: (0, 0))],
        out_specs=pl.BlockSpec((tm, n), lambda i: (i, 0)),
        compiler_params=_params("parallel"),
        name=name,
    )(a, w)


def _mod_kernel(c_ref, w_ref, b_ref, o_ref):
    c = c_ref[...]
    a = (c * _sigmoid(c)).astype(BF16)
    o_ref[...] = jnp.dot(a, w_ref[...].astype(BF16), preferred_element_type=F32) + b_ref[...]


def _modulation(cond, ada_w, ada_b):
    n = N_MOD * D_MODEL
    tn = D_MODEL
    out = pl.pallas_call(
        _mod_kernel,
        out_shape=jax.ShapeDtypeStruct((DEPTH, N_COND, n), F32),
        grid=(DEPTH, n // tn),
        in_specs=[pl.BlockSpec((N_COND, D_MODEL), lambda l, j: (0, 0)),
                  pl.BlockSpec((None, D_MODEL, tn), lambda l, j: (l, 0, j)),
                  pl.BlockSpec((None, 1, tn), lambda l, j: (l, 0, j))],
        out_specs=pl.BlockSpec((None, N_COND, tn), lambda l, j: (l, 0, j)),
        compiler_params=_params("parallel", "arbitrary"),
        name="modulation",
    )(cond, ada_w, ada_b.reshape(DEPTH, 1, n))
    return out.reshape(DEPTH, N_COND, 1, n)


def _rms(x, g):
    return x * lax.rsqrt(jnp.mean(x * x, axis=-1, keepdims=True) + NORM_EPS) * g


_FFN_CTX_TILES = N_CTX // TM


def _ffn_kernel(*refs, post, split_in):
    n_x = 2 if split_in else 1
    x_refs = refs[:n_x]
    g_ref, sh_ref, sc_ref, gt_ref, wg_ref, wu_ref, wd_ref = refs[n_x:n_x + 7]
    rest = refs[n_x + 7:]
    if post == "mod":
        g2_ref, sh2_ref, sc2_ref, o_ref, h2_ref, h_scr, acc_scr = rest
    elif post == "plain":
        g2_ref, ya_ref, yb_ref, h_scr, acc_scr = rest
    else:
        o_ref, h_scr, acc_scr = rest
    f = pl.program_id(1)
    is_ctx = pl.program_id(0) < _FFN_CTX_TILES

    def load_x():
        if split_in:
            return jnp.where(is_ctx, x_refs[0][...], x_refs[1][...])
        return x_refs[0][...]

    @pl.when(f == 0)
    def _():
        y = _rms(load_x(), g_ref[...])
        h_scr[...] = (y * (1.0 + sc_ref[...]) + sh_ref[...]).astype(BF16)
        acc_scr[...] = jnp.zeros_like(acc_scr)

    h = h_scr[...]
    gg = jnp.dot(h, wg_ref[...], preferred_element_type=F32)
    uu = jnp.dot(h, wu_ref[...], preferred_element_type=F32)
    a = (gg * _sigmoid(gg)) * uu
    acc_scr[...] += jnp.dot(a.astype(BF16), wd_ref[...], preferred_element_type=F32)

    @pl.when(f == pl.num_programs(1) - 1)
    def _():
        xn = load_x() + 0.5 * gt_ref[...] * acc_scr[...]
        if post == "plain":
            y = _rms(xn, g2_ref[...])

            @pl.when(is_ctx)
            def _():
                ya_ref[...] = y

            @pl.when(jnp.logical_not(is_ctx))
            def _():
                yb_ref[...] = y
        else:
            o_ref[...] = xn
        if post == "mod":
            h2_ref[...] = (_rms(xn, g2_ref[...]) * (1.0 + sc2_ref[...]) + sh2_ref[...]).astype(BF16)


def _ffn(x, norm_g, mod_l, chunk0, wg, wu, wd, post=None, post_g=None, post_chunk=None):
    def mod_spec(c):
        return pl.BlockSpec((None, 1, D_MODEL), lambda i, f: (_cond_row(i, TM), 0, c))

    row = pl.BlockSpec((TM, D_MODEL), lambda i, f: (i, 0))
    row_ctx = pl.BlockSpec((TM, D_MODEL), lambda i, f: (jnp.minimum(i, _FFN_CTX_TILES - 1), 0))
    row_lat = pl.BlockSpec((TM, D_MODEL), lambda i, f: (jnp.maximum(i - _FFN_CTX_TILES, 0), 0))
    vec = pl.BlockSpec((1, D_MODEL), lambda i, f: (0, 0))
    split_in = isinstance(x, tuple)
    in_specs = ([row_ctx, row_lat] if split_in else [row]) + [
        vec, mod_spec(chunk0), mod_spec(chunk0 + 1), mod_spec(chunk0 + 2),
        pl.BlockSpec((D_MODEL, TF), lambda i, f: (0, f)),
        pl.BlockSpec((D_MODEL, TF), lambda i, f: (0, f)),
        pl.BlockSpec((TF, D_MODEL), lambda i, f: (f, 0))]
    args = (list(x) if split_in else [x]) + [norm_g.reshape(1, D_MODEL), mod_l, mod_l, mod_l, wg, wu, wd]
    xs = jax.ShapeDtypeStruct((N_TOK, D_MODEL), F32)
    if post == "mod":
        in_specs += [vec, mod_spec(post_chunk), mod_spec(post_chunk + 1)]
        args += [post_g.reshape(1, D_MODEL), mod_l, mod_l]
        out_shape, out_specs = (xs, jax.ShapeDtypeStruct((N_TOK, D_MODEL), BF16)), (row, row)
    elif post == "plain":
        in_specs += [vec]
        args += [post_g.reshape(1, D_MODEL)]
        out_shape = (jax.ShapeDtypeStruct((N_CTX, D_MODEL), F32), jax.ShapeDtypeStruct((N_LAT, D_MODEL), F32))
        out_specs = (row_ctx, row_lat)
    else:
        out_shape, out_specs = xs, row
    return pl.pallas_call(
        functools.partial(_ffn_kernel, post=post, split_in=split_in),
        out_shape=out_shape,
        grid=(N_TOK // TM, D_FF // TF),
        in_specs=in_specs,
        out_specs=out_specs,
        scratch_shapes=[pltpu.VMEM((TM, D_MODEL), BF16), pltpu.VMEM((TM, D_MODEL), F32)],
        compiler_params=_params("arbitrary", "arbitrary"),
        name="ffn",
    )(*args)


_NT = (((1,), (1,)), ((), ()))
_LAT_ROWS = DEC_SEQ // GRID_W
_NA_KH = min(NA_WIN_H, _LAT_ROWS)
_NA_LOCAL = _NA_KH * GRID_W
_NA_PAIRS = 2 * NA_WIN_H - 2


def _head_mask(width, h, group):
    lane = lax.broadcasted_iota(jnp.int32, (1, width), 1)
    return (lane >= h * group) & (lane < (h + 1) * group)


def _stack_heads(q, nh, group):
    return jnp.concatenate([jnp.where(_head_mask(q.shape[1], h, group), q, 0.0) for h in range(nh)], axis=0)


def _pick_heads(r, nh, group):
    t = r.shape[0] // nh
    out = r[0:t]
    for h in range(1, nh):
        out = jnp.where(_head_mask(r.shape[1], h, group), r[h * t:(h + 1) * t], out)
    return out


def _softmax2(s_a, s_b):
    m = jnp.maximum(jnp.max(s_a, axis=-1, keepdims=True), jnp.max(s_b, axis=-1, keepdims=True))
    p_a = jnp.exp(s_a - m)
    p_b = jnp.exp(s_b - m)
    return p_a, p_b, jnp.sum(p_a, axis=-1, keepdims=True) + jnp.sum(p_b, axis=-1, keepdims=True)


def _na_ctx_kernel(q_ref, k_ref, v_ref, o_ref):
    q_st = _stack_heads(q_ref[...] * HEAD_DIM ** -0.5, NA_HEADS, HEAD_DIM).astype(BF16)
    s = lax.dot_general(q_st, k_ref[...].astype(BF16), _NT, preferred_element_type=F32)
    p = jnp.exp(s - jnp.max(s, axis=-1, keepdims=True))
    l = jnp.sum(p, axis=-1, keepdims=True)
    r = jnp.dot(p.astype(BF16), v_ref[...].astype(BF16), preferred_element_type=F32) / l
    o_ref[...] = _pick_heads(r, NA_HEADS, HEAD_DIM)


def _na_ctx(z_na):
    w = NA_HEADS * HEAD_DIM
    return pl.pallas_call(
        _na_ctx_kernel,
        out_shape=jax.ShapeDtypeStruct((N_CTX, w), F32),
        grid=(BATCH,),
        in_specs=[pl.BlockSpec((SEQ, w), lambda b: (b, 0)),
                  pl.BlockSpec((SEQ, w), lambda b: (b, 1)),
                  pl.BlockSpec((SEQ, w), lambda b: (b, 2))],
        out_specs=pl.BlockSpec((SEQ, w), lambda b: (b, 0)),
        compiler_params=_params("parallel"),
        name="na_ctx",
    )(z_na, z_na, z_na)


def _na_row_lo(qr):
    return min(max(qr - NA_WIN_H // 2, 0), _LAT_ROWS - _NA_KH)


def _na_lat_kernel(rpb_ref, q_ref, k_ref, v_ref, kc_ref, vc_ref, o_ref, bias_scr):
    scale = HEAD_DIM ** -0.5

    @pl.when(pl.program_id(0) == 0)
    def _():
        qc = lax.broadcasted_iota(jnp.int32, (GRID_W, 2 * GRID_W), 0)
        lane = lax.broadcasted_iota(jnp.int32, (GRID_W, 2 * GRID_W), 1)
        kc = lane & (GRID_W - 1)
        col_lo = jnp.clip(qc - NA_WIN_W // 2, 0, GRID_W - NA_WIN_W)
        rel = jnp.where((kc >= col_lo) & (kc < col_lo + NA_WIN_W), kc - qc + NA_WIN_W - 1, -1)
        second = lax.broadcasted_iota(jnp.int32, (1, 2 * GRID_W), 1) >= GRID_W

        def build(idx, carry):
            h = idx // _NA_PAIRS
            row = h * (2 * NA_WIN_H - 1) + (idx - h * _NA_PAIRS)
            tile = jnp.full((GRID_W, 2 * GRID_W), NEG_INF, F32)
            for j in range(2 * NA_WIN_W - 1):
                val = jnp.where(second, rpb_ref[row + 1, j], rpb_ref[row, j])
                tile = jnp.where(rel == j, val, tile)
            bias_scr[idx] = tile
            return carry

        lax.fori_loop(0, NA_HEADS * _NA_PAIRS, build, 0)

    kc = kc_ref[...].astype(BF16)
    vc = vc_ref[...].astype(BF16)
    for qr in range(_LAT_ROWS):
        lo = _na_row_lo(qr)
        dr0 = lo - qr + NA_WIN_H - 1
        q_st = _stack_heads(q_ref[qr * GRID_W:(qr + 1) * GRID_W, :] * scale, NA_HEADS, HEAD_DIM).astype(BF16)
        kl = k_ref[lo * GRID_W:lo * GRID_W + _NA_LOCAL, :].astype(BF16)
        vl = v_ref[lo * GRID_W:lo * GRID_W + _NA_LOCAL, :].astype(BF16)
        bias = jnp.concatenate(
            [jnp.concatenate([bias_scr[h * _NA_PAIRS + dr0 + 2 * i] for i in range(_NA_KH // 2)], axis=1)
             for h in range(NA_HEADS)], axis=0)
        s_loc = lax.dot_general(q_st, kl, _NT, preferred_element_type=F32) + bias
        s_ctx = lax.dot_general(q_st, kc, _NT, preferred_element_type=F32)
        p_loc, p_ctx, l = _softmax2(s_loc, s_ctx)
        r = (jnp.dot(p_loc.astype(BF16), vl, preferred_element_type=F32)
             + jnp.dot(p_ctx.astype(BF16), vc, preferred_element_type=F32)) / l
        o_ref[qr * GRID_W:(qr + 1) * GRID_W, :] = _pick_heads(r, NA_HEADS, HEAD_DIM)


def _na_latent(rpb, z_na, kc, vc):
    w = NA_HEADS * HEAD_DIM
    lat0 = N_CTX // DEC_SEQ

    def col(c):
        return pl.BlockSpec((DEC_SEQ, w), lambda b: (lat0 + b, c))

    cache = pl.BlockSpec((None, PAST_LEN, w), lambda b: (b, 0, 0))
    return pl.pallas_call(
        _na_lat_kernel,
        out_shape=jax.ShapeDtypeStruct((N_LAT, w), F32),
        grid=(DEC_BATCH,),
        in_specs=[pl.BlockSpec(memory_space=pltpu.SMEM), col(0), col(1), col(2), cache, cache],
        out_specs=pl.BlockSpec((DEC_SEQ, w), lambda b: (b, 0)),
        scratch_shapes=[pltpu.VMEM((NA_HEADS * _NA_PAIRS, GRID_W, 2 * GRID_W), F32)],
        compiler_params=_params("arbitrary"),
        name="na_latent",
    )(rpb, z_na, z_na, z_na, kc, vc)


_MLA_W = MLA_HEADS * LANES
_ROPE_LO = NOPE_DIM
_ROPE_HALF = ROPE_DIM // 2


def _rope_tables(t_len):
    n_freq = ROPE_DIM // 4
    inv = 1.0 / (ROPE_BASE ** (np.arange(n_freq) / n_freq))
    pos = np.arange(t_len)
    ang = np.concatenate([(pos // GRID_W)[:, None] * inv, (pos % GRID_W)[:, None] * inv], axis=-1)
    cos, sin = np.cos(ang), np.sin(ang)
    c = np.ones((t_len, LANES))
    s1 = np.zeros((t_len, LANES))
    s2 = np.zeros((t_len, LANES))
    a, b, e = _ROPE_LO, _ROPE_LO + _ROPE_HALF, _ROPE_LO + ROPE_DIM
    c[:, a:b] = cos
    c[:, b:e] = cos
    s1[:, a:b] = -sin
    s2[:, b:e] = sin
    return tuple(jnp.asarray(t, F32) for t in (c, s1, s2))


def _apply_rope(x, c, s1, s2):
    up = pltpu.roll(x, LANES - _ROPE_HALF, 1)
    dn = pltpu.roll(x, _ROPE_HALF, 1)
    return x * c + up * s1 + dn * s2


def _mla_prep_kernel(h_ref, w_ref, qg_ref, kvg_ref, wuq_ref, wukv_ref, q_ref, kv_ref, ckv_ref, kr_ref):
    z = jnp.dot(h_ref[...], w_ref[...], preferred_element_type=F32)
    cq = _rms(z[:, :Q_LORA], qg_ref[...])
    q_ref[...] = jnp.dot(cq.astype(BF16), wuq_ref[...], preferred_element_type=F32)
    ckv = _rms(z[:, Q_LORA:Q_LORA + KV_LORA], kvg_ref[...])
    ckv_ref[...] = ckv
    kv_ref[...] = jnp.dot(ckv.astype(BF16), wukv_ref[...], preferred_element_type=F32)
    kr_ref[...] = z[:, Q_LORA + KV_LORA:]


def _mla_prep(h, w_mla, qn_g, kvn_g, w_uq, w_ukv):
    wv = MLA_HEADS * V_DIM
    row = lambda n: pl.BlockSpec((TM, n), lambda i: (i, 0))
    full = lambda a: pl.BlockSpec(a.shape, lambda i: (0, 0))
    qn_g = qn_g.reshape(1, Q_LORA)
    kvn_g = kvn_g.reshape(1, KV_LORA)
    return pl.pallas_call(
        _mla_prep_kernel,
        out_shape=(jax.ShapeDtypeStruct((N_TOK, _MLA_W), F32), jax.ShapeDtypeStruct((N_TOK, _MLA_W + wv), F32),
                   jax.ShapeDtypeStruct((N_TOK, KV_LORA), F32), jax.ShapeDtypeStruct((N_TOK, LANES), F32)),
        grid=(N_TOK // TM,),
        in_specs=[row(D_MODEL), full(w_mla), full(qn_g), full(kvn_g), full(w_uq), full(w_ukv)],
        out_specs=(row(_MLA_W), row(_MLA_W + wv), row(KV_LORA), row(LANES)),
        compiler_params=_params("parallel"),
        name="mla_prep",
    )(h, w_mla, qn_g, kvn_g, w_uq, w_ukv)


def _mla_ctx_kernel(q_ref, kv_ref, kr_ref, o_ref):
    kr = kr_ref[...]
    v_all = kv_ref[:, _MLA_W:].astype(BF16)
    out = None
    for h in range(MLA_HEADS):
        qh = (q_ref[:, h * LANES:(h + 1) * LANES] * MLA_SCALE).astype(BF16)
        kh = (kv_ref[:, h * LANES:(h + 1) * LANES] + kr).astype(BF16)
        s = lax.dot_general(qh, kh, _NT, preferred_element_type=F32)
        p = jnp.exp(s - jnp.max(s, axis=-1, keepdims=True))
        l = jnp.sum(p, axis=-1, keepdims=True)
        r = jnp.dot(p.astype(BF16), v_all, preferred_element_type=F32) / l
        out = r if out is None else jnp.where(_head_mask(MLA_HEADS * V_DIM, h, V_DIM), r, out)
    o_ref[...] = out


def _mla_ctx(q_m, kv_m, kr_m):
    wv = MLA_HEADS * V_DIM
    return pl.pallas_call(
        _mla_ctx_kernel,
        out_shape=jax.ShapeDtypeStruct((N_CTX, wv), F32),
        grid=(BATCH,),
        in_specs=[pl.BlockSpec((SEQ, _MLA_W), lambda b: (b, 0)),
                  pl.BlockSpec((SEQ, _MLA_W + wv), lambda b: (b, 0)),
                  pl.BlockSpec((SEQ, LANES), lambda b: (b, 0))],
        out_specs=pl.BlockSpec((SEQ, wv), lambda b: (b, 0)),
        compiler_params=_params("parallel"),
        name="mla_ctx",
    )(q_m, kv_m, kr_m)


_MLA_BQ = 256


def _mla_lat_kernel(q_ref, kv_ref, kr_ref, kvc_ref, krc_ref, cq_ref, s1q_ref, s2q_ref, ck_ref, s1k_ref, s2k_ref,
                    o_ref):
    kr = _apply_rope(kr_ref[...], ck_ref[...], s1k_ref[...], s2k_ref[...])
    krc = krc_ref[...]
    v_lat = kv_ref[:, _MLA_W:].astype(BF16)
    v_ctx = kvc_ref[:, _MLA_W:].astype(BF16)
    out = None
    for h in range(MLA_HEADS):
        blk = slice(h * LANES, (h + 1) * LANES)
        qh = (_apply_rope(q_ref[:, blk], cq_ref[...], s1q_ref[...], s2q_ref[...]) * MLA_SCALE).astype(BF16)
        kl = (kv_ref[:, blk] + kr).astype(BF16)
        kc = (kvc_ref[:, blk] + krc).astype(BF16)
        s_l = lax.dot_general(qh, kl, _NT, preferred_element_type=F32)
        s_c = lax.dot_general(qh, kc, _NT, preferred_element_type=F32)
        p_l, p_c, l = _softmax2(s_l, s_c)
        r = (jnp.dot(p_l.astype(BF16), v_lat, preferred_element_type=F32)
             + jnp.dot(p_c.astype(BF16), v_ctx, preferred_element_type=F32)) / l
        out = r if out is None else jnp.where(_head_mask(MLA_HEADS * V_DIM, h, V_DIM), r, out)
    o_ref[...] = out


def _mla_latent(q_m, kv_m, kr_m, kv_cache, kr_cache, rope):
    wv = MLA_HEADS * V_DIM
    nq = DEC_SEQ // _MLA_BQ
    q0 = N_CTX // _MLA_BQ
    lat0 = N_CTX // DEC_SEQ
    c, s1, s2 = rope
    tq = pl.BlockSpec((_MLA_BQ, LANES), lambda b, j: (j, 0))
    tk = pl.BlockSpec((DEC_SEQ, LANES), lambda b, j: (0, 0))
    return pl.pallas_call(
        _mla_lat_kernel,
        out_shape=jax.ShapeDtypeStruct((N_LAT, wv), F32),
        grid=(DEC_BATCH, nq),
        in_specs=[pl.BlockSpec((_MLA_BQ, _MLA_W), lambda b, j: (q0 + b * nq + j, 0)),
                  pl.BlockSpec((DEC_SEQ, _MLA_W + wv), lambda b, j: (lat0 + b, 0)),
                  pl.BlockSpec((DEC_SEQ, LANES), lambda b, j: (lat0 + b, 0)),
                  pl.BlockSpec((PAST_LEN, _MLA_W + wv), lambda b, j: (b, 0)),
                  pl.BlockSpec((PAST_LEN, LANES), lambda b, j: (b, 0)),
                  tq, tq, tq, tk, tk, tk],
        out_specs=pl.BlockSpec((_MLA_BQ, wv), lambda b, j: (b * nq + j, 0)),
        compiler_params=_params("parallel", "arbitrary"),
        name="mla_lat",
    )(q_m, kv_m, kr_m, kv_cache, kr_cache, c, s1, s2, c, s1, s2)


def _head_sum(x, ones_ref):
    ones = ones_ref[...]
    hi = x.astype(BF16)
    rest = x - hi.astype(F32)
    mid = rest.astype(BF16)
    lo = (rest - mid.astype(F32)).astype(BF16)
    return (jnp.dot(hi, ones, preferred_element_type=F32) + jnp.dot(mid, ones, preferred_element_type=F32)
            + jnp.dot(lo, ones, preferred_element_type=F32))


SCAN_NB = LANES // (4 * RW_HEADS)
_PREP_ROWS = 512
_PREP_OUTS = 11


def _lora_act(a):
    col = lax.broadcasted_iota(jnp.int32, a.shape, 1)
    return jnp.where(col < 2 * DECAY_LORA, jnp.tanh(a), a)


def _rw_prep_kernel(h_ref, perm_ref, w_ref, lw_ref, lb_ref, g2_ref, kkw_ref, ka_ref, rk_ref, ones_ref, *out_refs,
                    nbb, tt):
    h = jnp.dot(perm_ref[...], h_ref[...].reshape(nbb * tt, D_MODEL), preferred_element_type=F32).astype(BF16)
    z = jnp.dot(h, w_ref[...], preferred_element_type=F32)
    r = z[:, 0:RW_DIM]
    k = z[:, RW_DIM:2 * RW_DIM]
    v = z[:, 2 * RW_DIM:3 * RW_DIM]
    lora_in = z[:, 3 * RW_DIM:4 * RW_DIM]
    gate_in = z[:, 4 * RW_DIM:4 * RW_DIM + GATE_LORA]
    wa = jnp.dot(_lora_act(lora_in).astype(BF16), lw_ref[...], preferred_element_type=F32) + lb_ref[...]
    g = jnp.dot(_sigmoid(gate_in).astype(BF16), g2_ref[...], preferred_element_type=F32)
    kk = k * kkw_ref[...]
    kk = kk / jnp.maximum(jnp.sqrt(_head_sum(kk * kk, ones_ref)), 1e-12)
    per_dir = []
    for d in range(2):
        decay = jnp.exp(-float(np.exp(-0.5)) * _sigmoid(wa[:, d * RW_DIM:(d + 1) * RW_DIM]))
        a = _sigmoid(wa[:, (2 + d) * RW_DIM:(3 + d) * RW_DIM])
        per_dir.append((decay, k * (1.0 + (a - 1.0) * ka_ref[...]), kk * a))
    bonus = _head_sum(r * rk_ref[...] * (per_dir[0][1] + per_dir[1][1]), ones_ref) * v
    outs = (per_dir[0][0], per_dir[1][0], per_dir[0][1], per_dir[1][1], per_dir[0][2], per_dir[1][2], kk, r, v,
            bonus, g)
    for ref, val in zip(out_refs, outs):
        ref[...] = val.reshape(tt, nbb, RW_DIM)


def _rw_prep(h_seq, seq0, nb, t, w_rw, lora_w, lora_b, g2, lp, ones):
    nbb = nb if nb * 16 <= _PREP_ROWS else SCAN_NB
    tt = _PREP_ROWS // nbb
    assert seq0 % nbb == 0 and nb % nbb == 0 and t % tt == 0 and tt % 16 == 0
    b0 = seq0 // nbb
    rows = np.arange(_PREP_ROWS)
    perm = np.zeros((_PREP_ROWS, _PREP_ROWS), np.float32)
    perm[rows, (rows % nbb) * tt + rows // nbb] = 1.0

    def full(shape):
        return pl.BlockSpec(shape, lambda i, c: (0,) * len(shape))

    return pl.pallas_call(
        functools.partial(_rw_prep_kernel, nbb=nbb, tt=tt),
        out_shape=(jax.ShapeDtypeStruct((t, nb, RW_DIM), F32),) * _PREP_OUTS,
        grid=(nb // nbb, t // tt),
        in_specs=[pl.BlockSpec((nbb, tt, D_MODEL), lambda i, c: (b0 + i, c, 0)), full(perm.shape),
                  full(w_rw.shape), full(lora_w.shape), full((1, 4 * RW_DIM)), full(g2.shape),
                  full((1, RW_DIM)), full((1, RW_DIM)), full((1, RW_DIM)), full((RW_DIM, RW_DIM))],
        out_specs=(pl.BlockSpec((tt, nbb, RW_DIM), lambda i, c: (c, i, 0)),) * _PREP_OUTS,
        compiler_params=_params("parallel", "arbitrary"),
        name="rwkv_prep",
    )(h_seq, jnp.asarray(perm, BF16), w_rw, lora_w, lora_b.reshape(1, 4 * RW_DIM), g2, lp["kk"].reshape(1, RW_DIM),
      lp["ka"].reshape(1, RW_DIM), lp["rk"].reshape(1, RW_DIM), ones)


_SCAN_OPS = 6


def _half_swap(x):
    return pltpu.roll(x, LANES // 2, 1)


def _rw_scan_kernel(*refs, n_src, n_keep, period, reverse):
    nb = SCAN_NB
    tc = SCAN_TC
    ins = [refs[s * _SCAN_OPS:(s + 1) * _SCAN_OPS] for s in range(n_src)]
    s0_ref = refs[n_src * _SCAN_OPS]
    y_refs = refs[n_src * _SCAN_OPS + 1:n_src * _SCAN_OPS + 1 + n_src]
    sf_keep_ref, sf_reset_ref, ops_scr, y_scr, s_scr = refs[n_src * _SCAN_OPS + 1 + n_src:]
    c = pl.program_id(0)
    keep_lanes = n_keep * RW_HEADS * nb
    left = lax.broadcasted_iota(jnp.int32, (nb, LANES), 1) < LANES // 2

    @pl.when(c == 0)
    def _():
        s_scr[...] = s0_ref[...]
        y_scr[...] = jnp.zeros_like(y_scr)

    @pl.when((c > 0) & (c % period == 0))
    def _():
        keep = lax.broadcasted_iota(jnp.int32, (1, 1, LANES), 2) < keep_lanes
        s_scr[...] = jnp.where(keep, s_scr[...], 0.0)

    def times(q, s):
        return (tc - 1 - 2 * q, tc - 2 - 2 * q) if reverse[s] else (2 * q, 2 * q + 1)

    def relayout_in(q, ops):
        for o in range(_SCAN_OPS):
            rows = []
            for s in range(n_src):
                t0, t1 = times(q, s)
                a0 = ins[s][o][t0]
                a1 = ins[s][o][t1]
                for h in range(RW_HEADS):
                    j = h // 2
                    p0 = a0[:, LANES * j:LANES * (j + 1)]
                    p1 = a1[:, LANES * j:LANES * (j + 1)]
                    if h % 2 == 0:
                        rows.append(jnp.where(left, p0, _half_swap(p1)))
                    else:
                        rows.append(jnp.where(left, _half_swap(p0), p1))
            tile = jnp.concatenate(rows, axis=0).T
            ops[o, 0] = tile[0:HEAD_DIM]
            ops[o, 1] = tile[HEAD_DIM:]

    def steps(ops, y):
        for t in range(2):
            for vi in range(HEAD_DIM):
                s = s_scr[vi]
                sa = jnp.sum(s * ops[1, t], axis=0, keepdims=True)
                s = s * ops[0, t] - sa * ops[2, t] + ops[5, t, pl.ds(vi, 1), :] * ops[3, t]
                s_scr[vi] = s
                y[t, pl.ds(vi, 1), :] = jnp.sum(s * ops[4, t], axis=0, keepdims=True)

    def relayout_out(q, y):
        tile = jnp.concatenate([y[0], y[1]], axis=0).T
        for s in range(n_src):
            t0, t1 = times(q, s)
            for j in range(RW_HEADS // 2):
                r0 = (s * RW_HEADS + 2 * j) * nb
                even = tile[r0:r0 + nb]
                odd = tile[r0 + nb:r0 + 2 * nb]
                y_refs[s][t0, :, LANES * j:LANES * (j + 1)] = jnp.where(left, even, _half_swap(odd))
                y_refs[s][t1, :, LANES * j:LANES * (j + 1)] = jnp.where(left, _half_swap(even), odd)

    n_pairs = tc // 2
    relayout_in(0, ops_scr.at[0])

    def body(q, carry):
        par = q % 2
        relayout_out(jnp.maximum(q - 1, 0), y_scr.at[1 - par])
        steps(ops_scr.at[par], y_scr.at[par])
        relayout_in(jnp.minimum(q + 1, n_pairs - 1), ops_scr.at[1 - par])
        return carry

    lax.fori_loop(0, n_pairs, body, 0)
    relayout_out(n_pairs - 1, y_scr.at[(n_pairs - 1) % 2])

    @pl.when(c % period == period - 1)
    def _():
        sf_reset_ref[...] = s_scr[...]

    @pl.when(c == pl.num_programs(0) - 1)
    def _():
        sf_keep_ref[...] = s_scr[...]


def _rw_scan(keep_srcs, reset_srcs, s0, t_keep, t_reset, reverse):
    tc, nb = SCAN_TC, SCAN_NB
    n_keep = len(keep_srcs)
    n_src = n_keep + len(reset_srcs)
    nc = t_keep // tc
    period = t_reset // tc
    n_phase = nc // period

    def idx(s, c):
        if s < n_keep:
            return 0, (nc - 1 - c if reverse[s] else c)
        cc = c % period
        return c // period, (period - 1 - cc if reverse[s] else cc)

    y_specs = tuple(pl.BlockSpec((tc, nb, RW_DIM), lambda c, s=s: (idx(s, c)[1], idx(s, c)[0], 0))
                    for s in range(n_src))
    in_specs, args = [], []
    for s, src in enumerate(list(keep_srcs) + list(reset_srcs)):
        in_specs += [y_specs[s]] * _SCAN_OPS
        args += list(src)
    st = pl.BlockSpec((HEAD_DIM, HEAD_DIM, LANES), lambda c: (0, 0, 0))
    y_shapes = tuple(jax.ShapeDtypeStruct((t_keep, nb, RW_DIM) if s < n_keep else (t_reset, nb * n_phase, RW_DIM), F32)
                     for s in range(n_src))
    out = pl.pallas_call(
        functools.partial(_rw_scan_kernel, n_src=n_src, n_keep=n_keep, period=period, reverse=tuple(reverse)),
        out_shape=y_shapes + (jax.ShapeDtypeStruct((HEAD_DIM, HEAD_DIM, LANES), F32),
                              jax.ShapeDtypeStruct((n_phase, HEAD_DIM, HEAD_DIM, LANES), F32)),
        grid=(nc,),
        in_specs=in_specs + [st],
        out_specs=y_specs + (st, pl.BlockSpec((None, HEAD_DIM, HEAD_DIM, LANES), lambda c: (c // period, 0, 0, 0))),
        scratch_shapes=[pltpu.VMEM((2, _SCAN_OPS, 2, HEAD_DIM, LANES), F32),
                        pltpu.VMEM((2, 2, HEAD_DIM, LANES), F32),
                        pltpu.VMEM((HEAD_DIM, HEAD_DIM, LANES), F32)],
        compiler_params=_params("arbitrary"),
        name="rwkv_scan",
    )(*args, s0)
    return out[:n_src], out[n_src], out[n_src + 1]


def _rw_post_kernel(yf_ref, yb_ref, bonus_ref, g_ref, gw_ref, gb_ref, ones_ref, o_ref, stage_scr, *, nb):
    tt = SCAN_TC
    inv = 1.0 / HEAD_DIM
    y = (yf_ref[...] + yb_ref[...]).reshape(tt * nb, RW_DIM)
    mu = _head_sum(y, ones_ref) * inv
    d = y - mu
    var = _head_sum(d * d, ones_ref) * inv
    yn = d * lax.rsqrt(var + GN_EPS) * gw_ref[...] + gb_ref[...]
    out = (yn + bonus_ref[...].reshape(tt * nb, RW_DIM)) * g_ref[...].reshape(tt * nb, RW_DIM)
    stage_scr[...] = out.reshape(tt, nb, RW_DIM)

    def body(b, carry):
        o_ref[b] = stage_scr[:, b, :]
        return carry

    lax.fori_loop(0, nb, body, 0)


def _rw_post(y_f, y_b, bonus, g, gn_w, gn_b, ones, nb, t):
    tt = SCAN_TC
    tmaj = pl.BlockSpec((tt, nb, RW_DIM), lambda c: (c, 0, 0))
    vec = pl.BlockSpec((1, RW_DIM), lambda c: (0, 0))
    out = pl.pallas_call(
        functools.partial(_rw_post_kernel, nb=nb),
        out_shape=jax.ShapeDtypeStruct((nb, t, RW_DIM), F32),
        grid=(t // tt,),
        in_specs=[tmaj, tmaj, tmaj, tmaj, vec, vec, pl.BlockSpec((RW_DIM, RW_DIM), lambda c: (0, 0))],
        out_specs=pl.BlockSpec((nb, tt, RW_DIM), lambda c: (0, c, 0)),
        scratch_shapes=[pltpu.VMEM((tt, nb, RW_DIM), F32)],
        compiler_params=_params("parallel"),
        name="rwkv_post",
    )(y_f, y_b, bonus, g, gn_w.reshape(1, RW_DIM), gn_b.reshape(1, RW_DIM), ones)
    return out.reshape(nb * t, RW_DIM)


def _scan_state(states):
    z = jnp.concatenate([s.transpose(2, 3, 1, 0).reshape(HEAD_DIM, HEAD_DIM, -1) for s in states], axis=-1)
    return jnp.pad(z, ((0, 0), (0, 0), (0, LANES - z.shape[-1])))


def _unscan_reset_state(sf, src):
    n = RW_HEADS * SCAN_NB
    z = sf[:, :, :, src * n:(src + 1) * n].reshape(-1, HEAD_DIM, HEAD_DIM, RW_HEADS, SCAN_NB)
    return z.transpose(0, 4, 3, 1, 2).reshape(-1, RW_HEADS, HEAD_DIM, HEAD_DIM)


def _rwkv_operands(h, seq0, nb, t, *args):
    dw_f, dw_b, kd_f, kd_b, b_f, b_b, kk, r, v, bonus, g = _rw_prep(h.reshape(-1, t, D_MODEL), seq0, nb, t, *args)
    return (dw_f, kk, b_f, kd_f, r, v), (dw_b, kk, b_b, kd_b, r, v), bonus, g


def _rwkv(h, w_rw, lora_w, lora_b, g2, lp, ones, s0_f, s0_b):
    assert DEC_BATCH == SCAN_NB and BATCH * SEQ == DEC_BATCH * DEC_SEQ
    args = (w_rw, lora_w, lora_b, g2, lp, ones)
    c_f, c_b, c_bonus, c_g = _rwkv_operands(h, 0, BATCH, SEQ, *args)
    l_f, l_b, l_bonus, l_g = _rwkv_operands(h, N_CTX // DEC_SEQ, DEC_BATCH, DEC_SEQ, *args)
    (yl_f, yl_b, yc_f, yc_b), _, sf_ctx = _rw_scan([l_f, l_b], [c_f, c_b], _scan_state([s0_f, s0_b]),
                                                   DEC_SEQ, SEQ, [False, True, False, True])
    post = (lp["gn_w"], lp["gn_b"], ones)
    rw_ctx = _rw_post(yc_f, yc_b, c_bonus, c_g, *post, BATCH, SEQ)
    rw_lat = _rw_post(yl_f, yl_b, l_bonus, l_g, *post, DEC_BATCH, DEC_SEQ)
    return rw_ctx, rw_lat, _unscan_reset_state(sf_ctx, 2), _unscan_reset_state(sf_ctx, 3)


_TM_MERGE = 512


def _merge_kernel(x_ref, gm_ref, h_ref, nac_ref, nal_ref, rwc_ref, rwl_ref, mlc_ref, mll_ref, wg_ref, wbr_ref, wo_ref,
                  o_ref):
    is_ctx = pl.program_id(0) < N_CTX // _TM_MERGE
    h = h_ref[...]
    branches = tuple(jnp.where(is_ctx, c_ref[...], l_ref[...])
                     for c_ref, l_ref in ((nac_ref, nal_ref), (rwc_ref, rwl_ref), (mlc_ref, mll_ref)))
    m = None
    for i, o_b in enumerate(branches):
        gate = jnp.dot(h, wg_ref[:, i * D_MODEL:(i + 1) * D_MODEL], preferred_element_type=F32)
        br = jnp.dot(o_b.astype(BF16), wbr_ref[i], preferred_element_type=F32)
        t = _sigmoid(gate) * br
        m = t if m is None else m + t
    mix = jnp.dot(m.astype(BF16), wo_ref[...], preferred_element_type=F32)
    o_ref[...] = x_ref[...] + gm_ref[...] * mix


def _merge(x, mod_l, h, na_ctx, na_lat, rw_ctx, rw_lat, mla_ctx, mla_lat, w_gate, w_br, w_o):
    tm = _TM_MERGE
    n_ctx = N_CTX // tm
    row = pl.BlockSpec((tm, D_MODEL), lambda i: (i, 0))
    br_ctx = pl.BlockSpec((tm, BRANCH_DIM), lambda i: (jnp.minimum(i, n_ctx - 1), 0))
    br_lat = pl.BlockSpec((tm, BRANCH_DIM), lambda i: (jnp.maximum(i - n_ctx, 0), 0))
    return pl.pallas_call(
        _merge_kernel,
        out_shape=jax.ShapeDtypeStruct((N_TOK, D_MODEL), F32),
        grid=(N_TOK // tm,),
        in_specs=[row, pl.BlockSpec((None, 1, D_MODEL), lambda i: (_cond_row(i, tm), 0, 5)),
                  row, br_ctx, br_lat, br_ctx, br_lat, br_ctx, br_lat,
                  pl.BlockSpec((D_MODEL, N_BRANCH * D_MODEL), lambda i: (0, 0)),
                  pl.BlockSpec((N_BRANCH, BRANCH_DIM, D_MODEL), lambda i: (0, 0, 0)),
                  pl.BlockSpec((D_MODEL, D_MODEL), lambda i: (0, 0))],
        out_specs=row,
        compiler_params=_params("arbitrary"),
        name="merge",
    )(x, mod_l, h, na_ctx, na_lat, rw_ctx, rw_lat, mla_ctx, mla_lat, w_gate, w_br, w_o)


def _block_diag(blocks):
    rows = sum(b.shape[0] for b in blocks)
    cols = sum(b.shape[1] for b in blocks)
    out = jnp.zeros((rows, cols), blocks[0].dtype)
    r = c = 0
    for b in blocks:
        out = lax.dynamic_update_slice(out, b, (r, c))
        r += b.shape[0]
        c += b.shape[1]
    return out


def _pad_cols(w, left, total):
    return jnp.pad(w, ((0, 0), (left, total - left - w.shape[1])))


def _mla_uq_layout(w_uq):
    per = NOPE_DIM + ROPE_DIM
    return jnp.concatenate([_pad_cols(w_uq[:, h * per:(h + 1) * per], 0, LANES) for h in range(MLA_HEADS)], axis=1)


def _mla_ukv_layout(w_ukv):
    per = NOPE_DIM + V_DIM
    k = [_pad_cols(w_ukv[:, h * per:h * per + NOPE_DIM], 0, LANES) for h in range(MLA_HEADS)]
    v = [w_ukv[:, h * per + NOPE_DIM:(h + 1) * per] for h in range(MLA_HEADS)]
    return jnp.concatenate(k + v, axis=1)


def _heads_on_lanes(cache):
    b, h, t, d = cache.shape
    return cache.transpose(0, 2, 1, 3).reshape(b, t, h * d)


def _heads_major(z, nb, t, nh):
    return z.reshape(nb, t, nh, -1).transpose(0, 2, 1, 3)


def kernel(x_prompt, x_sample, c, cache_na_k, cache_na_v, cache_mla_ckv, cache_mla_krope, state_rwkv_fwd, state_rwkv_bwd, c_ctx, ada_w, ada_b, norm_g, ffn_wg, ffn_wu, ffn_wd, w_in, na_rpb, rw_w0, rw_w2, rw_a0, rw_a2, rw_g2, rw_kk, rw_ka, rw_rk, rw_gn_w, rw_gn_b, mla_qn_g, mla_kvn_g, mla_w_uq, mla_w_ukv, w_br, w_o, final_g):
    x = (x_prompt.reshape(N_CTX, D_MODEL), x_sample.reshape(N_LAT, D_MODEL))
    cond = jnp.concatenate([c_ctx[None, :], c, jnp.zeros((N_COND - 1 - DEC_BATCH, D_MODEL), F32)], axis=0)
    mod = _modulation(cond, ada_w, ada_b)

    ones = jnp.asarray(np.kron(np.eye(RW_HEADS), np.ones((HEAD_DIM, HEAD_DIM))), BF16)
    rope = _rope_tables(DEC_SEQ)

    col_rw = 3 * NA_HEADS * HEAD_DIM
    col_mla = col_rw + 3 * RW_DIM + 2 * DECAY_LORA + 2 * ICLR_LORA + GATE_LORA
    col_kr = col_mla + Q_LORA + KV_LORA
    col_gate = col_kr + ROPE_DIM

    caches = []
    y = None
    for l in range(DEPTH):
        mod_l = mod[l]
        wi = w_in[l].astype(BF16)
        w_na = wi[:, :col_rw]
        w_rw = wi[:, col_rw:col_mla]
        w_mla = jnp.concatenate([wi[:, col_mla:col_kr], _pad_cols(wi[:, col_kr:col_gate], _ROPE_LO, LANES)], axis=1)
        w_gate = wi[:, col_gate:]

        x, h = _ffn(x, norm_g[l, 0], mod_l, 0, ffn_wg[l, 0].astype(BF16), ffn_wu[l, 0].astype(BF16),
                    ffn_wd[l, 0].astype(BF16), post="mod", post_g=norm_g[l, 1], post_chunk=3)

        z_na = _mm(h, w_na, name="proj_na")
        na_ctx = _na_ctx(z_na)
        na_lat = _na_latent(na_rpb[l].reshape(NA_HEADS * (2 * NA_WIN_H - 1), 2 * NA_WIN_W - 1), z_na,
                            _heads_on_lanes(cache_na_k[:, l]), _heads_on_lanes(cache_na_v[:, l]))

        lora_w = _block_diag([rw_w2[l, 0], rw_w2[l, 1], rw_a2[l, 0], rw_a2[l, 1]]).astype(BF16)
        lora_b = jnp.concatenate([rw_w0[l, 0], rw_w0[l, 1], rw_a0[l, 0], rw_a0[l, 1]])
        lp = {"kk": rw_kk[l], "ka": rw_ka[l], "rk": rw_rk[l].reshape(RW_DIM), "gn_w": rw_gn_w[l], "gn_b": rw_gn_b[l]}
        rw_ctx, rw_lat, sc_f, sc_b = _rwkv(h, w_rw, lora_w, lora_b, rw_g2[l].astype(BF16), lp, ones,
                                           state_rwkv_fwd[:, l], state_rwkv_bwd[:, l])

        w_ukv = _mla_ukv_layout(mla_w_ukv[l]).astype(BF16)
        q_m, kv_m, ckv, kr_m = _mla_prep(h, w_mla, mla_qn_g[l], mla_kvn_g[l],
                                         _mla_uq_layout(mla_w_uq[l]).astype(BF16), w_ukv)
        kv_cache = _mm(cache_mla_ckv[:, l].reshape(DEC_BATCH * PAST_LEN, KV_LORA), w_ukv, "mla_ukv_cache")
        kr_cache = _pad_cols(cache_mla_krope[:, l].reshape(DEC_BATCH * PAST_LEN, ROPE_DIM), _ROPE_LO, LANES)
        mla_ctx = _mla_ctx(q_m, kv_m, kr_m)
        mla_lat = _mla_latent(q_m, kv_m, kr_m, kv_cache, kr_cache, rope)

        x = _merge(x, mod_l, h, na_ctx, na_lat, rw_ctx, rw_lat, mla_ctx, mla_lat, w_gate,
                   w_br[l].astype(BF16), w_o[l].astype(BF16))

        ffn2 = (x, norm_g[l, 2], mod_l, 6, ffn_wg[l, 1].astype(BF16), ffn_wu[l, 1].astype(BF16),
                ffn_wd[l, 1].astype(BF16))
        if l == DEPTH - 1:
            y = _ffn(*ffn2, post="plain", post_g=final_g)
        else:
            x = _ffn(*ffn2)

        w = NA_HEADS * HEAD_DIM
        caches.append((_heads_major(z_na[:N_CTX, w:2 * w], BATCH, SEQ, NA_HEADS),
                       _heads_major(z_na[:N_CTX, 2 * w:3 * w], BATCH, SEQ, NA_HEADS),
                       ckv[:N_CTX].reshape(BATCH, SEQ, KV_LORA),
                       kr_m[:N_CTX, _ROPE_LO:_ROPE_LO + ROPE_DIM].reshape(BATCH, SEQ, ROPE_DIM),
                       sc_f, sc_b))

    y_prompt = y[0].reshape(BATCH, SEQ, D_MODEL)
    y_sample = y[1].reshape(DEC_BATCH, DEC_SEQ, D_MODEL)
    outs = [jnp.stack([cl[i] for cl in caches], axis=1) for i in range(6)]
    return (y_prompt, y_sample, *outs)
```

```python
import functools

import numpy as np
import jax
import jax.numpy as jnp
from jax import lax
from jax.experimental import pallas as pl
from jax.experimental.pallas import tpu as pltpu

F32 = jnp.float32
BF16 = jnp.bfloat16

D_MODEL = 1024
BATCH = 32
SEQ = 256
DEPTH = 2
DEC_BATCH = 8
DEC_SEQ = 1024
PAST_LEN = 256
GRID_W = 64
HEAD_DIM = 64
NA_HEADS = 4
NA_WIN_H = 8
NA_WIN_W = 16
RW_HEADS = 4
RW_DIM = RW_HEADS * HEAD_DIM
DECAY_LORA = 64
ICLR_LORA = 64
GATE_LORA = 128
GN_EPS = 64e-5
MLA_HEADS = 4
Q_LORA = 256
KV_LORA = 128
NOPE_DIM = 64
ROPE_DIM = 32
V_DIM = 64
MLA_SCALE = (NOPE_DIM + ROPE_DIM) ** -0.5
ROPE_BASE = 10000.0
N_BRANCH = 3
BRANCH_DIM = 256
D_FF = 2816
N_MOD = 9
NORM_EPS = 1e-6
NEG_INF = -1e30

N_CTX = BATCH * SEQ
N_LAT = DEC_BATCH * DEC_SEQ
N_TOK = N_CTX + N_LAT
N_COND = 16

LANES = 128
VMEM_LIMIT = 56 * 1024 * 1024

TM = 1024
TM_FFN = 512
TF = D_FF // 2
SCAN_TC = 64


def _cond_row(i, tm):
    return jnp.where(i * tm < N_CTX, 0, 1 + (i * tm - N_CTX) // DEC_SEQ)


def _sigmoid(x):
    return 1.0 / (1.0 + jnp.exp(-x))


def _params(*sem):
    return pltpu.CompilerParams(dimension_semantics=sem, vmem_limit_bytes=VMEM_LIMIT)


def _mm_kernel(a_ref, w_ref, o_ref):
    o_ref[...] = jnp.dot(a_ref[...].astype(BF16), w_ref[...], preferred_element_type=F32)


def _mm(a, w, name):
    m, k = a.shape
    n = w.shape[1]
    tm = min(m, TM)
    assert m % tm == 0 and w.shape[0] == k
    return pl.pallas_call(
        _mm_kernel,
        out_shape=jax.ShapeDtypeStruct((m, n), F32),
        grid=(m // tm,),
        in_specs=[pl.BlockSpec((tm, k), lambda i: (i, 0)), pl.BlockSpec((k, n), lambda i: (0, 0))],
        out_specs=pl.BlockSpec((tm, n), lambda i: (i, 0)),
        compiler_params=_params("parallel"),
        name=name,
    )(a, w)


def _mod_kernel(c_ref, w_ref, b_ref, o_ref):
    c = c_ref[...]
    a = (c * _sigmoid(c)).astype(BF16)
    o_ref[...] = jnp.dot(a, w_ref[...].astype(BF16), preferred_element_type=F32) + b_ref[...]


def _modulation(cond, ada_w, ada_b):
    n = N_MOD * D_MODEL
    tn = D_MODEL
    out = pl.pallas_call(
        _mod_kernel,
        out_shape=jax.ShapeDtypeStruct((DEPTH, N_COND, n), F32),
        grid=(DEPTH, n // tn),
        in_specs=[pl.BlockSpec((N_COND, D_MODEL), lambda l, j: (0, 0)),
                  pl.BlockSpec((None, D_MODEL, tn), lambda l, j: (l, 0, j)),
                  pl.BlockSpec((None, 1, tn), lambda l, j: (l, 0, j))],
        out_specs=pl.BlockSpec((None, N_COND, tn), lambda l, j: (l, 0, j)),
        compiler_params=_params("parallel", "arbitrary"),
        name="modulation",
    )(cond, ada_w, ada_b.reshape(DEPTH, 1, n))
    return out.reshape(DEPTH, N_COND, 1, n)


def _rms(x, g):
    return x * lax.rsqrt(jnp.mean(x * x, axis=-1, keepdims=True) + NORM_EPS) * g


_FFN_CTX_TILES = N_CTX // TM_FFN


def _ffn_kernel(*refs, post, split_in):
    n_x = 2 if split_in else 1
    x_refs = refs[:n_x]
    g_ref, sh_ref, sc_ref, gt_ref, wg_ref, wu_ref, wd_ref = refs[n_x:n_x + 7]
    rest = refs[n_x + 7:]
    if post == "mod":
        g2_ref, sh2_ref, sc2_ref, o_ref, h2_ref, h_scr, acc_scr = rest
    elif post == "plain":
        g2_ref, ya_ref, yb_ref, h_scr, acc_scr = rest
    else:
        o_ref, h_scr, acc_scr = rest
    f = pl.program_id(1)
    is_ctx = pl.program_id(0) < _FFN_CTX_TILES

    def load_x():
        if split_in:
            return jnp.where(is_ctx, x_refs[0][...], x_refs[1][...])
        return x_refs[0][...]

    @pl.when(f == 0)
    def _():
        y = _rms(load_x(), g_ref[...])
        h_scr[...] = (y * (1.0 + sc_ref[...]) + sh_ref[...]).astype(BF16)
        acc_scr[...] = jnp.zeros_like(acc_scr)

    h = h_scr[...]
    gg = jnp.dot(h, wg_ref[...], preferred_element_type=F32)
    uu = jnp.dot(h, wu_ref[...], preferred_element_type=F32)
    a = (gg * _sigmoid(gg)) * uu
    acc_scr[...] += jnp.dot(a.astype(BF16), wd_ref[...], preferred_element_type=F32)

    @pl.when(f == pl.num_programs(1) - 1)
    def _():
        xn = load_x() + 0.5 * gt_ref[...] * acc_scr[...]
        if post == "plain":
            y = _rms(xn, g2_ref[...])

            @pl.when(is_ctx)
            def _():
                ya_ref[...] = y

            @pl.when(jnp.logical_not(is_ctx))
            def _():
                yb_ref[...] = y
        else:
            o_ref[...] = xn
        if post == "mod":
            h2_ref[...] = (_rms(xn, g2_ref[...]) * (1.0 + sc2_ref[...]) + sh2_ref[...]).astype(BF16)


def _ffn(x, norm_g, mod_l, chunk0, wg, wu, wd, post=None, post_g=None, post_chunk=None):
    def mod_spec(c):
        return pl.BlockSpec((None, 1, D_MODEL), lambda i, f: (_cond_row(i, TM_FFN), 0, c))

    row = pl.BlockSpec((TM_FFN, D_MODEL), lambda i, f: (i, 0))
    row_ctx = pl.BlockSpec((TM_FFN, D_MODEL), lambda i, f: (jnp.minimum(i, _FFN_CTX_TILES - 1), 0))
    row_lat = pl.BlockSpec((TM_FFN, D_MODEL), lambda i, f: (jnp.maximum(i - _FFN_CTX_TILES, 0), 0))
    vec = pl.BlockSpec((1, D_MODEL), lambda i, f: (0, 0))
    split_in = isinstance(x, tuple)
    in_specs = ([row_ctx, row_lat] if split_in else [row]) + [
        vec, mod_spec(chunk0), mod_spec(chunk0 + 1), mod_spec(chunk0 + 2),
        pl.BlockSpec((D_MODEL, TF), lambda i, f: (0, f)),
        pl.BlockSpec((D_MODEL, TF), lambda i, f: (0, f)),
        pl.BlockSpec((TF, D_MODEL), lambda i, f: (f, 0))]
    args = (list(x) if split_in else [x]) + [norm_g.reshape(1, D_MODEL), mod_l, mod_l, mod_l, wg, wu, wd]
    xs = jax.ShapeDtypeStruct((N_TOK, D_MODEL), F32)
    if post == "mod":
        in_specs += [vec, mod_spec(post_chunk), mod_spec(post_chunk + 1)]
        args += [post_g.reshape(1, D_MODEL), mod_l, mod_l]
        out_shape, out_specs = (xs, jax.ShapeDtypeStruct((N_TOK, D_MODEL), BF16)), (row, row)
    elif post == "plain":
        in_specs += [vec]
        args += [post_g.reshape(1, D_MODEL)]
        out_shape = (jax.ShapeDtypeStruct((N_CTX, D_MODEL), F32), jax.ShapeDtypeStruct((N_LAT, D_MODEL), F32))
        out_specs = (row_ctx, row_lat)
    else:
        out_shape, out_specs = xs, row
    return pl.pallas_call(
        functools.partial(_ffn_kernel, post=post, split_in=split_in),
        out_shape=out_shape,
        grid=(N_TOK // TM_FFN, D_FF // TF),
        in_specs=in_specs,
        out_specs=out_specs,
        scratch_shapes=[pltpu.VMEM((TM_FFN, D_MODEL), BF16), pltpu.VMEM((TM_FFN, D_MODEL), F32)],
        compiler_params=_params("arbitrary", "arbitrary"),
        name="ffn",
    )(*args)


_NT = (((1,), (1,)), ((), ()))
_LAT_ROWS = DEC_SEQ // GRID_W
_NA_KH = min(NA_WIN_H, _LAT_ROWS)
_NA_LOCAL = _NA_KH * GRID_W
_NA_PAIRS = 2 * NA_WIN_H - 2


def _head_mask(width, h, group):
    lane = lax.broadcasted_iota(jnp.int32, (1, width), 1)
    return (lane >= h * group) & (lane < (h + 1) * group)


def _stack_heads(q, nh, group):
    return jnp.concatenate([jnp.where(_head_mask(q.shape[1], h, group), q, 0.0) for h in range(nh)], axis=0)


def _pick_heads(r, nh, group):
    t = r.shape[0] // nh
    out = r[0:t]
    for h in range(1, nh):
        out = jnp.where(_head_mask(r.shape[1], h, group), r[h * t:(h + 1) * t], out)
    return out


def _softmax2(s_a, s_b):
    m = jnp.maximum(jnp.max(s_a, axis=-1, keepdims=True), jnp.max(s_b, axis=-1, keepdims=True))
    p_a = jnp.exp(s_a - m)
    p_b = jnp.exp(s_b - m)
    return p_a, p_b, jnp.sum(p_a, axis=-1, keepdims=True) + jnp.sum(p_b, axis=-1, keepdims=True)


def _na_ctx_kernel(q_ref, k_ref, v_ref, *rest):
    o_ref, ko_ref, vo_ref = rest[-3:]
    k = k_ref[...]
    v = v_ref[...]
    q_st = _stack_heads(q_ref[...] * HEAD_DIM ** -0.5, NA_HEADS, HEAD_DIM).astype(BF16)
    s = lax.dot_general(q_st, k.astype(BF16), _NT, preferred_element_type=F32)
    p = jnp.exp(s - jnp.max(s, axis=-1, keepdims=True))
    l = jnp.sum(p, axis=-1, keepdims=True)
    r = jnp.dot(p.astype(BF16), v.astype(BF16), preferred_element_type=F32) / l
    o_ref[...] = _pick_heads(r, NA_HEADS, HEAD_DIM)
    for h in range(NA_HEADS):
        ko_ref[h] = k[:, h * HEAD_DIM:(h + 1) * HEAD_DIM]
        vo_ref[h] = v[:, h * HEAD_DIM:(h + 1) * HEAD_DIM]


def _na_ctx(z_na, layer, caches):
    w = NA_HEADS * HEAD_DIM
    cache_shape = jax.ShapeDtypeStruct((BATCH, DEPTH, NA_HEADS, SEQ, HEAD_DIM), F32)
    cache_spec = pl.BlockSpec((None, None, NA_HEADS, SEQ, HEAD_DIM), lambda b: (b, layer, 0, 0, 0))
    in_specs = [pl.BlockSpec((SEQ, w), lambda b: (b, 0)),
                pl.BlockSpec((SEQ, w), lambda b: (b, 1)),
                pl.BlockSpec((SEQ, w), lambda b: (b, 2))]
    args = [z_na, z_na, z_na]
    aliases = {}
    if caches is not None:
        in_specs += [pl.BlockSpec(memory_space=pl.ANY)] * 2
        args += list(caches)
        aliases = {3: 1, 4: 2}
    return pl.pallas_call(
        _na_ctx_kernel,
        out_shape=(jax.ShapeDtypeStruct((N_CTX, w), F32), cache_shape, cache_shape),
        grid=(BATCH,),
        in_specs=in_specs,
        out_specs=(pl.BlockSpec((SEQ, w), lambda b: (b, 0)), cache_spec, cache_spec),
        input_output_aliases=aliases,
        compiler_params=_params("parallel"),
        name="na_ctx",
    )(*args)


def _na_row_lo(qr):
    return min(max(qr - NA_WIN_H // 2, 0), _LAT_ROWS - _NA_KH)


def _na_lat_kernel(rpb_ref, q_ref, k_ref, v_ref, kc_ref, vc_ref, o_ref, bias_scr):
    scale = HEAD_DIM ** -0.5

    @pl.when(pl.program_id(0) == 0)
    def _():
        qc = lax.broadcasted_iota(jnp.int32, (GRID_W, 2 * GRID_W), 0)
        lane = lax.broadcasted_iota(jnp.int32, (GRID_W, 2 * GRID_W), 1)
        kc = lane & (GRID_W - 1)
        col_lo = jnp.clip(qc - NA_WIN_W // 2, 0, GRID_W - NA_WIN_W)
        rel = jnp.where((kc >= col_lo) & (kc < col_lo + NA_WIN_W), kc - qc + NA_WIN_W - 1, -1)
        second = lax.broadcasted_iota(jnp.int32, (1, 2 * GRID_W), 1) >= GRID_W

        def build(idx, carry):
            h = idx // _NA_PAIRS
            row = h * (2 * NA_WIN_H - 1) + (idx - h * _NA_PAIRS)
            tile = jnp.full((GRID_W, 2 * GRID_W), NEG_INF, F32)
            for j in range(2 * NA_WIN_W - 1):
                val = jnp.where(second, rpb_ref[row + 1, j], rpb_ref[row, j])
                tile = jnp.where(rel == j, val, tile)
            bias_scr[idx] = tile
            return carry

        lax.fori_loop(0, NA_HEADS * _NA_PAIRS, build, 0)

    kc = jnp.concatenate([kc_ref[h] for h in range(NA_HEADS)], axis=1).astype(BF16)
    vc = jnp.concatenate([vc_ref[h] for h in range(NA_HEADS)], axis=1).astype(BF16)
    for qr in range(_LAT_ROWS):
        lo = _na_row_lo(qr)
        dr0 = lo - qr + NA_WIN_H - 1
        q_st = _stack_heads(q_ref[qr * GRID_W:(qr + 1) * GRID_W, :] * scale, NA_HEADS, HEAD_DIM).astype(BF16)
        kl = k_ref[lo * GRID_W:lo * GRID_W + _NA_LOCAL, :].astype(BF16)
        vl = v_ref[lo * GRID_W:lo * GRID_W + _NA_LOCAL, :].astype(BF16)
        bias = jnp.concatenate(
            [jnp.concatenate([bias_scr[h * _NA_PAIRS + dr0 + 2 * i] for i in range(_NA_KH // 2)], axis=1)
             for h in range(NA_HEADS)], axis=0)
        s_loc = lax.dot_general(q_st, kl, _NT, preferred_element_type=F32) + bias
        s_ctx = lax.dot_general(q_st, kc, _NT, preferred_element_type=F32)
        p_loc, p_ctx, l = _softmax2(s_loc, s_ctx)
        r = (jnp.dot(p_loc.astype(BF16), vl, preferred_element_type=F32)
             + jnp.dot(p_ctx.astype(BF16), vc, preferred_element_type=F32)) / l
        o_ref[qr * GRID_W:(qr + 1) * GRID_W, :] = _pick_heads(r, NA_HEADS, HEAD_DIM)


def _na_latent(rpb, z_na, kc, vc, layer):
    w = NA_HEADS * HEAD_DIM
    lat0 = N_CTX // DEC_SEQ

    def col(c):
        return pl.BlockSpec((DEC_SEQ, w), lambda b: (lat0 + b, c))

    cache = pl.BlockSpec((None, None, NA_HEADS, PAST_LEN, HEAD_DIM), lambda b: (b, layer, 0, 0, 0))
    return pl.pallas_call(
        _na_lat_kernel,
        out_shape=jax.ShapeDtypeStruct((N_LAT, w), F32),
        grid=(DEC_BATCH,),
        in_specs=[pl.BlockSpec(memory_space=pltpu.SMEM), col(0), col(1), col(2), cache, cache],
        out_specs=pl.BlockSpec((DEC_SEQ, w), lambda b: (b, 0)),
        scratch_shapes=[pltpu.VMEM((NA_HEADS * _NA_PAIRS, GRID_W, 2 * GRID_W), F32)],
        compiler_params=_params("arbitrary"),
        name="na_latent",
    )(rpb, z_na, z_na, z_na, kc, vc)


_MLA_W = MLA_HEADS * LANES
_ROPE_LO = NOPE_DIM
_ROPE_HALF = ROPE_DIM // 2


def _rope_tables(t_len):
    n_freq = ROPE_DIM // 4
    inv = 1.0 / (ROPE_BASE ** (np.arange(n_freq) / n_freq))
    pos = np.arange(t_len)
    ang = np.concatenate([(pos // GRID_W)[:, None] * inv, (pos % GRID_W)[:, None] * inv], axis=-1)
    cos, sin = np.cos(ang), np.sin(ang)
    c = np.ones((t_len, LANES))
    s1 = np.zeros((t_len, LANES))
    s2 = np.zeros((t_len, LANES))
    a, b, e = _ROPE_LO, _ROPE_LO + _ROPE_HALF, _ROPE_LO + ROPE_DIM
    c[:, a:b] = cos
    c[:, b:e] = cos
    s1[:, a:b] = -sin
    s2[:, b:e] = sin
    return tuple(jnp.asarray(t, F32) for t in (c, s1, s2))


def _apply_rope(x, c, s1, s2):
    up = pltpu.roll(x, LANES - _ROPE_HALF, 1)
    dn = pltpu.roll(x, _ROPE_HALF, 1)
    return x * c + up * s1 + dn * s2


def _mla_prep_kernel(h_ref, w_ref, qg_ref, kvg_ref, wuq_ref, wukv_ref, q_ref, kv_ref, ckv_ref, kr_ref):
    z = jnp.dot(h_ref[...], w_ref[...], preferred_element_type=F32)
    cq = _rms(z[:, :Q_LORA], qg_ref[...])
    q_ref[...] = jnp.dot(cq.astype(BF16), wuq_ref[...], preferred_element_type=F32)
    ckv = _rms(z[:, Q_LORA:Q_LORA + KV_LORA], kvg_ref[...])
    ckv_ref[...] = ckv
    kv_ref[...] = jnp.dot(ckv.astype(BF16), wukv_ref[...], preferred_element_type=F32)
    kr_ref[...] = z[:, Q_LORA + KV_LORA:]


def _mla_prep(h, w_mla, qn_g, kvn_g, w_uq, w_ukv):
    wv = MLA_HEADS * V_DIM
    row = lambda n: pl.BlockSpec((TM, n), lambda i: (i, 0))
    full = lambda a: pl.BlockSpec(a.shape, lambda i: (0, 0))
    qn_g = qn_g.reshape(1, Q_LORA)
    kvn_g = kvn_g.reshape(1, KV_LORA)
    return pl.pallas_call(
        _mla_prep_kernel,
        out_shape=(jax.ShapeDtypeStruct((N_TOK, _MLA_W), F32), jax.ShapeDtypeStruct((N_TOK, _MLA_W + wv), F32),
                   jax.ShapeDtypeStruct((N_TOK, KV_LORA), F32), jax.ShapeDtypeStruct((N_TOK, LANES), F32)),
        grid=(N_TOK // TM,),
        in_specs=[row(D_MODEL), full(w_mla), full(qn_g), full(kvn_g), full(w_uq), full(w_ukv)],
        out_specs=(row(_MLA_W), row(_MLA_W + wv), row(KV_LORA), row(LANES)),
        compiler_params=_params("parallel"),
        name="mla_prep",
    )(h, w_mla, qn_g, kvn_g, w_uq, w_ukv)


def _mla_ctx_kernel(q_ref, kv_ref, kr_ref, o_ref):
    kr = kr_ref[...]
    v_all = kv_ref[:, _MLA_W:].astype(BF16)
    out = None
    for h in range(MLA_HEADS):
        qh = (q_ref[:, h * LANES:(h + 1) * LANES] * MLA_SCALE).astype(BF16)
        kh = (kv_ref[:, h * LANES:(h + 1) * LANES] + kr).astype(BF16)
        s = lax.dot_general(qh, kh, _NT, preferred_element_type=F32)
        p = jnp.exp(s - jnp.max(s, axis=-1, keepdims=True))
        l = jnp.sum(p, axis=-1, keepdims=True)
        r = jnp.dot(p.astype(BF16), v_all, preferred_element_type=F32) / l
        out = r if out is None else jnp.where(_head_mask(MLA_HEADS * V_DIM, h, V_DIM), r, out)
    o_ref[...] = out


def _mla_ctx(q_m, kv_m, kr_m):
    wv = MLA_HEADS * V_DIM
    return pl.pallas_call(
        _mla_ctx_kernel,
        out_shape=jax.ShapeDtypeStruct((N_CTX, wv), F32),
        grid=(BATCH,),
        in_specs=[pl.BlockSpec((SEQ, _MLA_W), lambda b: (b, 0)),
                  pl.BlockSpec((SEQ, _MLA_W + wv), lambda b: (b, 0)),
                  pl.BlockSpec((SEQ, LANES), lambda b: (b, 0))],
        out_specs=pl.BlockSpec((SEQ, wv), lambda b: (b, 0)),
        compiler_params=_params("parallel"),
        name="mla_ctx",
    )(q_m, kv_m, kr_m)


_MLA_BQ = 256


def _mla_lat_kernel(q_ref, kv_ref, kr_ref, kvc_ref, krc_ref, cq_ref, s1q_ref, s2q_ref, ck_ref, s1k_ref, s2k_ref,
                    o_ref):
    kr = _apply_rope(kr_ref[...], ck_ref[...], s1k_ref[...], s2k_ref[...])
    krc = krc_ref[...]
    v_lat = kv_ref[:, _MLA_W:].astype(BF16)
    v_ctx = kvc_ref[:, _MLA_W:].astype(BF16)
    out = None
    for h in range(MLA_HEADS):
        blk = slice(h * LANES, (h + 1) * LANES)
        qh = (_apply_rope(q_ref[:, blk], cq_ref[...], s1q_ref[...], s2q_ref[...]) * MLA_SCALE).astype(BF16)
        kl = (kv_ref[:, blk] + kr).astype(BF16)
        kc = (kvc_ref[:, blk] + krc).astype(BF16)
        s_l = lax.dot_general(qh, kl, _NT, preferred_element_type=F32)
        s_c = lax.dot_general(qh, kc, _NT, preferred_element_type=F32)
        p_l, p_c, l = _softmax2(s_l, s_c)
        r = (jnp.dot(p_l.astype(BF16), v_lat, preferred_element_type=F32)
             + jnp.dot(p_c.astype(BF16), v_ctx, preferred_element_type=F32)) / l
        out = r if out is None else jnp.where(_head_mask(MLA_HEADS * V_DIM, h, V_DIM), r, out)
    o_ref[...] = out


def _mla_latent(q_m, kv_m, kr_m, kv_cache, kr_cache, rope):
    wv = MLA_HEADS * V_DIM
    nq = DEC_SEQ // _MLA_BQ
    q0 = N_CTX // _MLA_BQ
    lat0 = N_CTX // DEC_SEQ
    c, s1, s2 = rope
    tq = pl.BlockSpec((_MLA_BQ, LANES), lambda b, j: (j, 0))
    tk = pl.BlockSpec((DEC_SEQ, LANES), lambda b, j: (0, 0))
    return pl.pallas_call(
        _mla_lat_kernel,
        out_shape=jax.ShapeDtypeStruct((N_LAT, wv), F32),
        grid=(DEC_BATCH, nq),
        in_specs=[pl.BlockSpec((_MLA_BQ, _MLA_W), lambda b, j: (q0 + b * nq + j, 0)),
                  pl.BlockSpec((DEC_SEQ, _MLA_W + wv), lambda b, j: (lat0 + b, 0)),
                  pl.BlockSpec((DEC_SEQ, LANES), lambda b, j: (lat0 + b, 0)),
                  pl.BlockSpec((PAST_LEN, _MLA_W + wv), lambda b, j: (b, 0)),
                  pl.BlockSpec((PAST_LEN, LANES), lambda b, j: (b, 0)),
                  tq, tq, tq, tk, tk, tk],
        out_specs=pl.BlockSpec((_MLA_BQ, wv), lambda b, j: (b * nq + j, 0)),
        compiler_params=_params("parallel", "arbitrary"),
        name="mla_lat",
    )(q_m, kv_m, kr_m, kv_cache, kr_cache, c, s1, s2, c, s1, s2)


def _head_sum(x, ones_ref):
    ones = ones_ref[...]
    hi = x.astype(BF16)
    rest = x - hi.astype(F32)
    mid = rest.astype(BF16)
    lo = (rest - mid.astype(F32)).astype(BF16)
    return (jnp.dot(hi, ones, preferred_element_type=F32) + jnp.dot(mid, ones, preferred_element_type=F32)
            + jnp.dot(lo, ones, preferred_element_type=F32))


SCAN_NB = LANES // (4 * RW_HEADS)
_PREP_ROWS = 512
_PREP_OUTS = 11


def _lora_act(a):
    col = lax.broadcasted_iota(jnp.int32, a.shape, 1)
    return jnp.where(col < 2 * DECAY_LORA, jnp.tanh(a), a)


def _rw_prep_kernel(h_ref, perm_ref, w_ref, lw_ref, lb_ref, g2_ref, kkw_ref, ka_ref, rk_ref, ones_ref, *out_refs,
                    nbb, tt):
    h = jnp.dot(perm_ref[...], h_ref[...].reshape(nbb * tt, D_MODEL), preferred_element_type=F32).astype(BF16)
    z = jnp.dot(h, w_ref[...], preferred_element_type=F32)
    r = z[:, 0:RW_DIM]
    k = z[:, RW_DIM:2 * RW_DIM]
    v = z[:, 2 * RW_DIM:3 * RW_DIM]
    lora_in = z[:, 3 * RW_DIM:4 * RW_DIM]
    gate_in = z[:, 4 * RW_DIM:4 * RW_DIM + GATE_LORA]
    wa = jnp.dot(_lora_act(lora_in).astype(BF16), lw_ref[...], preferred_element_type=F32) + lb_ref[...]
    g = jnp.dot(_sigmoid(gate_in).astype(BF16), g2_ref[...], preferred_element_type=F32)
    kk = k * kkw_ref[...]
    kk = kk / jnp.maximum(jnp.sqrt(_head_sum(kk * kk, ones_ref)), 1e-12)
    per_dir = []
    for d in range(2):
        decay = jnp.exp(-float(np.exp(-0.5)) * _sigmoid(wa[:, d * RW_DIM:(d + 1) * RW_DIM]))
        a = _sigmoid(wa[:, (2 + d) * RW_DIM:(3 + d) * RW_DIM])
        per_dir.append((decay, k * (1.0 + (a - 1.0) * ka_ref[...]), kk * a))
    bonus = _head_sum(r * rk_ref[...] * (per_dir[0][1] + per_dir[1][1]), ones_ref) * v
    outs = (per_dir[0][0], per_dir[1][0], per_dir[0][1], per_dir[1][1], per_dir[0][2], per_dir[1][2], kk, r, v,
            bonus, g)
    for ref, val in zip(out_refs, outs):
        ref[...] = val.reshape(tt, nbb, RW_DIM)


def _rw_prep(h_seq, seq0, nb, t, w_rw, lora_w, lora_b, g2, lp, ones):
    nbb = nb if nb * 16 <= _PREP_ROWS else SCAN_NB
    tt = _PREP_ROWS // nbb
    assert seq0 % nbb == 0 and nb % nbb == 0 and t % tt == 0 and tt % 16 == 0
    b0 = seq0 // nbb
    rows = np.arange(_PREP_ROWS)
    perm = np.zeros((_PREP_ROWS, _PREP_ROWS), np.float32)
    perm[rows, (rows % nbb) * tt + rows // nbb] = 1.0

    def full(shape):
        return pl.BlockSpec(shape, lambda i, c: (0,) * len(shape))

    return pl.pallas_call(
        functools.partial(_rw_prep_kernel, nbb=nbb, tt=tt),
        out_shape=(jax.ShapeDtypeStruct((t, nb, RW_DIM), F32),) * _PREP_OUTS,
        grid=(nb // nbb, t // tt),
        in_specs=[pl.BlockSpec((nbb, tt, D_MODEL), lambda i, c: (b0 + i, c, 0)), full(perm.shape),
                  full(w_rw.shape), full(lora_w.shape), full((1, 4 * RW_DIM)), full(g2.shape),
                  full((1, RW_DIM)), full((1, RW_DIM)), full((1, RW_DIM)), full((RW_DIM, RW_DIM))],
        out_specs=(pl.BlockSpec((tt, nbb, RW_DIM), lambda i, c: (c, i, 0)),) * _PREP_OUTS,
        compiler_params=_params("parallel", "arbitrary"),
        name="rwkv_prep",
    )(h_seq, jnp.asarray(perm, BF16), w_rw, lora_w, lora_b.reshape(1, 4 * RW_DIM), g2, lp["kk"].reshape(1, RW_DIM),
      lp["ka"].reshape(1, RW_DIM), lp["rk"].reshape(1, RW_DIM), ones)


_SCAN_OPS = 6


def _half_swap(x):
    return pltpu.roll(x, LANES // 2, 1)


def _rw_scan_kernel(*refs, n_src, n_keep, period, reverse):
    nb = SCAN_NB
    tc = SCAN_TC
    ins = [refs[s * _SCAN_OPS:(s + 1) * _SCAN_OPS] for s in range(n_src)]
    s0_ref = refs[n_src * _SCAN_OPS]
    y_refs = refs[n_src * _SCAN_OPS + 1:n_src * _SCAN_OPS + 1 + n_src]
    sf_keep_ref, sf_reset_ref, ops_scr, y_scr, s_scr = refs[n_src * _SCAN_OPS + 1 + n_src:]
    c = pl.program_id(0)
    keep_lanes = n_keep * RW_HEADS * nb
    left = lax.broadcasted_iota(jnp.int32, (nb, LANES), 1) < LANES // 2

    @pl.when(c == 0)
    def _():
        s_scr[...] = s0_ref[...]
        y_scr[...] = jnp.zeros_like(y_scr)

    @pl.when((c > 0) & (c % period == 0))
    def _():
        keep = lax.broadcasted_iota(jnp.int32, (1, 1, LANES), 2) < keep_lanes
        s_scr[...] = jnp.where(keep, s_scr[...], 0.0)

    def times(q, s):
        return (tc - 1 - 2 * q, tc - 2 - 2 * q) if reverse[s] else (2 * q, 2 * q + 1)

    def relayout_in(q, ops):
        for o in range(_SCAN_OPS):
            rows = []
            for s in range(n_src):
                t0, t1 = times(q, s)
                a0 = ins[s][o][t0]
                a1 = ins[s][o][t1]
                for h in range(RW_HEADS):
                    j = h // 2
                    p0 = a0[:, LANES * j:LANES * (j + 1)]
                    p1 = a1[:, LANES * j:LANES * (j + 1)]
                    if h % 2 == 0:
                        rows.append(jnp.where(left, p0, _half_swap(p1)))
                    else:
                        rows.append(jnp.where(left, _half_swap(p0), p1))
            tile = jnp.concatenate(rows, axis=0).T
            ops[o, 0] = tile[0:HEAD_DIM]
            ops[o, 1] = tile[HEAD_DIM:]

    def steps(ops, y):
        for t in range(2):
            for vi in range(HEAD_DIM):
                s = s_scr[vi]
                sa = jnp.sum(s * ops[1, t], axis=0, keepdims=True)
                s = s * ops[0, t] - sa * ops[2, t] + ops[5, t, pl.ds(vi, 1), :] * ops[3, t]
                s_scr[vi] = s
                y[t, pl.ds(vi, 1), :] = jnp.sum(s * ops[4, t], axis=0, keepdims=True)

    def relayout_out(q, y):
        tile = jnp.concatenate([y[0], y[1]], axis=0).T
        for s in range(n_src):
            t0, t1 = times(q, s)
            for j in range(RW_HEADS // 2):
                r0 = (s * RW_HEADS + 2 * j) * nb
                even = tile[r0:r0 + nb]
                odd = tile[r0 + nb:r0 + 2 * nb]
                y_refs[s][t0, :, LANES * j:LANES * (j + 1)] = jnp.where(left, even, _half_swap(odd))
                y_refs[s][t1, :, LANES * j:LANES * (j + 1)] = jnp.where(left, _half_swap(even), odd)

    n_pairs = tc // 2
    relayout_in(0, ops_scr.at[0])

    def body(q, carry):
        par = q % 2
        relayout_out(jnp.maximum(q - 1, 0), y_scr.at[1 - par])
        steps(ops_scr.at[par], y_scr.at[par])
        relayout_in(jnp.minimum(q + 1, n_pairs - 1), ops_scr.at[1 - par])
        return carry

    lax.fori_loop(0, n_pairs, body, 0)
    relayout_out(n_pairs - 1, y_scr.at[(n_pairs - 1) % 2])

    @pl.when(c % period == period - 1)
    def _():
        sf_reset_ref[...] = s_scr[...]

    @pl.when(c == pl.num_programs(0) - 1)
    def _():
        sf_keep_ref[...] = s_scr[...]


def _rw_scan(keep_srcs, reset_srcs, s0, t_keep, t_reset, reverse):
    tc, nb = SCAN_TC, SCAN_NB
    n_keep = len(keep_srcs)
    n_src = n_keep + len(reset_srcs)
    nc = t_keep // tc
    period = t_reset // tc
    n_phase = nc // period

    def idx(s, c):
        if s < n_keep:
            return 0, (nc - 1 - c if reverse[s] else c)
        cc = c % period
        return c // period, (period - 1 - cc if reverse[s] else cc)

    y_specs = tuple(pl.BlockSpec((tc, nb, RW_DIM), lambda c, s=s: (idx(s, c)[1], idx(s, c)[0], 0))
                    for s in range(n_src))
    in_specs, args = [], []
    for s, src in enumerate(list(keep_srcs) + list(reset_srcs)):
        in_specs += [y_specs[s]] * _SCAN_OPS
        args += list(src)
    st = pl.BlockSpec((HEAD_DIM, HEAD_DIM, LANES), lambda c: (0, 0, 0))
    y_shapes = tuple(jax.ShapeDtypeStruct((t_keep, nb, RW_DIM) if s < n_keep else (t_reset, nb * n_phase, RW_DIM), F32)
                     for s in range(n_src))
    out = pl.pallas_call(
        functools.partial(_rw_scan_kernel, n_src=n_src, n_keep=n_keep, period=period, reverse=tuple(reverse)),
        out_shape=y_shapes + (jax.ShapeDtypeStruct((HEAD_DIM, HEAD_DIM, LANES), F32),
                              jax.ShapeDtypeStruct((n_phase, HEAD_DIM, HEAD_DIM, LANES), F32)),
        grid=(nc,),
        in_specs=in_specs + [st],
        out_specs=y_specs + (st, pl.BlockSpec((None, HEAD_DIM, HEAD_DIM, LANES), lambda c: (c // period, 0, 0, 0))),
        scratch_shapes=[pltpu.VMEM((2, _SCAN_OPS, 2, HEAD_DIM, LANES), F32),
                        pltpu.VMEM((2, 2, HEAD_DIM, LANES), F32),
                        pltpu.VMEM((HEAD_DIM, HEAD_DIM, LANES), F32)],
        compiler_params=_params("arbitrary"),
        name="rwkv_scan",
    )(*args, s0)
    return out[:n_src], out[n_src], out[n_src + 1]


def _rw_post_kernel(yf_ref, yb_ref, bonus_ref, g_ref, gw_ref, gb_ref, ones_ref, o_ref, stage_scr, *, nb):
    tt = SCAN_TC
    inv = 1.0 / HEAD_DIM
    y = (yf_ref[...] + yb_ref[...]).reshape(tt * nb, RW_DIM)
    mu = _head_sum(y, ones_ref) * inv
    d = y - mu
    var = _head_sum(d * d, ones_ref) * inv
    yn = d * lax.rsqrt(var + GN_EPS) * gw_ref[...] + gb_ref[...]
    out = (yn + bonus_ref[...].reshape(tt * nb, RW_DIM)) * g_ref[...].reshape(tt * nb, RW_DIM)
    stage_scr[...] = out.reshape(tt, nb, RW_DIM)

    def body(b, carry):
        o_ref[b] = stage_scr[:, b, :]
        return carry

    lax.fori_loop(0, nb, body, 0)


def _rw_post(y_f, y_b, bonus, g, gn_w, gn_b, ones, nb, t):
    tt = SCAN_TC
    tmaj = pl.BlockSpec((tt, nb, RW_DIM), lambda c: (c, 0, 0))
    vec = pl.BlockSpec((1, RW_DIM), lambda c: (0, 0))
    out = pl.pallas_call(
        functools.partial(_rw_post_kernel, nb=nb),
        out_shape=jax.ShapeDtypeStruct((nb, t, RW_DIM), F32),
        grid=(t // tt,),
        in_specs=[tmaj, tmaj, tmaj, tmaj, vec, vec, pl.BlockSpec((RW_DIM, RW_DIM), lambda c: (0, 0))],
        out_specs=pl.BlockSpec((nb, tt, RW_DIM), lambda c: (0, c, 0)),
        scratch_shapes=[pltpu.VMEM((tt, nb, RW_DIM), F32)],
        compiler_params=_params("parallel"),
        name="rwkv_post",
    )(y_f, y_b, bonus, g, gn_w.reshape(1, RW_DIM), gn_b.reshape(1, RW_DIM), ones)
    return out.reshape(nb * t, RW_DIM)


def _scan_state(states):
    z = jnp.concatenate([s.transpose(2, 3, 1, 0).reshape(HEAD_DIM, HEAD_DIM, -1) for s in states], axis=-1)
    return jnp.pad(z, ((0, 0), (0, 0), (0, LANES - z.shape[-1])))


def _unscan_reset_state(sf, src):
    n = RW_HEADS * SCAN_NB
    z = sf[:, :, :, src * n:(src + 1) * n].reshape(-1, HEAD_DIM, HEAD_DIM, RW_HEADS, SCAN_NB)
    return z.transpose(0, 4, 3, 1, 2).reshape(-1, RW_HEADS, HEAD_DIM, HEAD_DIM)


def _rwkv_operands(h, seq0, nb, t, *args):
    dw_f, dw_b, kd_f, kd_b, b_f, b_b, kk, r, v, bonus, g = _rw_prep(h.reshape(-1, t, D_MODEL), seq0, nb, t, *args)
    return (dw_f, kk, b_f, kd_f, r, v), (dw_b, kk, b_b, kd_b, r, v), bonus, g


def _rwkv(h, w_rw, lora_w, lora_b, g2, lp, ones, s0_f, s0_b):
    assert DEC_BATCH == SCAN_NB and BATCH * SEQ == DEC_BATCH * DEC_SEQ
    args = (w_rw, lora_w, lora_b, g2, lp, ones)
    c_f, c_b, c_bonus, c_g = _rwkv_operands(h, 0, BATCH, SEQ, *args)
    l_f, l_b, l_bonus, l_g = _rwkv_operands(h, N_CTX // DEC_SEQ, DEC_BATCH, DEC_SEQ, *args)
    (yl_f, yl_b, yc_f, yc_b), _, sf_ctx = _rw_scan([l_f, l_b], [c_f, c_b], _scan_state([s0_f, s0_b]),
                                                   DEC_SEQ, SEQ, [False, True, False, True])
    post = (lp["gn_w"], lp["gn_b"], ones)
    rw_ctx = _rw_post(yc_f, yc_b, c_bonus, c_g, *post, BATCH, SEQ)
    rw_lat = _rw_post(yl_f, yl_b, l_bonus, l_g, *post, DEC_BATCH, DEC_SEQ)
    return rw_ctx, rw_lat, _unscan_reset_state(sf_ctx, 2), _unscan_reset_state(sf_ctx, 3)


_TM_MERGE = 512


def _merge_kernel(x_ref, gm_ref, h_ref, nac_ref, nal_ref, rwc_ref, rwl_ref, mlc_ref, mll_ref, wg_ref, wbr_ref, wo_ref,
                  o_ref):
    is_ctx = pl.program_id(0) < N_CTX // _TM_MERGE
    h = h_ref[...]
    branches = tuple(jnp.where(is_ctx, c_ref[...], l_ref[...])
                     for c_ref, l_ref in ((nac_ref, nal_ref), (rwc_ref, rwl_ref), (mlc_ref, mll_ref)))
    m = None
    for i, o_b in enumerate(branches):
        gate = jnp.dot(h, wg_ref[:, i * D_MODEL:(i + 1) * D_MODEL], preferred_element_type=F32)
        br = jnp.dot(o_b.astype(BF16), wbr_ref[i], preferred_element_type=F32)
        t = _sigmoid(gate) * br
        m = t if m is None else m + t
    mix = jnp.dot(m.astype(BF16), wo_ref[...], preferred_element_type=F32)
    o_ref[...] = x_ref[...] + gm_ref[...] * mix


def _merge(x, mod_l, h, na_ctx, na_lat, rw_ctx, rw_lat, mla_ctx, mla_lat, w_gate, w_br, w_o):
    tm = _TM_MERGE
    n_ctx = N_CTX // tm
    row = pl.BlockSpec((tm, D_MODEL), lambda i: (i, 0))
    br_ctx = pl.BlockSpec((tm, BRANCH_DIM), lambda i: (jnp.minimum(i, n_ctx - 1), 0))
    br_lat = pl.BlockSpec((tm, BRANCH_DIM), lambda i: (jnp.maximum(i - n_ctx, 0), 0))
    return pl.pallas_call(
        _merge_kernel,
        out_shape=jax.ShapeDtypeStruct((N_TOK, D_MODEL), F32),
        grid=(N_TOK // tm,),
        in_specs=[row, pl.BlockSpec((None, 1, D_MODEL), lambda i: (_cond_row(i, tm), 0, 5)),
                  row, br_ctx, br_lat, br_ctx, br_lat, br_ctx, br_lat,
                  pl.BlockSpec((D_MODEL, N_BRANCH * D_MODEL), lambda i: (0, 0)),
                  pl.BlockSpec((N_BRANCH, BRANCH_DIM, D_MODEL), lambda i: (0, 0, 0)),
                  pl.BlockSpec((D_MODEL, D_MODEL), lambda i: (0, 0))],
        out_specs=row,
        compiler_params=_params("arbitrary"),
        name="merge",
    )(x, mod_l, h, na_ctx, na_lat, rw_ctx, rw_lat, mla_ctx, mla_lat, w_gate, w_br, w_o)


def _block_diag(blocks):
    rows = sum(b.shape[0] for b in blocks)
    cols = sum(b.shape[1] for b in blocks)
    out = jnp.zeros((rows, cols), blocks[0].dtype)
    r = c = 0
    for b in blocks:
        out = lax.dynamic_update_slice(out, b, (r, c))
        r += b.shape[0]
        c += b.shape[1]
    return out


def _pad_cols(w, left, total):
    return jnp.pad(w, ((0, 0), (left, total - left - w.shape[1])))


def _mla_uq_layout(w_uq):
    per = NOPE_DIM + ROPE_DIM
    return jnp.concatenate([_pad_cols(w_uq[:, h * per:(h + 1) * per], 0, LANES) for h in range(MLA_HEADS)], axis=1)


def _mla_ukv_layout(w_ukv):
    per = NOPE_DIM + V_DIM
    k = [_pad_cols(w_ukv[:, h * per:h * per + NOPE_DIM], 0, LANES) for h in range(MLA_HEADS)]
    v = [w_ukv[:, h * per + NOPE_DIM:(h + 1) * per] for h in range(MLA_HEADS)]
    return jnp.concatenate(k + v, axis=1)


def kernel(x_prompt, x_sample, c, cache_na_k, cache_na_v, cache_mla_ckv, cache_mla_krope, state_rwkv_fwd, state_rwkv_bwd, c_ctx, ada_w, ada_b, norm_g, ffn_wg, ffn_wu, ffn_wd, w_in, na_rpb, rw_w0, rw_w2, rw_a0, rw_a2, rw_g2, rw_kk, rw_ka, rw_rk, rw_gn_w, rw_gn_b, mla_qn_g, mla_kvn_g, mla_w_uq, mla_w_ukv, w_br, w_o, final_g):
    x = (x_prompt.reshape(N_CTX, D_MODEL), x_sample.reshape(N_LAT, D_MODEL))
    cond = jnp.concatenate([c_ctx[None, :], c, jnp.zeros((N_COND - 1 - DEC_BATCH, D_MODEL), F32)], axis=0)
    mod = _modulation(cond, ada_w, ada_b)

    ones = jnp.asarray(np.kron(np.eye(RW_HEADS), np.ones((HEAD_DIM, HEAD_DIM))), BF16)
    rope = _rope_tables(DEC_SEQ)

    col_rw = 3 * NA_HEADS * HEAD_DIM
    col_mla = col_rw + 3 * RW_DIM + 2 * DECAY_LORA + 2 * ICLR_LORA + GATE_LORA
    col_kr = col_mla + Q_LORA + KV_LORA
    col_gate = col_kr + ROPE_DIM

    caches = []
    na_kv = None
    y = None
    for l in range(DEPTH):
        mod_l = mod[l]
        wi = w_in[l].astype(BF16)
        w_na = wi[:, :col_rw]
        w_rw = wi[:, col_rw:col_mla]
        w_mla = jnp.concatenate([wi[:, col_mla:col_kr], _pad_cols(wi[:, col_kr:col_gate], _ROPE_LO, LANES)], axis=1)
        w_gate = wi[:, col_gate:]

        x, h = _ffn(x, norm_g[l, 0], mod_l, 0, ffn_wg[l, 0].astype(BF16), ffn_wu[l, 0].astype(BF16),
                    ffn_wd[l, 0].astype(BF16), post="mod", post_g=norm_g[l, 1], post_chunk=3)

        z_na = _mm(h, w_na, name="proj_na")
        na_ctx, *na_kv = _na_ctx(z_na, l, na_kv)
        na_lat = _na_latent(na_rpb[l].reshape(NA_HEADS * (2 * NA_WIN_H - 1), 2 * NA_WIN_W - 1), z_na,
                            cache_na_k, cache_na_v, l)

        lora_w = _block_diag([rw_w2[l, 0], rw_w2[l, 1], rw_a2[l, 0], rw_a2[l, 1]]).astype(BF16)
        lora_b = jnp.concatenate([rw_w0[l, 0], rw_w0[l, 1], rw_a0[l, 0], rw_a0[l, 1]])
        lp = {"kk": rw_kk[l], "ka": rw_ka[l], "rk": rw_rk[l].reshape(RW_DIM), "gn_w": rw_gn_w[l], "gn_b": rw_gn_b[l]}
        rw_ctx, rw_lat, sc_f, sc_b = _rwkv(h, w_rw, lora_w, lora_b, rw_g2[l].astype(BF16), lp, ones,
                                           state_rwkv_fwd[:, l], state_rwkv_bwd[:, l])

        w_ukv = _mla_ukv_layout(mla_w_ukv[l]).astype(BF16)
        q_m, kv_m, ckv, kr_m = _mla_prep(h, w_mla, mla_qn_g[l], mla_kvn_g[l],
                                         _mla_uq_layout(mla_w_uq[l]).astype(BF16), w_ukv)
        kv_cache = _mm(cache_mla_ckv[:, l].reshape(DEC_BATCH * PAST_LEN, KV_LORA), w_ukv, "mla_ukv_cache")
        kr_cache = _pad_cols(cache_mla_krope[:, l].reshape(DEC_BATCH * PAST_LEN, ROPE_DIM), _ROPE_LO, LANES)
        mla_ctx = _mla_ctx(q_m, kv_m, kr_m)
        mla_lat = _mla_latent(q_m, kv_m, kr_m, kv_cache, kr_cache, rope)

        x = _merge(x, mod_l, h, na_ctx, na_lat, rw_ctx, rw_lat, mla_ctx, mla_lat, w_gate,
                   w_br[l].astype(BF16), w_o[l].astype(BF16))

        ffn2 = (x, norm_g[l, 2], mod_l, 6, ffn_wg[l, 1].astype(BF16), ffn_wu[l, 1].astype(BF16),
                ffn_wd[l, 1].astype(BF16))
        if l == DEPTH - 1:
            y = _ffn(*ffn2, post="plain", post_g=final_g)
        else:
            x = _ffn(*ffn2)

        caches.append((ckv[:N_CTX].reshape(BATCH, SEQ, KV_LORA),
                       kr_m[:N_CTX, _ROPE_LO:_ROPE_LO + ROPE_DIM].reshape(BATCH, SEQ, ROPE_DIM),
                       sc_f, sc_b))

    y_prompt = y[0].reshape(BATCH, SEQ, D_MODEL)
    y_sample = y[1].reshape(DEC_BATCH, DEC_SEQ, D_MODEL)
    outs = [jnp.stack([cl[i] for cl in caches], axis=1) for i in range(4)]
    return (y_prompt, y_sample, *na_kv, *outs)
```

```python
import functools

import numpy as np
import jax
import jax.numpy as jnp
from jax import lax
from jax.experimental import pallas as pl
from jax.experimental.pallas import tpu as pltpu

F32 = jnp.float32
BF16 = jnp.bfloat16

D_MODEL = 1024
BATCH = 32
SEQ = 256
DEPTH = 2
DEC_BATCH = 8
DEC_SEQ = 1024
PAST_LEN = 256
GRID_W = 64
HEAD_DIM = 64
NA_HEADS = 4
NA_WIN_H = 8
NA_WIN_W = 16
RW_HEADS = 4
RW_DIM = RW_HEADS * HEAD_DIM
DECAY_LORA = 64
ICLR_LORA = 64
GATE_LORA = 128
GN_EPS = 64e-5
MLA_HEADS = 4
Q_LORA = 256
KV_LORA = 128
NOPE_DIM = 64
ROPE_DIM = 32
V_DIM = 64
MLA_SCALE = (NOPE_DIM + ROPE_DIM) ** -0.5
ROPE_BASE = 10000.0
N_BRANCH = 3
BRANCH_DIM = 256
D_FF = 2816
N_MOD = 9
NORM_EPS = 1e-6
NEG_INF = -1e30

N_CTX = BATCH * SEQ
N_LAT = DEC_BATCH * DEC_SEQ
N_TOK = N_CTX + N_LAT
N_COND = 16

LANES = 128
VMEM_LIMIT = 56 * 1024 * 1024

TM = 1024
TM_FFN = 512
SCAN_TC = 64


def _cond_row(i, tm):
    return jnp.where(i * tm < N_CTX, 0, 1 + (i * tm - N_CTX) // DEC_SEQ)


def _sigmoid(x):
    return 1.0 / (1.0 + jnp.exp(-x))


def _params(*sem):
    return pltpu.CompilerParams(dimension_semantics=sem, vmem_limit_bytes=VMEM_LIMIT)


def _mm_kernel(a_ref, w_ref, o_ref):
    o_ref[...] = jnp.dot(a_ref[...].astype(BF16), w_ref[...], preferred_element_type=F32)


def _mm(a, w, name):
    m, k = a.shape
    n = w.shape[1]
    tm = min(m, TM)
    assert m % tm == 0 and w.shape[0] == k
    return pl.pallas_call(
        _mm_kernel,
        out_shape=jax.ShapeDtypeStruct((m, n), F32),
        grid=(m // tm,),
        in_specs=[pl.BlockSpec((tm, k), lambda i: (i, 0)), pl.BlockSpec((k, n), lambda i: (0, 0))],
        out_specs=pl.BlockSpec((tm, n), lambda i: (i, 0)),
        compiler_params=_params("parallel"),
        name=name,
    )(a, w)


def _mod_kernel(c_ref, w_ref, b_ref, o_ref):
    c = c_ref[...]
    a = (c * _sigmoid(c)).astype(BF16)
    o_ref[...] = jnp.dot(a, w_ref[...].astype(BF16), preferred_element_type=F32) + b_ref[...]


def _modulation(cond, ada_w, ada_b):
    n = N_MOD * D_MODEL
    tn = D_MODEL
    out = pl.pallas_call(
        _mod_kernel,
        out_shape=jax.ShapeDtypeStruct((DEPTH, N_COND, n), F32),
        grid=(DEPTH, n // tn),
        in_specs=[pl.BlockSpec((N_COND, D_MODEL), lambda l, j: (0, 0)),
                  pl.BlockSpec((None, D_MODEL, tn), lambda l, j: (l, 0, j)),
                  pl.BlockSpec((None, 1, tn), lambda l, j: (l, 0, j))],
        out_specs=pl.BlockSpec((None, N_COND, tn), lambda l, j: (l, 0, j)),
        compiler_params=_params("parallel", "arbitrary"),
        name="modulation",
    )(cond, ada_w, ada_b.reshape(DEPTH, 1, n))
    return out.reshape(DEPTH, N_COND, 1, n)


def _rms(x, g):
    return x * lax.rsqrt(jnp.mean(x * x, axis=-1, keepdims=True) + NORM_EPS) * g


_FFN_CTX_TILES = N_CTX // TM_FFN


def _ffn_kernel(*refs, post, split_in):
    n_x = 2 if split_in else 1
    x_refs = refs[:n_x]
    g_ref, sh_ref, sc_ref, gt_ref, wg_ref, wu_ref, wd_ref = refs[n_x:n_x + 7]
    rest = refs[n_x + 7:]
    is_ctx = pl.program_id(0) < _FFN_CTX_TILES
    x = jnp.where(is_ctx, x_refs[0][...], x_refs[1][...]) if split_in else x_refs[0][...]
    h = (_rms(x, g_ref[...]) * (1.0 + sc_ref[...]) + sh_ref[...]).astype(BF16)
    gg = jnp.dot(h, wg_ref[...], preferred_element_type=F32)
    uu = jnp.dot(h, wu_ref[...], preferred_element_type=F32)
    a = (gg * _sigmoid(gg)) * uu
    xn = x + 0.5 * gt_ref[...] * jnp.dot(a.astype(BF16), wd_ref[...], preferred_element_type=F32)
    if post == "plain":
        g2_ref, ya_ref, yb_ref = rest
        y = _rms(xn, g2_ref[...])

        @pl.when(is_ctx)
        def _():
            ya_ref[...] = y

        @pl.when(jnp.logical_not(is_ctx))
        def _():
            yb_ref[...] = y
    elif post == "mod":
        g2_ref, sh2_ref, sc2_ref, o_ref, h2_ref = rest
        o_ref[...] = xn
        h2_ref[...] = (_rms(xn, g2_ref[...]) * (1.0 + sc2_ref[...]) + sh2_ref[...]).astype(BF16)
    else:
        (o_ref,) = rest
        o_ref[...] = xn


def _ffn(x, norm_g, mod_l, chunk0, wg, wu, wd, post=None, post_g=None, post_chunk=None):
    def mod_spec(c):
        return pl.BlockSpec((None, 1, D_MODEL), lambda i: (_cond_row(i, TM_FFN), 0, c))

    def weight(shape):
        return pl.BlockSpec(shape, lambda i: (0, 0), pipeline_mode=pl.Buffered(1))

    row = pl.BlockSpec((TM_FFN, D_MODEL), lambda i: (i, 0))
    row_ctx = pl.BlockSpec((TM_FFN, D_MODEL), lambda i: (jnp.minimum(i, _FFN_CTX_TILES - 1), 0))
    row_lat = pl.BlockSpec((TM_FFN, D_MODEL), lambda i: (jnp.maximum(i - _FFN_CTX_TILES, 0), 0))
    vec = pl.BlockSpec((1, D_MODEL), lambda i: (0, 0))
    split_in = isinstance(x, tuple)
    in_specs = ([row_ctx, row_lat] if split_in else [row]) + [
        vec, mod_spec(chunk0), mod_spec(chunk0 + 1), mod_spec(chunk0 + 2),
        weight((D_MODEL, D_FF)), weight((D_MODEL, D_FF)), weight((D_FF, D_MODEL))]
    args = (list(x) if split_in else [x]) + [norm_g.reshape(1, D_MODEL), mod_l, mod_l, mod_l, wg, wu, wd]
    xs = jax.ShapeDtypeStruct((N_TOK, D_MODEL), F32)
    if post == "mod":
        in_specs += [vec, mod_spec(post_chunk), mod_spec(post_chunk + 1)]
        args += [post_g.reshape(1, D_MODEL), mod_l, mod_l]
        out_shape, out_specs = (xs, jax.ShapeDtypeStruct((N_TOK, D_MODEL), BF16)), (row, row)
    elif post == "plain":
        in_specs += [vec]
        args += [post_g.reshape(1, D_MODEL)]
        out_shape = (jax.ShapeDtypeStruct((N_CTX, D_MODEL), F32), jax.ShapeDtypeStruct((N_LAT, D_MODEL), F32))
        out_specs = (row_ctx, row_lat)
    else:
        out_shape, out_specs = xs, row
    return pl.pallas_call(
        functools.partial(_ffn_kernel, post=post, split_in=split_in),
        out_shape=out_shape,
        grid=(N_TOK // TM_FFN,),
        in_specs=in_specs,
        out_specs=out_specs,
        compiler_params=_params("arbitrary"),
        name="ffn",
    )(*args)


_NT = (((1,), (1,)), ((), ()))
_LAT_ROWS = DEC_SEQ // GRID_W
_NA_KH = min(NA_WIN_H, _LAT_ROWS)
_NA_LOCAL = _NA_KH * GRID_W
_NA_PAIRS = 2 * NA_WIN_H - 2


def _head_mask(width, h, group):
    lane = lax.broadcasted_iota(jnp.int32, (1, width), 1)
    return (lane >= h * group) & (lane < (h + 1) * group)


def _stack_heads(q, nh, group):
    return jnp.concatenate([jnp.where(_head_mask(q.shape[1], h, group), q, 0.0) for h in range(nh)], axis=0)


def _pick_heads(r, nh, group):
    t = r.shape[0] // nh
    out = r[0:t]
    for h in range(1, nh):
        out = jnp.where(_head_mask(r.shape[1], h, group), r[h * t:(h + 1) * t], out)
    return out


def _softmax2(s_a, s_b):
    m = jnp.maximum(jnp.max(s_a, axis=-1, keepdims=True), jnp.max(s_b, axis=-1, keepdims=True))
    p_a = jnp.exp(s_a - m)
    p_b = jnp.exp(s_b - m)
    return p_a, p_b, jnp.sum(p_a, axis=-1, keepdims=True) + jnp.sum(p_b, axis=-1, keepdims=True)


def _na_ctx_kernel(q_ref, k_ref, v_ref, *rest):
    o_ref, ko_ref, vo_ref = rest[-3:]
    k = k_ref[...]
    v = v_ref[...]
    q_st = _stack_heads(q_ref[...] * HEAD_DIM ** -0.5, NA_HEADS, HEAD_DIM).astype(BF16)
    s = lax.dot_general(q_st, k.astype(BF16), _NT, preferred_element_type=F32)
    p = jnp.exp(s - jnp.max(s, axis=-1, keepdims=True))
    l = jnp.sum(p, axis=-1, keepdims=True)
    r = jnp.dot(p.astype(BF16), v.astype(BF16), preferred_element_type=F32) / l
    o_ref[...] = _pick_heads(r, NA_HEADS, HEAD_DIM)
    for h in range(NA_HEADS):
        ko_ref[h] = k[:, h * HEAD_DIM:(h + 1) * HEAD_DIM]
        vo_ref[h] = v[:, h * HEAD_DIM:(h + 1) * HEAD_DIM]


def _na_ctx(z_na, layer, caches):
    w = NA_HEADS * HEAD_DIM
    cache_shape = jax.ShapeDtypeStruct((BATCH, DEPTH, NA_HEADS, SEQ, HEAD_DIM), F32)
    cache_spec = pl.BlockSpec((None, None, NA_HEADS, SEQ, HEAD_DIM), lambda b: (b, layer, 0, 0, 0))
    in_specs = [pl.BlockSpec((SEQ, w), lambda b: (b, 0)),
                pl.BlockSpec((SEQ, w), lambda b: (b, 1)),
                pl.BlockSpec((SEQ, w), lambda b: (b, 2))]
    args = [z_na, z_na, z_na]
    aliases = {}
    if caches is not None:
        in_specs += [pl.BlockSpec(memory_space=pl.ANY)] * 2
        args += list(caches)
        aliases = {3: 1, 4: 2}
    return pl.pallas_call(
        _na_ctx_kernel,
        out_shape=(jax.ShapeDtypeStruct((N_CTX, w), F32), cache_shape, cache_shape),
        grid=(BATCH,),
        in_specs=in_specs,
        out_specs=(pl.BlockSpec((SEQ, w), lambda b: (b, 0)), cache_spec, cache_spec),
        input_output_aliases=aliases,
        compiler_params=_params("parallel"),
        name="na_ctx",
    )(*args)


def _na_row_lo(qr):
    return min(max(qr - NA_WIN_H // 2, 0), _LAT_ROWS - _NA_KH)


def _na_lat_kernel(rpb_ref, q_ref, k_ref, v_ref, kc_ref, vc_ref, o_ref, bias_scr):
    scale = HEAD_DIM ** -0.5

    @pl.when(pl.program_id(0) == 0)
    def _():
        qc = lax.broadcasted_iota(jnp.int32, (GRID_W, 2 * GRID_W), 0)
        lane = lax.broadcasted_iota(jnp.int32, (GRID_W, 2 * GRID_W), 1)
        kc = lane & (GRID_W - 1)
        col_lo = jnp.clip(qc - NA_WIN_W // 2, 0, GRID_W - NA_WIN_W)
        rel = jnp.where((kc >= col_lo) & (kc < col_lo + NA_WIN_W), kc - qc + NA_WIN_W - 1, -1)
        second = lax.broadcasted_iota(jnp.int32, (1, 2 * GRID_W), 1) >= GRID_W

        def build(idx, carry):
            h = idx // _NA_PAIRS
            row = h * (2 * NA_WIN_H - 1) + (idx - h * _NA_PAIRS)
            tile = jnp.full((GRID_W, 2 * GRID_W), NEG_INF, F32)
            for j in range(2 * NA_WIN_W - 1):
                val = jnp.where(second, rpb_ref[row + 1, j], rpb_ref[row, j])
                tile = jnp.where(rel == j, val, tile)
            bias_scr[idx] = tile
            return carry

        lax.fori_loop(0, NA_HEADS * _NA_PAIRS, build, 0)

    kc = jnp.concatenate([kc_ref[h] for h in range(NA_HEADS)], axis=1).astype(BF16)
    vc = jnp.concatenate([vc_ref[h] for h in range(NA_HEADS)], axis=1).astype(BF16)
    for qr in range(_LAT_ROWS):
        lo = _na_row_lo(qr)
        dr0 = lo - qr + NA_WIN_H - 1
        q_st = _stack_heads(q_ref[qr * GRID_W:(qr + 1) * GRID_W, :] * scale, NA_HEADS, HEAD_DIM).astype(BF16)
        kl = k_ref[lo * GRID_W:lo * GRID_W + _NA_LOCAL, :].astype(BF16)
        vl = v_ref[lo * GRID_W:lo * GRID_W + _NA_LOCAL, :].astype(BF16)
        bias = jnp.concatenate(
            [jnp.concatenate([bias_scr[h * _NA_PAIRS + dr0 + 2 * i] for i in range(_NA_KH // 2)], axis=1)
             for h in range(NA_HEADS)], axis=0)
        s_loc = lax.dot_general(q_st, kl, _NT, preferred_element_type=F32) + bias
        s_ctx = lax.dot_general(q_st, kc, _NT, preferred_element_type=F32)
        p_loc, p_ctx, l = _softmax2(s_loc, s_ctx)
        r = (jnp.dot(p_loc.astype(BF16), vl, preferred_element_type=F32)
             + jnp.dot(p_ctx.astype(BF16), vc, preferred_element_type=F32)) / l
        o_ref[qr * GRID_W:(qr + 1) * GRID_W, :] = _pick_heads(r, NA_HEADS, HEAD_DIM)


def _na_latent(rpb, z_na, kc, vc, layer):
    w = NA_HEADS * HEAD_DIM
    lat0 = N_CTX // DEC_SEQ

    def col(c):
        return pl.BlockSpec((DEC_SEQ, w), lambda b: (lat0 + b, c))

    cache = pl.BlockSpec((None, None, NA_HEADS, PAST_LEN, HEAD_DIM), lambda b: (b, layer, 0, 0, 0))
    return pl.pallas_call(
        _na_lat_kernel,
        out_shape=jax.ShapeDtypeStruct((N_LAT, w), F32),
        grid=(DEC_BATCH,),
        in_specs=[pl.BlockSpec(memory_space=pltpu.SMEM), col(0), col(1), col(2), cache, cache],
        out_specs=pl.BlockSpec((DEC_SEQ, w), lambda b: (b, 0)),
        scratch_shapes=[pltpu.VMEM((NA_HEADS * _NA_PAIRS, GRID_W, 2 * GRID_W), F32)],
        compiler_params=_params("arbitrary"),
        name="na_latent",
    )(rpb, z_na, z_na, z_na, kc, vc)


_MLA_W = MLA_HEADS * LANES
_ROPE_LO = NOPE_DIM
_ROPE_HALF = ROPE_DIM // 2


def _rope_tables(t_len):
    n_freq = ROPE_DIM // 4
    inv = 1.0 / (ROPE_BASE ** (np.arange(n_freq) / n_freq))
    pos = np.arange(t_len)
    ang = np.concatenate([(pos // GRID_W)[:, None] * inv, (pos % GRID_W)[:, None] * inv], axis=-1)
    cos, sin = np.cos(ang), np.sin(ang)
    c = np.ones((t_len, LANES))
    s1 = np.zeros((t_len, LANES))
    s2 = np.zeros((t_len, LANES))
    a, b, e = _ROPE_LO, _ROPE_LO + _ROPE_HALF, _ROPE_LO + ROPE_DIM
    c[:, a:b] = cos
    c[:, b:e] = cos
    s1[:, a:b] = -sin
    s2[:, b:e] = sin
    return tuple(jnp.asarray(t, F32) for t in (c, s1, s2))


def _apply_rope(x, c, s1, s2):
    up = pltpu.roll(x, LANES - _ROPE_HALF, 1)
    dn = pltpu.roll(x, _ROPE_HALF, 1)
    return x * c + up * s1 + dn * s2


def _mla_prep_kernel(h_ref, w_ref, qg_ref, kvg_ref, wuq_ref, wukv_ref, q_ref, kv_ref, ckv_ref, kr_ref):
    z = jnp.dot(h_ref[...], w_ref[...], preferred_element_type=F32)
    cq = _rms(z[:, :Q_LORA], qg_ref[...])
    q_ref[...] = jnp.dot(cq.astype(BF16), wuq_ref[...], preferred_element_type=F32)
    ckv = _rms(z[:, Q_LORA:Q_LORA + KV_LORA], kvg_ref[...])
    ckv_ref[...] = ckv
    kv_ref[...] = jnp.dot(ckv.astype(BF16), wukv_ref[...], preferred_element_type=F32)
    kr_ref[...] = z[:, Q_LORA + KV_LORA:]


def _mla_prep(h, w_mla, qn_g, kvn_g, w_uq, w_ukv):
    wv = MLA_HEADS * V_DIM
    row = lambda n: pl.BlockSpec((TM, n), lambda i: (i, 0))
    full = lambda a: pl.BlockSpec(a.shape, lambda i: (0, 0))
    qn_g = qn_g.reshape(1, Q_LORA)
    kvn_g = kvn_g.reshape(1, KV_LORA)
    return pl.pallas_call(
        _mla_prep_kernel,
        out_shape=(jax.ShapeDtypeStruct((N_TOK, _MLA_W), F32), jax.ShapeDtypeStruct((N_TOK, _MLA_W + wv), F32),
                   jax.ShapeDtypeStruct((N_TOK, KV_LORA), F32), jax.ShapeDtypeStruct((N_TOK, LANES), F32)),
        grid=(N_TOK // TM,),
        in_specs=[row(D_MODEL), full(w_mla), full(qn_g), full(kvn_g), full(w_uq), full(w_ukv)],
        out_specs=(row(_MLA_W), row(_MLA_W + wv), row(KV_LORA), row(LANES)),
        compiler_params=_params("parallel"),
        name="mla_prep",
    )(h, w_mla, qn_g, kvn_g, w_uq, w_ukv)


def _mla_ctx_kernel(q_ref, kv_ref, kr_ref, o_ref):
    kr = kr_ref[...]
    v_all = kv_ref[:, _MLA_W:].astype(BF16)
    out = None
    for h in range(MLA_HEADS):
        qh = (q_ref[:, h * LANES:(h + 1) * LANES] * MLA_SCALE).astype(BF16)
        kh = (kv_ref[:, h * LANES:(h + 1) * LANES] + kr).astype(BF16)
        s = lax.dot_general(qh, kh, _NT, preferred_element_type=F32)
        p = jnp.exp(s - jnp.max(s, axis=-1, keepdims=True))
        l = jnp.sum(p, axis=-1, keepdims=True)
        r = jnp.dot(p.astype(BF16), v_all, preferred_element_type=F32) / l
        out = r if out is None else jnp.where(_head_mask(MLA_HEADS * V_DIM, h, V_DIM), r, out)
    o_ref[...] = out


def _mla_ctx(q_m, kv_m, kr_m):
    wv = MLA_HEADS * V_DIM
    return pl.pallas_call(
        _mla_ctx_kernel,
        out_shape=jax.ShapeDtypeStruct((N_CTX, wv), F32),
        grid=(BATCH,),
        in_specs=[pl.BlockSpec((SEQ, _MLA_W), lambda b: (b, 0)),
                  pl.BlockSpec((SEQ, _MLA_W + wv), lambda b: (b, 0)),
                  pl.BlockSpec((SEQ, LANES), lambda b: (b, 0))],
        out_specs=pl.BlockSpec((SEQ, wv), lambda b: (b, 0)),
        compiler_params=_params("parallel"),
        name="mla_ctx",
    )(q_m, kv_m, kr_m)


_MLA_BQ = 256


def _mla_lat_kernel(q_ref, kv_ref, kr_ref, kvc_ref, krc_ref, cq_ref, s1q_ref, s2q_ref, ck_ref, s1k_ref, s2k_ref,
                    o_ref):
    kr = _apply_rope(kr_ref[...], ck_ref[...], s1k_ref[...], s2k_ref[...])
    krc = krc_ref[...]
    v_lat = kv_ref[:, _MLA_W:].astype(BF16)
    v_ctx = kvc_ref[:, _MLA_W:].astype(BF16)
    out = None
    for h in range(MLA_HEADS):
        blk = slice(h * LANES, (h + 1) * LANES)
        qh = (_apply_rope(q_ref[:, blk], cq_ref[...], s1q_ref[...], s2q_ref[...]) * MLA_SCALE).astype(BF16)
        kl = (kv_ref[:, blk] + kr).astype(BF16)
        kc = (kvc_ref[:, blk] + krc).astype(BF16)
        s_l = lax.dot_general(qh, kl, _NT, preferred_element_type=F32)
        s_c = lax.dot_general(qh, kc, _NT, preferred_element_type=F32)
        p_l, p_c, l = _softmax2(s_l, s_c)
        r = (jnp.dot(p_l.astype(BF16), v_lat, preferred_element_type=F32)
             + jnp.dot(p_c.astype(BF16), v_ctx, preferred_element_type=F32)) / l
        out = r if out is None else jnp.where(_head_mask(MLA_HEADS * V_DIM, h, V_DIM), r, out)
    o_ref[...] = out


def _mla_latent(q_m, kv_m, kr_m, kv_cache, kr_cache, rope):
    wv = MLA_HEADS * V_DIM
    nq = DEC_SEQ // _MLA_BQ
    q0 = N_CTX // _MLA_BQ
    lat0 = N_CTX // DEC_SEQ
    c, s1, s2 = rope
    tq = pl.BlockSpec((_MLA_BQ, LANES), lambda b, j: (j, 0))
    tk = pl.BlockSpec((DEC_SEQ, LANES), lambda b, j: (0, 0))
    return pl.pallas_call(
        _mla_lat_kernel,
        out_shape=jax.ShapeDtypeStruct((N_LAT, wv), F32),
        grid=(DEC_BATCH, nq),
        in_specs=[pl.BlockSpec((_MLA_BQ, _MLA_W), lambda b, j: (q0 + b * nq + j, 0)),
                  pl.BlockSpec((DEC_SEQ, _MLA_W + wv), lambda b, j: (lat0 + b, 0)),
                  pl.BlockSpec((DEC_SEQ, LANES), lambda b, j: (lat0 + b, 0)),
                  pl.BlockSpec((PAST_LEN, _MLA_W + wv), lambda b, j: (b, 0)),
                  pl.BlockSpec((PAST_LEN, LANES), lambda b, j: (b, 0)),
                  tq, tq, tq, tk, tk, tk],
        out_specs=pl.BlockSpec((_MLA_BQ, wv), lambda b, j: (b * nq + j, 0)),
        compiler_params=_params("parallel", "arbitrary"),
        name="mla_lat",
    )(q_m, kv_m, kr_m, kv_cache, kr_cache, c, s1, s2, c, s1, s2)


def _head_sum(x, ones_ref):
    ones = ones_ref[...]
    hi = x.astype(BF16)
    rest = x - hi.astype(F32)
    mid = rest.astype(BF16)
    lo = (rest - mid.astype(F32)).astype(BF16)
    return (jnp.dot(hi, ones, preferred_element_type=F32) + jnp.dot(mid, ones, preferred_element_type=F32)
            + jnp.dot(lo, ones, preferred_element_type=F32))


SCAN_NB = LANES // (4 * RW_HEADS)
_PREP_ROWS = 512
_PREP_OUTS = 11


def _lora_act(a):
    col = lax.broadcasted_iota(jnp.int32, a.shape, 1)
    return jnp.where(col < 2 * DECAY_LORA, jnp.tanh(a), a)


def _rw_prep_kernel(h_ref, perm_ref, w_ref, lw_ref, lb_ref, g2_ref, kkw_ref, ka_ref, rk_ref, ones_ref, *out_refs,
                    nbb, tt):
    h = jnp.dot(perm_ref[...], h_ref[...].reshape(nbb * tt, D_MODEL), preferred_element_type=F32).astype(BF16)
    z = jnp.dot(h, w_ref[...], preferred_element_type=F32)
    r = z[:, 0:RW_DIM]
    k = z[:, RW_DIM:2 * RW_DIM]
    v = z[:, 2 * RW_DIM:3 * RW_DIM]
    lora_in = z[:, 3 * RW_DIM:4 * RW_DIM]
    gate_in = z[:, 4 * RW_DIM:4 * RW_DIM + GATE_LORA]
    wa = jnp.dot(_lora_act(lora_in).astype(BF16), lw_ref[...], preferred_element_type=F32) + lb_ref[...]
    g = jnp.dot(_sigmoid(gate_in).astype(BF16), g2_ref[...], preferred_element_type=F32)
    kk = k * kkw_ref[...]
    kk = kk / jnp.maximum(jnp.sqrt(_head_sum(kk * kk, ones_ref)), 1e-12)
    per_dir = []
    for d in range(2):
        decay = jnp.exp(-float(np.exp(-0.5)) * _sigmoid(wa[:, d * RW_DIM:(d + 1) * RW_DIM]))
        a = _sigmoid(wa[:, (2 + d) * RW_DIM:(3 + d) * RW_DIM])
        per_dir.append((decay, k * (1.0 + (a - 1.0) * ka_ref[...]), kk * a))
    bonus = _head_sum(r * rk_ref[...] * (per_dir[0][1] + per_dir[1][1]), ones_ref) * v
    outs = (per_dir[0][0], per_dir[1][0], per_dir[0][1], per_dir[1][1], per_dir[0][2], per_dir[1][2], kk, r, v,
            bonus, g)
    for ref, val in zip(out_refs, outs):
        ref[...] = val.reshape(tt, nbb, RW_DIM)


def _rw_prep(h_seq, seq0, nb, t, w_rw, lora_w, lora_b, g2, lp, ones):
    nbb = nb if nb * 16 <= _PREP_ROWS else SCAN_NB
    tt = _PREP_ROWS // nbb
    assert seq0 % nbb == 0 and nb % nbb == 0 and t % tt == 0 and tt % 16 == 0
    b0 = seq0 // nbb
    rows = np.arange(_PREP_ROWS)
    perm = np.zeros((_PREP_ROWS, _PREP_ROWS), np.float32)
    perm[rows, (rows % nbb) * tt + rows // nbb] = 1.0

    def full(shape):
        return pl.BlockSpec(shape, lambda i, c: (0,) * len(shape))

    return pl.pallas_call(
        functools.partial(_rw_prep_kernel, nbb=nbb, tt=tt),
        out_shape=(jax.ShapeDtypeStruct((t, nb, RW_DIM), F32),) * _PREP_OUTS,
        grid=(nb // nbb, t // tt),
        in_specs=[pl.BlockSpec((nbb, tt, D_MODEL), lambda i, c: (b0 + i, c, 0)), full(perm.shape),
                  full(w_rw.shape), full(lora_w.shape), full((1, 4 * RW_DIM)), full(g2.shape),
                  full((1, RW_DIM)), full((1, RW_DIM)), full((1, RW_DIM)), full((RW_DIM, RW_DIM))],
        out_specs=(pl.BlockSpec((tt, nbb, RW_DIM), lambda i, c: (c, i, 0)),) * _PREP_OUTS,
        compiler_params=_params("parallel", "arbitrary"),
        name="rwkv_prep",
    )(h_seq, jnp.asarray(perm, BF16), w_rw, lora_w, lora_b.reshape(1, 4 * RW_DIM), g2, lp["kk"].reshape(1, RW_DIM),
      lp["ka"].reshape(1, RW_DIM), lp["rk"].reshape(1, RW_DIM), ones)


_SCAN_OPS = 6


def _half_swap(x):
    return pltpu.roll(x, LANES // 2, 1)


def _rw_scan_kernel(*refs, n_src, n_keep, period, reverse):
    nb = SCAN_NB
    tc = SCAN_TC
    ins = [refs[s * _SCAN_OPS:(s + 1) * _SCAN_OPS] for s in range(n_src)]
    s0_ref = refs[n_src * _SCAN_OPS]
    y_refs = refs[n_src * _SCAN_OPS + 1:n_src * _SCAN_OPS + 1 + n_src]
    sf_keep_ref, sf_reset_ref, ops_scr, y_scr, s_scr = refs[n_src * _SCAN_OPS + 1 + n_src:]
    c = pl.program_id(0)
    keep_lanes = n_keep * RW_HEADS * nb
    left = lax.broadcasted_iota(jnp.int32, (nb, LANES), 1) < LANES // 2

    @pl.when(c == 0)
    def _():
        s_scr[...] = s0_ref[...]
        y_scr[...] = jnp.zeros_like(y_scr)

    @pl.when((c > 0) & (c % period == 0))
    def _():
        keep = lax.broadcasted_iota(jnp.int32, (1, 1, LANES), 2) < keep_lanes
        s_scr[...] = jnp.where(keep, s_scr[...], 0.0)

    def times(q, s):
        return (tc - 1 - 2 * q, tc - 2 - 2 * q) if reverse[s] else (2 * q, 2 * q + 1)

    def relayout_in(q, ops):
        for o in range(_SCAN_OPS):
            rows = []
            for s in range(n_src):
                t0, t1 = times(q, s)
                a0 = ins[s][o][t0]
                a1 = ins[s][o][t1]
                for h in range(RW_HEADS):
                    j = h // 2
                    p0 = a0[:, LANES * j:LANES * (j + 1)]
                    p1 = a1[:, LANES * j:LANES * (j + 1)]
                    if h % 2 == 0:
                        rows.append(jnp.where(left, p0, _half_swap(p1)))
                    else:
                        rows.append(jnp.where(left, _half_swap(p0), p1))
            tile = jnp.concatenate(rows, axis=0).T
            ops[o, 0] = tile[0:HEAD_DIM]
            ops[o, 1] = tile[HEAD_DIM:]

    def steps(ops, y):
        for t in range(2):
            for vi in range(HEAD_DIM):
                s = s_scr[vi]
                sa = jnp.sum(s * ops[1, t], axis=0, keepdims=True)
                s = s * ops[0, t] - sa * ops[2, t] + ops[5, t, pl.ds(vi, 1), :] * ops[3, t]
                s_scr[vi] = s
                y[t, pl.ds(vi, 1), :] = jnp.sum(s * ops[4, t], axis=0, keepdims=True)

    def relayout_out(q, y):
        tile = jnp.concatenate([y[0], y[1]], axis=0).T
        for s in range(n_src):
            t0, t1 = times(q, s)
            for j in range(RW_HEADS // 2):
                r0 = (s * RW_HEADS + 2 * j) * nb
                even = tile[r0:r0 + nb]
                odd = tile[r0 + nb:r0 + 2 * nb]
                y_refs[s][t0, :, LANES * j:LANES * (j + 1)] = jnp.where(left, even, _half_swap(odd))
                y_refs[s][t1, :, LANES * j:LANES * (j + 1)] = jnp.where(left, _half_swap(even), odd)

    n_pairs = tc // 2
    relayout_in(0, ops_scr.at[0])

    def body(q, carry):
        par = q % 2
        relayout_out(jnp.maximum(q - 1, 0), y_scr.at[1 - par])
        steps(ops_scr.at[par], y_scr.at[par])
        relayout_in(jnp.minimum(q + 1, n_pairs - 1), ops_scr.at[1 - par])
        return carry

    lax.fori_loop(0, n_pairs, body, 0)
    relayout_out(n_pairs - 1, y_scr.at[(n_pairs - 1) % 2])

    @pl.when(c % period == period - 1)
    def _():
        sf_reset_ref[...] = s_scr[...]

    @pl.when(c == pl.num_programs(0) - 1)
    def _():
        sf_keep_ref[...] = s_scr[...]


def _rw_scan(keep_srcs, reset_srcs, s0, t_keep, t_reset, reverse):
    tc, nb = SCAN_TC, SCAN_NB
    n_keep = len(keep_srcs)
    n_src = n_keep + len(reset_srcs)
    nc = t_keep // tc
    period = t_reset // tc
    n_phase = nc // period

    def idx(s, c):
        if s < n_keep:
            return 0, (nc - 1 - c if reverse[s] else c)
        cc = c % period
        return c // period, (period - 1 - cc if reverse[s] else cc)

    y_specs = tuple(pl.BlockSpec((tc, nb, RW_DIM), lambda c, s=s: (idx(s, c)[1], idx(s, c)[0], 0))
                    for s in range(n_src))
    in_specs, args = [], []
    for s, src in enumerate(list(keep_srcs) + list(reset_srcs)):
        in_specs += [y_specs[s]] * _SCAN_OPS
        args += list(src)
    st = pl.BlockSpec((HEAD_DIM, HEAD_DIM, LANES), lambda c: (0, 0, 0))
    y_shapes = tuple(jax.ShapeDtypeStruct((t_keep, nb, RW_DIM) if s < n_keep else (t_reset, nb * n_phase, RW_DIM), F32)
                     for s in range(n_src))
    out = pl.pallas_call(
        functools.partial(_rw_scan_kernel, n_src=n_src, n_keep=n_keep, period=period, reverse=tuple(reverse)),
        out_shape=y_shapes + (jax.ShapeDtypeStruct((HEAD_DIM, HEAD_DIM, LANES), F32),
                              jax.ShapeDtypeStruct((n_phase, HEAD_DIM, HEAD_DIM, LANES), F32)),
        grid=(nc,),
        in_specs=in_specs + [st],
        out_specs=y_specs + (st, pl.BlockSpec((None, HEAD_DIM, HEAD_DIM, LANES), lambda c: (c // period, 0, 0, 0))),
        scratch_shapes=[pltpu.VMEM((2, _SCAN_OPS, 2, HEAD_DIM, LANES), F32),
                        pltpu.VMEM((2, 2, HEAD_DIM, LANES), F32),
                        pltpu.VMEM((HEAD_DIM, HEAD_DIM, LANES), F32)],
        compiler_params=_params("arbitrary"),
        name="rwkv_scan",
    )(*args, s0)
    return out[:n_src], out[n_src], out[n_src + 1]


def _rw_post_kernel(yf_ref, yb_ref, bonus_ref, g_ref, gw_ref, gb_ref, ones_ref, o_ref, stage_scr, *, nb):
    tt = SCAN_TC
    inv = 1.0 / HEAD_DIM
    y = (yf_ref[...] + yb_ref[...]).reshape(tt * nb, RW_DIM)
    mu = _head_sum(y, ones_ref) * inv
    d = y - mu
    var = _head_sum(d * d, ones_ref) * inv
    yn = d * lax.rsqrt(var + GN_EPS) * gw_ref[...] + gb_ref[...]
    out = (yn + bonus_ref[...].reshape(tt * nb, RW_DIM)) * g_ref[...].reshape(tt * nb, RW_DIM)
    stage_scr[...] = out.reshape(tt, nb, RW_DIM)

    def body(b, carry):
        o_ref[b] = stage_scr[:, b, :]
        return carry

    lax.fori_loop(0, nb, body, 0)


def _rw_post(y_f, y_b, bonus, g, gn_w, gn_b, ones, nb, t):
    tt = SCAN_TC
    tmaj = pl.BlockSpec((tt, nb, RW_DIM), lambda c: (c, 0, 0))
    vec = pl.BlockSpec((1, RW_DIM), lambda c: (0, 0))
    out = pl.pallas_call(
        functools.partial(_rw_post_kernel, nb=nb),
        out_shape=jax.ShapeDtypeStruct((nb, t, RW_DIM), F32),
        grid=(t // tt,),
        in_specs=[tmaj, tmaj, tmaj, tmaj, vec, vec, pl.BlockSpec((RW_DIM, RW_DIM), lambda c: (0, 0))],
        out_specs=pl.BlockSpec((nb, tt, RW_DIM), lambda c: (0, c, 0)),
        scratch_shapes=[pltpu.VMEM((tt, nb, RW_DIM), F32)],
        compiler_params=_params("parallel"),
        name="rwkv_post",
    )(y_f, y_b, bonus, g, gn_w.reshape(1, RW_DIM), gn_b.reshape(1, RW_DIM), ones)
    return out.reshape(nb * t, RW_DIM)


def _scan_state(states):
    z = jnp.concatenate([s.transpose(2, 3, 1, 0).reshape(HEAD_DIM, HEAD_DIM, -1) for s in states], axis=-1)
    return jnp.pad(z, ((0, 0), (0, 0), (0, LANES - z.shape[-1])))


def _unscan_reset_state(sf, src):
    n = RW_HEADS * SCAN_NB
    z = sf[:, :, :, src * n:(src + 1) * n].reshape(-1, HEAD_DIM, HEAD_DIM, RW_HEADS, SCAN_NB)
    return z.transpose(0, 4, 3, 1, 2).reshape(-1, RW_HEADS, HEAD_DIM, HEAD_DIM)


def _rwkv_operands(h, seq0, nb, t, *args):
    dw_f, dw_b, kd_f, kd_b, b_f, b_b, kk, r, v, bonus, g = _rw_prep(h.reshape(-1, t, D_MODEL), seq0, nb, t, *args)
    return (dw_f, kk, b_f, kd_f, r, v), (dw_b, kk, b_b, kd_b, r, v), bonus, g


def _rwkv(h, w_rw, lora_w, lora_b, g2, lp, ones, s0_f, s0_b):
    assert DEC_BATCH == SCAN_NB and BATCH * SEQ == DEC_BATCH * DEC_SEQ
    args = (w_rw, lora_w, lora_b, g2, lp, ones)
    c_f, c_b, c_bonus, c_g = _rwkv_operands(h, 0, BATCH, SEQ, *args)
    l_f, l_b, l_bonus, l_g = _rwkv_operands(h, N_CTX // DEC_SEQ, DEC_BATCH, DEC_SEQ, *args)
    (yl_f, yl_b, yc_f, yc_b), _, sf_ctx = _rw_scan([l_f, l_b], [c_f, c_b], _scan_state([s0_f, s0_b]),
                                                   DEC_SEQ, SEQ, [False, True, False, True])
    post = (lp["gn_w"], lp["gn_b"], ones)
    rw_ctx = _rw_post(yc_f, yc_b, c_bonus, c_g, *post, BATCH, SEQ)
    rw_lat = _rw_post(yl_f, yl_b, l_bonus, l_g, *post, DEC_BATCH, DEC_SEQ)
    return rw_ctx, rw_lat, _unscan_reset_state(sf_ctx, 2), _unscan_reset_state(sf_ctx, 3)


_TM_MERGE = 512


def _merge_kernel(x_ref, gm_ref, h_ref, nac_ref, nal_ref, rwc_ref, rwl_ref, mlc_ref, mll_ref, wg_ref, wbr_ref, wo_ref,
                  o_ref):
    is_ctx = pl.program_id(0) < N_CTX // _TM_MERGE
    h = h_ref[...]
    branches = tuple(jnp.where(is_ctx, c_ref[...], l_ref[...])
                     for c_ref, l_ref in ((nac_ref, nal_ref), (rwc_ref, rwl_ref), (mlc_ref, mll_ref)))
    m = None
    for i, o_b in enumerate(branches):
        gate = jnp.dot(h, wg_ref[:, i * D_MODEL:(i + 1) * D_MODEL], preferred_element_type=F32)
        br = jnp.dot(o_b.astype(BF16), wbr_ref[i], preferred_element_type=F32)
        t = _sigmoid(gate) * br
        m = t if m is None else m + t
    mix = jnp.dot(m.astype(BF16), wo_ref[...], preferred_element_type=F32)
    o_ref[...] = x_ref[...] + gm_ref[...] * mix


def _merge(x, mod_l, h, na_ctx, na_lat, rw_ctx, rw_lat, mla_ctx, mla_lat, w_gate, w_br, w_o):
    tm = _TM_MERGE
    n_ctx = N_CTX // tm
    row = pl.BlockSpec((tm, D_MODEL), lambda i: (i, 0))
    br_ctx = pl.BlockSpec((tm, BRANCH_DIM), lambda i: (jnp.minimum(i, n_ctx - 1), 0))
    br_lat = pl.BlockSpec((tm, BRANCH_DIM), lambda i: (jnp.maximum(i - n_ctx, 0), 0))
    return pl.pallas_call(
        _merge_kernel,
        out_shape=jax.ShapeDtypeStruct((N_TOK, D_MODEL), F32),
        grid=(N_TOK // tm,),
        in_specs=[row, pl.BlockSpec((None, 1, D_MODEL), lambda i: (_cond_row(i, tm), 0, 5)),
                  row, br_ctx, br_lat, br_ctx, br_lat, br_ctx, br_lat,
                  pl.BlockSpec((D_MODEL, N_BRANCH * D_MODEL), lambda i: (0, 0)),
                  pl.BlockSpec((N_BRANCH, BRANCH_DIM, D_MODEL), lambda i: (0, 0, 0)),
                  pl.BlockSpec((D_MODEL, D_MODEL), lambda i: (0, 0))],
        out_specs=row,
        compiler_params=_params("arbitrary"),
        name="merge",
    )(x, mod_l, h, na_ctx, na_lat, rw_ctx, rw_lat, mla_ctx, mla_lat, w_gate, w_br, w_o)


def _block_diag(blocks):
    rows = sum(b.shape[0] for b in blocks)
    cols = sum(b.shape[1] for b in blocks)
    out = jnp.zeros((rows, cols), blocks[0].dtype)
    r = c = 0
    for b in blocks:
        out = lax.dynamic_update_slice(out, b, (r, c))
        r += b.shape[0]
        c += b.shape[1]
    return out


def _pad_cols(w, left, total):
    return jnp.pad(w, ((0, 0), (left, total - left - w.shape[1])))


def _mla_uq_layout(w_uq):
    per = NOPE_DIM + ROPE_DIM
    return jnp.concatenate([_pad_cols(w_uq[:, h * per:(h + 1) * per], 0, LANES) for h in range(MLA_HEADS)], axis=1)


def _mla_ukv_layout(w_ukv):
    per = NOPE_DIM + V_DIM
    k = [_pad_cols(w_ukv[:, h * per:h * per + NOPE_DIM], 0, LANES) for h in range(MLA_HEADS)]
    v = [w_ukv[:, h * per + NOPE_DIM:(h + 1) * per] for h in range(MLA_HEADS)]
    return jnp.concatenate(k + v, axis=1)


def kernel(x_prompt, x_sample, c, cache_na_k, cache_na_v, cache_mla_ckv, cache_mla_krope, state_rwkv_fwd, state_rwkv_bwd, c_ctx, ada_w, ada_b, norm_g, ffn_wg, ffn_wu, ffn_wd, w_in, na_rpb, rw_w0, rw_w2, rw_a0, rw_a2, rw_g2, rw_kk, rw_ka, rw_rk, rw_gn_w, rw_gn_b, mla_qn_g, mla_kvn_g, mla_w_uq, mla_w_ukv, w_br, w_o, final_g):
    x = (x_prompt.reshape(N_CTX, D_MODEL), x_sample.reshape(N_LAT, D_MODEL))
    cond = jnp.concatenate([c_ctx[None, :], c, jnp.zeros((N_COND - 1 - DEC_BATCH, D_MODEL), F32)], axis=0)
    mod = _modulation(cond, ada_w, ada_b)

    ones = jnp.asarray(np.kron(np.eye(RW_HEADS), np.ones((HEAD_DIM, HEAD_DIM))), BF16)
    rope = _rope_tables(DEC_SEQ)

    col_rw = 3 * NA_HEADS * HEAD_DIM
    col_mla = col_rw + 3 * RW_DIM + 2 * DECAY_LORA + 2 * ICLR_LORA + GATE_LORA
    col_kr = col_mla + Q_LORA + KV_LORA
    col_gate = col_kr + ROPE_DIM

    caches = []
    na_kv = None
    y = None
    for l in range(DEPTH):
        mod_l = mod[l]
        wi = w_in[l]
        w_na = wi[:, :col_rw].astype(BF16)
        w_rw = wi[:, col_rw:col_mla].astype(BF16)
        w_mla = jnp.concatenate([wi[:, col_mla:col_kr], _pad_cols(wi[:, col_kr:col_gate], _ROPE_LO, LANES)],
                                axis=1).astype(BF16)
        w_gate = wi[:, col_gate:].astype(BF16)

        x, h = _ffn(x, norm_g[l, 0], mod_l, 0, ffn_wg[l, 0].astype(BF16), ffn_wu[l, 0].astype(BF16),
                    ffn_wd[l, 0].astype(BF16), post="mod", post_g=norm_g[l, 1], post_chunk=3)

        z_na = _mm(h, w_na, name="proj_na")
        na_ctx, *na_kv = _na_ctx(z_na, l, na_kv)
        na_lat = _na_latent(na_rpb[l].reshape(NA_HEADS * (2 * NA_WIN_H - 1), 2 * NA_WIN_W - 1), z_na,
                            cache_na_k, cache_na_v, l)

        lora_w = _block_diag([rw_w2[l, 0], rw_w2[l, 1], rw_a2[l, 0], rw_a2[l, 1]]).astype(BF16)
        lora_b = jnp.concatenate([rw_w0[l, 0], rw_w0[l, 1], rw_a0[l, 0], rw_a0[l, 1]])
        lp = {"kk": rw_kk[l], "ka": rw_ka[l], "rk": rw_rk[l].reshape(RW_DIM), "gn_w": rw_gn_w[l], "gn_b": rw_gn_b[l]}
        rw_ctx, rw_lat, sc_f, sc_b = _rwkv(h, w_rw, lora_w, lora_b, rw_g2[l].astype(BF16), lp, ones,
                                           state_rwkv_fwd[:, l], state_rwkv_bwd[:, l])

        w_ukv = _mla_ukv_layout(mla_w_ukv[l]).astype(BF16)
        q_m, kv_m, ckv, kr_m = _mla_prep(h, w_mla, mla_qn_g[l], mla_kvn_g[l],
                                         _mla_uq_layout(mla_w_uq[l]).astype(BF16), w_ukv)
        kv_cache = _mm(cache_mla_ckv[:, l].reshape(DEC_BATCH * PAST_LEN, KV_LORA), w_ukv, "mla_ukv_cache")
        kr_cache = _pad_cols(cache_mla_krope[:, l].reshape(DEC_BATCH * PAST_LEN, ROPE_DIM), _ROPE_LO, LANES)
        mla_ctx = _mla_ctx(q_m, kv_m, kr_m)
        mla_lat = _mla_latent(q_m, kv_m, kr_m, kv_cache, kr_cache, rope)

        x = _merge(x, mod_l, h, na_ctx, na_lat, rw_ctx, rw_lat, mla_ctx, mla_lat, w_gate,
                   w_br[l].astype(BF16), w_o[l].astype(BF16))

        ffn2 = (x, norm_g[l, 2], mod_l, 6, ffn_wg[l, 1].astype(BF16), ffn_wu[l, 1].astype(BF16),
                ffn_wd[l, 1].astype(BF16))
        if l == DEPTH - 1:
            y = _ffn(*ffn2, post="plain", post_g=final_g)
        else:
            x = _ffn(*ffn2)

        caches.append((ckv[:N_CTX].reshape(BATCH, SEQ, KV_LORA),
                       kr_m[:N_CTX, _ROPE_LO:_ROPE_LO + ROPE_DIM].reshape(BATCH, SEQ, ROPE_DIM),
                       sc_f, sc_b))

    y_prompt = y[0].reshape(BATCH, SEQ, D_MODEL)
    y_sample = y[1].reshape(DEC_BATCH, DEC_SEQ, D_MODEL)
    outs = [jnp.stack([cl[i] for cl in caches], axis=1) for i in range(4)]
    return (y_prompt, y_sample, *na_kv, *outs)
```

```python
import functools

import numpy as np
import jax
import jax.numpy as jnp
from jax import lax
from jax.experimental import pallas as pl
from jax.experimental.pallas import tpu as pltpu

F32 = jnp.float32
BF16 = jnp.bfloat16

D_MODEL = 1024
BATCH = 32
SEQ = 256
DEPTH = 2
DEC_BATCH = 8
DEC_SEQ = 1024
PAST_LEN = 256
GRID_W = 64
HEAD_DIM = 64
NA_HEADS = 4
NA_WIN_H = 8
NA_WIN_W = 16
RW_HEADS = 4
RW_DIM = RW_HEADS * HEAD_DIM
DECAY_LORA = 64
ICLR_LORA = 64
GATE_LORA = 128
GN_EPS = 64e-5
MLA_HEADS = 4
Q_LORA = 256
KV_LORA = 128
NOPE_DIM = 64
ROPE_DIM = 32
V_DIM = 64
MLA_SCALE = (NOPE_DIM + ROPE_DIM) ** -0.5
ROPE_BASE = 10000.0
N_BRANCH = 3
BRANCH_DIM = 256
D_FF = 2816
N_MOD = 9
NORM_EPS = 1e-6
NEG_INF = -1e30

N_CTX = BATCH * SEQ
N_LAT = DEC_BATCH * DEC_SEQ
N_TOK = N_CTX + N_LAT
N_COND = 16

LANES = 128
VMEM_LIMIT = 56 * 1024 * 1024

TM = 1024
TM_FFN = 512
SCAN_TC = 64


def _cond_row(i, tm):
    return jnp.where(i * tm < N_CTX, 0, 1 + (i * tm - N_CTX) // DEC_SEQ)


def _sigmoid(x):
    return 1.0 / (1.0 + jnp.exp(-x))


def _params(*sem):
    return pltpu.CompilerParams(dimension_semantics=sem, vmem_limit_bytes=VMEM_LIMIT)


def _mm_kernel(a_ref, w_ref, o_ref):
    o_ref[...] = jnp.dot(a_ref[...].astype(BF16), w_ref[...], preferred_element_type=F32)


def _mm(a, w, name):
    m, k = a.shape
    n = w.shape[1]
    tm = min(m, TM)
    assert m % tm == 0 and w.shape[0] == k
    return pl.pallas_call(
        _mm_kernel,
        out_shape=jax.ShapeDtypeStruct((m, n), F32),
        grid=(m // tm,),
        in_specs=[pl.BlockSpec((tm, k), lambda i: (i, 0)), pl.BlockSpec((k, n), lambda i: (0, 0))],
        out_specs=pl.BlockSpec((tm, n), lambda i: (i, 0)),
        compiler_params=_params("parallel"),
        name=name,
    )(a, w)


def _mod_kernel(c_ref, w_ref, b_ref, o_ref):
    c = c_ref[...]
    a = (c * _sigmoid(c)).astype(BF16)
    o_ref[...] = jnp.dot(a, w_ref[...].astype(BF16), preferred_element_type=F32) + b_ref[...]


def _modulation(cond, ada_w, ada_b):
    n = N_MOD * D_MODEL
    tn = 3 * D_MODEL
    out = pl.pallas_call(
        _mod_kernel,
        out_shape=jax.ShapeDtypeStruct((DEPTH, N_COND, n), F32),
        grid=(DEPTH, n // tn),
        in_specs=[pl.BlockSpec((N_COND, D_MODEL), lambda l, j: (0, 0)),
                  pl.BlockSpec((None, D_MODEL, tn), lambda l, j: (l, 0, j)),
                  pl.BlockSpec((None, 1, tn), lambda l, j: (l, 0, j))],
        out_specs=pl.BlockSpec((None, N_COND, tn), lambda l, j: (l, 0, j)),
        compiler_params=_params("parallel", "arbitrary"),
        name="modulation",
    )(cond, ada_w, ada_b.reshape(DEPTH, 1, n))
    return out.reshape(DEPTH, N_COND, 1, n)


def _rms(x, g):
    return x * lax.rsqrt(jnp.mean(x * x, axis=-1, keepdims=True) + NORM_EPS) * g


_FFN_CTX_TILES = N_CTX // TM_FFN


def _ffn_kernel(*refs, post, split_in):
    n_x = 2 if split_in else 1
    x_refs = refs[:n_x]
    g_ref, sh_ref, sc_ref, gt_ref, wg_ref, wu_ref, wd_ref = refs[n_x:n_x + 7]
    rest = refs[n_x + 7:]
    is_ctx = pl.program_id(0) < _FFN_CTX_TILES
    x = jnp.where(is_ctx, x_refs[0][...], x_refs[1][...]) if split_in else x_refs[0][...]
    h = (_rms(x, g_ref[...]) * (1.0 + sc_ref[...]) + sh_ref[...]).astype(BF16)
    gg = jnp.dot(h, wg_ref[...], preferred_element_type=F32)
    uu = jnp.dot(h, wu_ref[...], preferred_element_type=F32)
    a = (gg * _sigmoid(gg)) * uu
    xn = x + 0.5 * gt_ref[...] * jnp.dot(a.astype(BF16), wd_ref[...], preferred_element_type=F32)
    if post == "plain":
        g2_ref, ya_ref, yb_ref = rest
        y = _rms(xn, g2_ref[...])

        @pl.when(is_ctx)
        def _():
            ya_ref[...] = y

        @pl.when(jnp.logical_not(is_ctx))
        def _():
            yb_ref[...] = y
    elif post == "mod":
        g2_ref, sh2_ref, sc2_ref, o_ref, h2_ref = rest
        o_ref[...] = xn
        h2_ref[...] = (_rms(xn, g2_ref[...]) * (1.0 + sc2_ref[...]) + sh2_ref[...]).astype(BF16)
    else:
        (o_ref,) = rest
        o_ref[...] = xn


def _ffn(x, norm_g, mod_l, chunk0, wg, wu, wd, post=None, post_g=None, post_chunk=None):
    def mod_spec(c):
        return pl.BlockSpec((None, 1, D_MODEL), lambda i: (_cond_row(i, TM_FFN), 0, c))

    def weight(shape):
        return pl.BlockSpec(shape, lambda i: (0, 0), pipeline_mode=pl.Buffered(1))

    row = pl.BlockSpec((TM_FFN, D_MODEL), lambda i: (i, 0))
    row_ctx = pl.BlockSpec((TM_FFN, D_MODEL), lambda i: (jnp.minimum(i, _FFN_CTX_TILES - 1), 0))
    row_lat = pl.BlockSpec((TM_FFN, D_MODEL), lambda i: (jnp.maximum(i - _FFN_CTX_TILES, 0), 0))
    vec = pl.BlockSpec((1, D_MODEL), lambda i: (0, 0))
    split_in = isinstance(x, tuple)
    in_specs = ([row_ctx, row_lat] if split_in else [row]) + [
        vec, mod_spec(chunk0), mod_spec(chunk0 + 1), mod_spec(chunk0 + 2),
        weight((D_MODEL, D_FF)), weight((D_MODEL, D_FF)), weight((D_FF, D_MODEL))]
    args = (list(x) if split_in else [x]) + [norm_g.reshape(1, D_MODEL), mod_l, mod_l, mod_l, wg, wu, wd]
    xs = jax.ShapeDtypeStruct((N_TOK, D_MODEL), F32)
    if post == "mod":
        in_specs += [vec, mod_spec(post_chunk), mod_spec(post_chunk + 1)]
        args += [post_g.reshape(1, D_MODEL), mod_l, mod_l]
        out_shape, out_specs = (xs, jax.ShapeDtypeStruct((N_TOK, D_MODEL), BF16)), (row, row)
    elif post == "plain":
        in_specs += [vec]
        args += [post_g.reshape(1, D_MODEL)]
        out_shape = (jax.ShapeDtypeStruct((N_CTX, D_MODEL), F32), jax.ShapeDtypeStruct((N_LAT, D_MODEL), F32))
        out_specs = (row_ctx, row_lat)
    else:
        out_shape, out_specs = xs, row
    return pl.pallas_call(
        functools.partial(_ffn_kernel, post=post, split_in=split_in),
        out_shape=out_shape,
        grid=(N_TOK // TM_FFN,),
        in_specs=in_specs,
        out_specs=out_specs,
        compiler_params=_params("arbitrary"),
        name="ffn",
    )(*args)


_NT = (((1,), (1,)), ((), ()))
_LAT_ROWS = DEC_SEQ // GRID_W
_NA_KH = min(NA_WIN_H, _LAT_ROWS)
_NA_LOCAL = _NA_KH * GRID_W
_NA_PAIRS = 2 * NA_WIN_H - 2


def _head_mask(width, h, group):
    lane = lax.broadcasted_iota(jnp.int32, (1, width), 1)
    return (lane >= h * group) & (lane < (h + 1) * group)


def _stack_heads(q, nh, group):
    return jnp.concatenate([jnp.where(_head_mask(q.shape[1], h, group), q, 0.0) for h in range(nh)], axis=0)


def _pick_heads(r, nh, group):
    t = r.shape[0] // nh
    out = r[0:t]
    for h in range(1, nh):
        out = jnp.where(_head_mask(r.shape[1], h, group), r[h * t:(h + 1) * t], out)
    return out


def _softmax2(s_a, s_b):
    m = jnp.maximum(jnp.max(s_a, axis=-1, keepdims=True), jnp.max(s_b, axis=-1, keepdims=True))
    p_a = jnp.exp(s_a - m)
    p_b = jnp.exp(s_b - m)
    return p_a, p_b, jnp.sum(p_a, axis=-1, keepdims=True) + jnp.sum(p_b, axis=-1, keepdims=True)


def _na_ctx_kernel(q_ref, k_ref, v_ref, *rest):
    o_ref, ko_ref, vo_ref = rest[-3:]
    k = k_ref[...]
    v = v_ref[...]
    q_st = _stack_heads(q_ref[...] * HEAD_DIM ** -0.5, NA_HEADS, HEAD_DIM).astype(BF16)
    s = lax.dot_general(q_st, k.astype(BF16), _NT, preferred_element_type=F32)
    p = jnp.exp(s - jnp.max(s, axis=-1, keepdims=True))
    l = jnp.sum(p, axis=-1, keepdims=True)
    r = jnp.dot(p.astype(BF16), v.astype(BF16), preferred_element_type=F32) / l
    o_ref[...] = _pick_heads(r, NA_HEADS, HEAD_DIM)
    for h in range(NA_HEADS):
        ko_ref[h] = k[:, h * HEAD_DIM:(h + 1) * HEAD_DIM]
        vo_ref[h] = v[:, h * HEAD_DIM:(h + 1) * HEAD_DIM]


def _na_ctx(z_na, layer, caches):
    w = NA_HEADS * HEAD_DIM
    cache_shape = jax.ShapeDtypeStruct((BATCH, DEPTH, NA_HEADS, SEQ, HEAD_DIM), F32)
    cache_spec = pl.BlockSpec((None, None, NA_HEADS, SEQ, HEAD_DIM), lambda b: (b, layer, 0, 0, 0))
    return pl.pallas_call(
        _na_ctx_kernel,
        out_shape=(jax.ShapeDtypeStruct((N_CTX, w), F32), cache_shape, cache_shape),
        grid=(BATCH,),
        in_specs=[pl.BlockSpec((SEQ, w), lambda b: (b, 0)),
                  pl.BlockSpec((SEQ, w), lambda b: (b, 1)),
                  pl.BlockSpec((SEQ, w), lambda b: (b, 2)),
                  pl.BlockSpec(memory_space=pl.ANY), pl.BlockSpec(memory_space=pl.ANY)],
        out_specs=(pl.BlockSpec((SEQ, w), lambda b: (b, 0)), cache_spec, cache_spec),
        input_output_aliases={3: 1, 4: 2},
        compiler_params=_params("parallel"),
        name="na_ctx",
    )(z_na, z_na, z_na, *caches)


def _na_row_lo(qr):
    return min(max(qr - NA_WIN_H // 2, 0), _LAT_ROWS - _NA_KH)


def _na_lat_kernel(rpb_ref, q_ref, k_ref, v_ref, kc_ref, vc_ref, o_ref, bias_scr):
    scale = HEAD_DIM ** -0.5

    @pl.when(pl.program_id(0) == 0)
    def _():
        qc = lax.broadcasted_iota(jnp.int32, (GRID_W, 2 * GRID_W), 0)
        lane = lax.broadcasted_iota(jnp.int32, (GRID_W, 2 * GRID_W), 1)
        kc = lane & (GRID_W - 1)
        col_lo = jnp.clip(qc - NA_WIN_W // 2, 0, GRID_W - NA_WIN_W)
        rel = jnp.where((kc >= col_lo) & (kc < col_lo + NA_WIN_W), kc - qc + NA_WIN_W - 1, -1)
        second = lax.broadcasted_iota(jnp.int32, (1, 2 * GRID_W), 1) >= GRID_W

        def build(idx, carry):
            h = idx // _NA_PAIRS
            row = h * (2 * NA_WIN_H - 1) + (idx - h * _NA_PAIRS)
            tile = jnp.full((GRID_W, 2 * GRID_W), NEG_INF, F32)
            for j in range(2 * NA_WIN_W - 1):
                val = jnp.where(second, rpb_ref[row + 1, j], rpb_ref[row, j])
                tile = jnp.where(rel == j, val, tile)
            bias_scr[idx] = tile
            return carry

        lax.fori_loop(0, NA_HEADS * _NA_PAIRS, build, 0)

    kc = jnp.concatenate([kc_ref[h] for h in range(NA_HEADS)], axis=1).astype(BF16)
    vc = jnp.concatenate([vc_ref[h] for h in range(NA_HEADS)], axis=1).astype(BF16)
    for qr in range(_LAT_ROWS):
        lo = _na_row_lo(qr)
        dr0 = lo - qr + NA_WIN_H - 1
        q_st = _stack_heads(q_ref[qr * GRID_W:(qr + 1) * GRID_W, :] * scale, NA_HEADS, HEAD_DIM).astype(BF16)
        kl = k_ref[lo * GRID_W:lo * GRID_W + _NA_LOCAL, :].astype(BF16)
        vl = v_ref[lo * GRID_W:lo * GRID_W + _NA_LOCAL, :].astype(BF16)
        bias = jnp.concatenate(
            [jnp.concatenate([bias_scr[h * _NA_PAIRS + dr0 + 2 * i] for i in range(_NA_KH // 2)], axis=1)
             for h in range(NA_HEADS)], axis=0)
        s_loc = lax.dot_general(q_st, kl, _NT, preferred_element_type=F32) + bias
        s_ctx = lax.dot_general(q_st, kc, _NT, preferred_element_type=F32)
        p_loc, p_ctx, l = _softmax2(s_loc, s_ctx)
        r = (jnp.dot(p_loc.astype(BF16), vl, preferred_element_type=F32)
             + jnp.dot(p_ctx.astype(BF16), vc, preferred_element_type=F32)) / l
        o_ref[qr * GRID_W:(qr + 1) * GRID_W, :] = _pick_heads(r, NA_HEADS, HEAD_DIM)


def _na_latent(rpb, z_na, kc, vc, layer):
    w = NA_HEADS * HEAD_DIM
    lat0 = N_CTX // DEC_SEQ

    def col(c):
        return pl.BlockSpec((DEC_SEQ, w), lambda b: (lat0 + b, c))

    cache = pl.BlockSpec((None, None, NA_HEADS, PAST_LEN, HEAD_DIM), lambda b: (b, layer, 0, 0, 0))
    return pl.pallas_call(
        _na_lat_kernel,
        out_shape=jax.ShapeDtypeStruct((N_LAT, w), F32),
        grid=(DEC_BATCH,),
        in_specs=[pl.BlockSpec(memory_space=pltpu.SMEM), col(0), col(1), col(2), cache, cache],
        out_specs=pl.BlockSpec((DEC_SEQ, w), lambda b: (b, 0)),
        scratch_shapes=[pltpu.VMEM((NA_HEADS * _NA_PAIRS, GRID_W, 2 * GRID_W), F32)],
        compiler_params=_params("arbitrary"),
        name="na_latent",
    )(rpb, z_na, z_na, z_na, kc, vc)


_MLA_W = MLA_HEADS * LANES
_ROPE_LO = NOPE_DIM
_ROPE_HALF = ROPE_DIM // 2


def _rope_tables(t_len):
    n_freq = ROPE_DIM // 4
    inv = 1.0 / (ROPE_BASE ** (np.arange(n_freq) / n_freq))
    pos = np.arange(t_len)
    ang = np.concatenate([(pos // GRID_W)[:, None] * inv, (pos % GRID_W)[:, None] * inv], axis=-1)
    cos, sin = np.cos(ang), np.sin(ang)
    c = np.ones((t_len, LANES))
    s1 = np.zeros((t_len, LANES))
    s2 = np.zeros((t_len, LANES))
    a, b, e = _ROPE_LO, _ROPE_LO + _ROPE_HALF, _ROPE_LO + ROPE_DIM
    c[:, a:b] = cos
    c[:, b:e] = cos
    s1[:, a:b] = -sin
    s2[:, b:e] = sin
    return tuple(jnp.asarray(t, F32) for t in (c, s1, s2))


def _apply_rope(x, c, s1, s2):
    up = pltpu.roll(x, LANES - _ROPE_HALF, 1)
    dn = pltpu.roll(x, _ROPE_HALF, 1)
    return x * c + up * s1 + dn * s2


def _mla_prep_kernel(h_ref, w_ref, qg_ref, kvg_ref, wuq_ref, wukv_ref, q_ref, kv_ref, ckv_ref, kr_ref):
    z = jnp.dot(h_ref[...], w_ref[...], preferred_element_type=F32)
    cq = _rms(z[:, :Q_LORA], qg_ref[...])
    q_ref[...] = jnp.dot(cq.astype(BF16), wuq_ref[...], preferred_element_type=F32)
    ckv = _rms(z[:, Q_LORA:Q_LORA + KV_LORA], kvg_ref[...])
    ckv_ref[...] = ckv
    kv_ref[...] = jnp.dot(ckv.astype(BF16), wukv_ref[...], preferred_element_type=F32)
    kr_ref[...] = z[:, Q_LORA + KV_LORA:]


def _mla_prep(h, w_mla, qn_g, kvn_g, w_uq, w_ukv):
    wv = MLA_HEADS * V_DIM
    row = lambda n: pl.BlockSpec((TM, n), lambda i: (i, 0))
    full = lambda a: pl.BlockSpec(a.shape, lambda i: (0, 0))
    qn_g = qn_g.reshape(1, Q_LORA)
    kvn_g = kvn_g.reshape(1, KV_LORA)
    return pl.pallas_call(
        _mla_prep_kernel,
        out_shape=(jax.ShapeDtypeStruct((N_TOK, _MLA_W), F32), jax.ShapeDtypeStruct((N_TOK, _MLA_W + wv), F32),
                   jax.ShapeDtypeStruct((N_TOK, KV_LORA), F32), jax.ShapeDtypeStruct((N_TOK, LANES), F32)),
        grid=(N_TOK // TM,),
        in_specs=[row(D_MODEL), full(w_mla), full(qn_g), full(kvn_g), full(w_uq), full(w_ukv)],
        out_specs=(row(_MLA_W), row(_MLA_W + wv), row(KV_LORA), row(LANES)),
        compiler_params=_params("parallel"),
        name="mla_prep",
    )(h, w_mla, qn_g, kvn_g, w_uq, w_ukv)


def _mla_ctx_kernel(q_ref, kv_ref, kr_ref, o_ref):
    kr = kr_ref[...]
    v_all = kv_ref[:, _MLA_W:].astype(BF16)
    out = None
    for h in range(MLA_HEADS):
        qh = (q_ref[:, h * LANES:(h + 1) * LANES] * MLA_SCALE).astype(BF16)
        kh = (kv_ref[:, h * LANES:(h + 1) * LANES] + kr).astype(BF16)
        s = lax.dot_general(qh, kh, _NT, preferred_element_type=F32)
        p = jnp.exp(s - jnp.max(s, axis=-1, keepdims=True))
        l = jnp.sum(p, axis=-1, keepdims=True)
        r = jnp.dot(p.astype(BF16), v_all, preferred_element_type=F32) / l
        out = r if out is None else jnp.where(_head_mask(MLA_HEADS * V_DIM, h, V_DIM), r, out)
    o_ref[...] = out


def _mla_ctx(q_m, kv_m, kr_m):
    wv = MLA_HEADS * V_DIM
    return pl.pallas_call(
        _mla_ctx_kernel,
        out_shape=jax.ShapeDtypeStruct((N_CTX, wv), F32),
        grid=(BATCH,),
        in_specs=[pl.BlockSpec((SEQ, _MLA_W), lambda b: (b, 0)),
                  pl.BlockSpec((SEQ, _MLA_W + wv), lambda b: (b, 0)),
                  pl.BlockSpec((SEQ, LANES), lambda b: (b, 0))],
        out_specs=pl.BlockSpec((SEQ, wv), lambda b: (b, 0)),
        compiler_params=_params("parallel"),
        name="mla_ctx",
    )(q_m, kv_m, kr_m)


_MLA_BQ = 256


def _mla_lat_kernel(q_ref, kv_ref, kr_ref, kvc_ref, krc_ref, cq_ref, s1q_ref, s2q_ref, ck_ref, s1k_ref, s2k_ref,
                    o_ref):
    kr = _apply_rope(kr_ref[...], ck_ref[...], s1k_ref[...], s2k_ref[...])
    krc = krc_ref[...]
    v_lat = kv_ref[:, _MLA_W:].astype(BF16)
    v_ctx = kvc_ref[:, _MLA_W:].astype(BF16)
    out = None
    for h in range(MLA_HEADS):
        blk = slice(h * LANES, (h + 1) * LANES)
        qh = (_apply_rope(q_ref[:, blk], cq_ref[...], s1q_ref[...], s2q_ref[...]) * MLA_SCALE).astype(BF16)
        kl = (kv_ref[:, blk] + kr).astype(BF16)
        kc = (kvc_ref[:, blk] + krc).astype(BF16)
        s_l = lax.dot_general(qh, kl, _NT, preferred_element_type=F32)
        s_c = lax.dot_general(qh, kc, _NT, preferred_element_type=F32)
        p_l, p_c, l = _softmax2(s_l, s_c)
        r = (jnp.dot(p_l.astype(BF16), v_lat, preferred_element_type=F32)
             + jnp.dot(p_c.astype(BF16), v_ctx, preferred_element_type=F32)) / l
        out = r if out is None else jnp.where(_head_mask(MLA_HEADS * V_DIM, h, V_DIM), r, out)
    o_ref[...] = out


def _mla_latent(q_m, kv_m, kr_m, kv_cache, kr_cache, rope):
    wv = MLA_HEADS * V_DIM
    nq = DEC_SEQ // _MLA_BQ
    q0 = N_CTX // _MLA_BQ
    lat0 = N_CTX // DEC_SEQ
    c, s1, s2 = rope
    tq = pl.BlockSpec((_MLA_BQ, LANES), lambda b, j: (j, 0))
    tk = pl.BlockSpec((DEC_SEQ, LANES), lambda b, j: (0, 0))
    return pl.pallas_call(
        _mla_lat_kernel,
        out_shape=jax.ShapeDtypeStruct((N_LAT, wv), F32),
        grid=(DEC_BATCH, nq),
        in_specs=[pl.BlockSpec((_MLA_BQ, _MLA_W), lambda b, j: (q0 + b * nq + j, 0)),
                  pl.BlockSpec((DEC_SEQ, _MLA_W + wv), lambda b, j: (lat0 + b, 0)),
                  pl.BlockSpec((DEC_SEQ, LANES), lambda b, j: (lat0 + b, 0)),
                  pl.BlockSpec((PAST_LEN, _MLA_W + wv), lambda b, j: (b, 0)),
                  pl.BlockSpec((PAST_LEN, LANES), lambda b, j: (b, 0)),
                  tq, tq, tq, tk, tk, tk],
        out_specs=pl.BlockSpec((_MLA_BQ, wv), lambda b, j: (b * nq + j, 0)),
        compiler_params=_params("parallel", "arbitrary"),
        name="mla_lat",
    )(q_m, kv_m, kr_m, kv_cache, kr_cache, c, s1, s2, c, s1, s2)


def _head_sum(x, ones_ref):
    ones = ones_ref[...]
    hi = x.astype(BF16)
    rest = x - hi.astype(F32)
    mid = rest.astype(BF16)
    lo = (rest - mid.astype(F32)).astype(BF16)
    return (jnp.dot(hi, ones, preferred_element_type=F32) + jnp.dot(mid, ones, preferred_element_type=F32)
            + jnp.dot(lo, ones, preferred_element_type=F32))


SCAN_NB = LANES // (4 * RW_HEADS)
_PREP_ROWS = 512
_PREP_OUTS = 11


def _lora_act(a):
    col = lax.broadcasted_iota(jnp.int32, a.shape, 1)
    return jnp.where(col < 2 * DECAY_LORA, jnp.tanh(a), a)


def _rw_prep_kernel(h_ref, perm_ref, w_ref, lw_ref, lb_ref, g2_ref, kkw_ref, ka_ref, rk_ref, ones_ref, *out_refs,
                    nbb, tt):
    h = jnp.dot(perm_ref[...], h_ref[...].reshape(nbb * tt, D_MODEL), preferred_element_type=F32).astype(BF16)
    z = jnp.dot(h, w_ref[...], preferred_element_type=F32)
    r = z[:, 0:RW_DIM]
    k = z[:, RW_DIM:2 * RW_DIM]
    v = z[:, 2 * RW_DIM:3 * RW_DIM]
    lora_in = z[:, 3 * RW_DIM:4 * RW_DIM]
    gate_in = z[:, 4 * RW_DIM:4 * RW_DIM + GATE_LORA]
    wa = jnp.dot(_lora_act(lora_in).astype(BF16), lw_ref[...], preferred_element_type=F32) + lb_ref[...]
    g = jnp.dot(_sigmoid(gate_in).astype(BF16), g2_ref[...], preferred_element_type=F32)
    kk = k * kkw_ref[...]
    kk = kk / jnp.maximum(jnp.sqrt(_head_sum(kk * kk, ones_ref)), 1e-12)
    per_dir = []
    for d in range(2):
        decay = jnp.exp(-float(np.exp(-0.5)) * _sigmoid(wa[:, d * RW_DIM:(d + 1) * RW_DIM]))
        a = _sigmoid(wa[:, (2 + d) * RW_DIM:(3 + d) * RW_DIM])
        per_dir.append((decay, k * (1.0 + (a - 1.0) * ka_ref[...]), kk * a))
    bonus = _head_sum(r * rk_ref[...] * (per_dir[0][1] + per_dir[1][1]), ones_ref) * v
    outs = (per_dir[0][0], per_dir[1][0], per_dir[0][1], per_dir[1][1], per_dir[0][2], per_dir[1][2], kk, r, v,
            bonus, g)
    for ref, val in zip(out_refs, outs):
        ref[...] = val.reshape(tt, nbb, RW_DIM)


def _rw_prep(h_seq, seq0, nb, t, w_rw, lora_w, lora_b, g2, lp, ones):
    nbb = nb if nb * 16 <= _PREP_ROWS else SCAN_NB
    tt = _PREP_ROWS // nbb
    assert seq0 % nbb == 0 and nb % nbb == 0 and t % tt == 0 and tt % 16 == 0
    b0 = seq0 // nbb
    rows = np.arange(_PREP_ROWS)
    perm = np.zeros((_PREP_ROWS, _PREP_ROWS), np.float32)
    perm[rows, (rows % nbb) * tt + rows // nbb] = 1.0

    def full(shape):
        return pl.BlockSpec(shape, lambda i, c: (0,) * len(shape))

    return pl.pallas_call(
        functools.partial(_rw_prep_kernel, nbb=nbb, tt=tt),
        out_shape=(jax.ShapeDtypeStruct((t, nb, RW_DIM), F32),) * _PREP_OUTS,
        grid=(nb // nbb, t // tt),
        in_specs=[pl.BlockSpec((nbb, tt, D_MODEL), lambda i, c: (b0 + i, c, 0)), full(perm.shape),
                  full(w_rw.shape), full(lora_w.shape), full((1, 4 * RW_DIM)), full(g2.shape),
                  full((1, RW_DIM)), full((1, RW_DIM)), full((1, RW_DIM)), full((RW_DIM, RW_DIM))],
        out_specs=(pl.BlockSpec((tt, nbb, RW_DIM), lambda i, c: (c, i, 0)),) * _PREP_OUTS,
        compiler_params=_params("parallel", "arbitrary"),
        name="rwkv_prep",
    )(h_seq, jnp.asarray(perm, BF16), w_rw, lora_w, lora_b.reshape(1, 4 * RW_DIM), g2, lp["kk"].reshape(1, RW_DIM),
      lp["ka"].reshape(1, RW_DIM), lp["rk"].reshape(1, RW_DIM), ones)


_SCAN_OPS = 6


def _half_swap(x):
    return pltpu.roll(x, LANES // 2, 1)


def _rw_scan_kernel(*refs, n_src, n_keep, period, reverse):
    nb = SCAN_NB
    tc = SCAN_TC
    ins = [refs[s * _SCAN_OPS:(s + 1) * _SCAN_OPS] for s in range(n_src)]
    s0_ref = refs[n_src * _SCAN_OPS]
    y_refs = refs[n_src * _SCAN_OPS + 1:n_src * _SCAN_OPS + 1 + n_src]
    sf_keep_ref, sf_reset_ref, ops_scr, y_scr, s_scr = refs[n_src * _SCAN_OPS + 1 + n_src:]
    c = pl.program_id(0)
    keep_lanes = n_keep * RW_HEADS * nb
    left = lax.broadcasted_iota(jnp.int32, (nb, LANES), 1) < LANES // 2

    @pl.when(c == 0)
    def _():
        s_scr[...] = s0_ref[...]
        y_scr[...] = jnp.zeros_like(y_scr)

    @pl.when((c > 0) & (c % period == 0))
    def _():
        keep = lax.broadcasted_iota(jnp.int32, (1, 1, LANES), 2) < keep_lanes
        s_scr[...] = jnp.where(keep, s_scr[...], 0.0)

    def times(q, s):
        return (tc - 1 - 2 * q, tc - 2 - 2 * q) if reverse[s] else (2 * q, 2 * q + 1)

    def relayout_in(q, ops):
        for o in range(_SCAN_OPS):
            rows = []
            for s in range(n_src):
                t0, t1 = times(q, s)
                a0 = ins[s][o][t0]
                a1 = ins[s][o][t1]
                for h in range(RW_HEADS):
                    j = h // 2
                    p0 = a0[:, LANES * j:LANES * (j + 1)]
                    p1 = a1[:, LANES * j:LANES * (j + 1)]
                    if h % 2 == 0:
                        rows.append(jnp.where(left, p0, _half_swap(p1)))
                    else:
                        rows.append(jnp.where(left, _half_swap(p0), p1))
            tile = jnp.concatenate(rows, axis=0).T
            ops[o, 0] = tile[0:HEAD_DIM]
            ops[o, 1] = tile[HEAD_DIM:]

    def steps(ops, y):
        for t in range(2):
            for vi in range(HEAD_DIM):
                s = s_scr[vi]
                sa = jnp.sum(s * ops[1, t], axis=0, keepdims=True)
                s = s * ops[0, t] - sa * ops[2, t] + ops[5, t, pl.ds(vi, 1), :] * ops[3, t]
                s_scr[vi] = s
                y[t, pl.ds(vi, 1), :] = jnp.sum(s * ops[4, t], axis=0, keepdims=True)

    def relayout_out(q, y):
        tile = jnp.concatenate([y[0], y[1]], axis=0).T
        for s in range(n_src):
            t0, t1 = times(q, s)
            for j in range(RW_HEADS // 2):
                r0 = (s * RW_HEADS + 2 * j) * nb
                even = tile[r0:r0 + nb]
                odd = tile[r0 + nb:r0 + 2 * nb]
                y_refs[s][t0, :, LANES * j:LANES * (j + 1)] = jnp.where(left, even, _half_swap(odd))
                y_refs[s][t1, :, LANES * j:LANES * (j + 1)] = jnp.where(left, _half_swap(even), odd)

    n_pairs = tc // 2
    relayout_in(0, ops_scr.at[0])

    def body(q, carry):
        par = q % 2
        relayout_out(jnp.maximum(q - 1, 0), y_scr.at[1 - par])
        steps(ops_scr.at[par], y_scr.at[par])
        relayout_in(jnp.minimum(q + 1, n_pairs - 1), ops_scr.at[1 - par])
        return carry

    lax.fori_loop(0, n_pairs, body, 0)
    relayout_out(n_pairs - 1, y_scr.at[(n_pairs - 1) % 2])

    @pl.when(c % period == period - 1)
    def _():
        sf_reset_ref[...] = s_scr[...]

    @pl.when(c == pl.num_programs(0) - 1)
    def _():
        sf_keep_ref[...] = s_scr[...]


def _rw_scan(keep_srcs, reset_srcs, s0, t_keep, t_reset, reverse):
    tc, nb = SCAN_TC, SCAN_NB
    n_keep = len(keep_srcs)
    n_src = n_keep + len(reset_srcs)
    nc = t_keep // tc
    period = t_reset // tc
    n_phase = nc // period

    def idx(s, c):
        if s < n_keep:
            return 0, (nc - 1 - c if reverse[s] else c)
        cc = c % period
        return c // period, (period - 1 - cc if reverse[s] else cc)

    y_specs = tuple(pl.BlockSpec((tc, nb, RW_DIM), lambda c, s=s: (idx(s, c)[1], idx(s, c)[0], 0))
                    for s in range(n_src))
    in_specs, args = [], []
    for s, src in enumerate(list(keep_srcs) + list(reset_srcs)):
        in_specs += [y_specs[s]] * _SCAN_OPS
        args += list(src)
    st = pl.BlockSpec((HEAD_DIM, HEAD_DIM, LANES), lambda c: (0, 0, 0))
    y_shapes = tuple(jax.ShapeDtypeStruct((t_keep, nb, RW_DIM) if s < n_keep else (t_reset, nb * n_phase, RW_DIM), F32)
                     for s in range(n_src))
    out = pl.pallas_call(
        functools.partial(_rw_scan_kernel, n_src=n_src, n_keep=n_keep, period=period, reverse=tuple(reverse)),
        out_shape=y_shapes + (jax.ShapeDtypeStruct((HEAD_DIM, HEAD_DIM, LANES), F32),
                              jax.ShapeDtypeStruct((n_phase, HEAD_DIM, HEAD_DIM, LANES), F32)),
        grid=(nc,),
        in_specs=in_specs + [st],
        out_specs=y_specs + (st, pl.BlockSpec((None, HEAD_DIM, HEAD_DIM, LANES), lambda c: (c // period, 0, 0, 0))),
        scratch_shapes=[pltpu.VMEM((2, _SCAN_OPS, 2, HEAD_DIM, LANES), F32),
                        pltpu.VMEM((2, 2, HEAD_DIM, LANES), F32),
                        pltpu.VMEM((HEAD_DIM, HEAD_DIM, LANES), F32)],
        compiler_params=_params("arbitrary"),
        name="rwkv_scan",
    )(*args, s0)
    return out[:n_src], out[n_src], out[n_src + 1]


def _rw_post_kernel(yf_ref, yb_ref, bonus_ref, g_ref, gw_ref, gb_ref, ones_ref, o_ref, stage_scr, *, nb):
    tt = SCAN_TC
    inv = 1.0 / HEAD_DIM
    y = (yf_ref[...] + yb_ref[...]).reshape(tt * nb, RW_DIM)
    mu = _head_sum(y, ones_ref) * inv
    d = y - mu
    var = _head_sum(d * d, ones_ref) * inv
    yn = d * lax.rsqrt(var + GN_EPS) * gw_ref[...] + gb_ref[...]
    out = (yn + bonus_ref[...].reshape(tt * nb, RW_DIM)) * g_ref[...].reshape(tt * nb, RW_DIM)
    stage_scr[...] = out.reshape(tt, nb, RW_DIM)

    def body(b, carry):
        o_ref[b] = stage_scr[:, b, :]
        return carry

    lax.fori_loop(0, nb, body, 0)


def _rw_post(y_f, y_b, bonus, g, gn_w, gn_b, ones, nb, t):
    tt = SCAN_TC
    tmaj = pl.BlockSpec((tt, nb, RW_DIM), lambda c: (c, 0, 0))
    vec = pl.BlockSpec((1, RW_DIM), lambda c: (0, 0))
    out = pl.pallas_call(
        functools.partial(_rw_post_kernel, nb=nb),
        out_shape=jax.ShapeDtypeStruct((nb, t, RW_DIM), F32),
        grid=(t // tt,),
        in_specs=[tmaj, tmaj, tmaj, tmaj, vec, vec, pl.BlockSpec((RW_DIM, RW_DIM), lambda c: (0, 0))],
        out_specs=pl.BlockSpec((nb, tt, RW_DIM), lambda c: (0, c, 0)),
        scratch_shapes=[pltpu.VMEM((tt, nb, RW_DIM), F32)],
        compiler_params=_params("parallel"),
        name="rwkv_post",
    )(y_f, y_b, bonus, g, gn_w.reshape(1, RW_DIM), gn_b.reshape(1, RW_DIM), ones)
    return out.reshape(nb * t, RW_DIM)


def _scan_state(states):
    z = jnp.concatenate([s.transpose(2, 3, 1, 0).reshape(HEAD_DIM, HEAD_DIM, -1) for s in states], axis=-1)
    return jnp.pad(z, ((0, 0), (0, 0), (0, LANES - z.shape[-1])))


def _unscan_reset_state(sf, src):
    n = RW_HEADS * SCAN_NB
    z = sf[:, :, :, src * n:(src + 1) * n].reshape(-1, HEAD_DIM, HEAD_DIM, RW_HEADS, SCAN_NB)
    return z.transpose(0, 4, 3, 1, 2).reshape(-1, RW_HEADS, HEAD_DIM, HEAD_DIM)


def _rwkv_operands(h, seq0, nb, t, *args):
    dw_f, dw_b, kd_f, kd_b, b_f, b_b, kk, r, v, bonus, g = _rw_prep(h.reshape(-1, t, D_MODEL), seq0, nb, t, *args)
    return (dw_f, kk, b_f, kd_f, r, v), (dw_b, kk, b_b, kd_b, r, v), bonus, g


def _rwkv(h, w_rw, lora_w, lora_b, g2, lp, ones, s0_f, s0_b):
    assert DEC_BATCH == SCAN_NB and BATCH * SEQ == DEC_BATCH * DEC_SEQ
    args = (w_rw, lora_w, lora_b, g2, lp, ones)
    c_f, c_b, c_bonus, c_g = _rwkv_operands(h, 0, BATCH, SEQ, *args)
    l_f, l_b, l_bonus, l_g = _rwkv_operands(h, N_CTX // DEC_SEQ, DEC_BATCH, DEC_SEQ, *args)
    (yl_f, yl_b, yc_f, yc_b), _, sf_ctx = _rw_scan([l_f, l_b], [c_f, c_b], _scan_state([s0_f, s0_b]),
                                                   DEC_SEQ, SEQ, [False, True, False, True])
    post = (lp["gn_w"], lp["gn_b"], ones)
    rw_ctx = _rw_post(yc_f, yc_b, c_bonus, c_g, *post, BATCH, SEQ)
    rw_lat = _rw_post(yl_f, yl_b, l_bonus, l_g, *post, DEC_BATCH, DEC_SEQ)
    return rw_ctx, rw_lat, _unscan_reset_state(sf_ctx, 2), _unscan_reset_state(sf_ctx, 3)


_TM_MERGE = 512


def _merge_kernel(x_ref, gm_ref, h_ref, nac_ref, nal_ref, rwc_ref, rwl_ref, mlc_ref, mll_ref, wg_ref, wbr_ref, wo_ref,
                  o_ref):
    is_ctx = pl.program_id(0) < N_CTX // _TM_MERGE
    h = h_ref[...]
    branches = tuple(jnp.where(is_ctx, c_ref[...], l_ref[...])
                     for c_ref, l_ref in ((nac_ref, nal_ref), (rwc_ref, rwl_ref), (mlc_ref, mll_ref)))
    m = None
    for i, o_b in enumerate(branches):
        gate = jnp.dot(h, wg_ref[:, i * D_MODEL:(i + 1) * D_MODEL], preferred_element_type=F32)
        br = jnp.dot(o_b.astype(BF16), wbr_ref[i], preferred_element_type=F32)
        t = _sigmoid(gate) * br
        m = t if m is None else m + t
    mix = jnp.dot(m.astype(BF16), wo_ref[...], preferred_element_type=F32)
    o_ref[...] = x_ref[...] + gm_ref[...] * mix


def _merge(x, mod_l, h, na_ctx, na_lat, rw_ctx, rw_lat, mla_ctx, mla_lat, w_gate, w_br, w_o):
    tm = _TM_MERGE
    n_ctx = N_CTX // tm
    row = pl.BlockSpec((tm, D_MODEL), lambda i: (i, 0))
    br_ctx = pl.BlockSpec((tm, BRANCH_DIM), lambda i: (jnp.minimum(i, n_ctx - 1), 0))
    br_lat = pl.BlockSpec((tm, BRANCH_DIM), lambda i: (jnp.maximum(i - n_ctx, 0), 0))
    return pl.pallas_call(
        _merge_kernel,
        out_shape=jax.ShapeDtypeStruct((N_TOK, D_MODEL), F32),
        grid=(N_TOK // tm,),
        in_specs=[row, pl.BlockSpec((None, 1, D_MODEL), lambda i: (_cond_row(i, tm), 0, 5)),
                  row, br_ctx, br_lat, br_ctx, br_lat, br_ctx, br_lat,
                  pl.BlockSpec((D_MODEL, N_BRANCH * D_MODEL), lambda i: (0, 0)),
                  pl.BlockSpec((N_BRANCH, BRANCH_DIM, D_MODEL), lambda i: (0, 0, 0)),
                  pl.BlockSpec((D_MODEL, D_MODEL), lambda i: (0, 0))],
        out_specs=row,
        compiler_params=_params("arbitrary"),
        name="merge",
    )(x, mod_l, h, na_ctx, na_lat, rw_ctx, rw_lat, mla_ctx, mla_lat, w_gate, w_br, w_o)


def _block_diag(blocks):
    rows = sum(b.shape[0] for b in blocks)
    cols = sum(b.shape[1] for b in blocks)
    out = jnp.zeros((rows, cols), blocks[0].dtype)
    r = c = 0
    for b in blocks:
        out = lax.dynamic_update_slice(out, b, (r, c))
        r += b.shape[0]
        c += b.shape[1]
    return out


def _pad_cols(w, left, total):
    return jnp.pad(w, ((0, 0), (left, total - left - w.shape[1])))


def _mla_uq_layout(w_uq):
    per = NOPE_DIM + ROPE_DIM
    return jnp.concatenate([_pad_cols(w_uq[:, h * per:(h + 1) * per], 0, LANES) for h in range(MLA_HEADS)], axis=1)


def _mla_ukv_layout(w_ukv):
    per = NOPE_DIM + V_DIM
    k = [_pad_cols(w_ukv[:, h * per:h * per + NOPE_DIM], 0, LANES) for h in range(MLA_HEADS)]
    v = [w_ukv[:, h * per + NOPE_DIM:(h + 1) * per] for h in range(MLA_HEADS)]
    return jnp.concatenate(k + v, axis=1)


def kernel(x_prompt, x_sample, c, cache_na_k, cache_na_v, cache_mla_ckv, cache_mla_krope, state_rwkv_fwd, state_rwkv_bwd, c_ctx, ada_w, ada_b, norm_g, ffn_wg, ffn_wu, ffn_wd, w_in, na_rpb, rw_w0, rw_w2, rw_a0, rw_a2, rw_g2, rw_kk, rw_ka, rw_rk, rw_gn_w, rw_gn_b, mla_qn_g, mla_kvn_g, mla_w_uq, mla_w_ukv, w_br, w_o, final_g):
    x = (x_prompt.reshape(N_CTX, D_MODEL), x_sample.reshape(N_LAT, D_MODEL))
    cond = jnp.concatenate([c_ctx[None, :], c, jnp.zeros((N_COND - 1 - DEC_BATCH, D_MODEL), F32)], axis=0)
    mod = _modulation(cond, ada_w, ada_b)

    ones = jnp.asarray(np.kron(np.eye(RW_HEADS), np.ones((HEAD_DIM, HEAD_DIM))), BF16)
    rope = _rope_tables(DEC_SEQ)

    col_rw = 3 * NA_HEADS * HEAD_DIM
    col_mla = col_rw + 3 * RW_DIM + 2 * DECAY_LORA + 2 * ICLR_LORA + GATE_LORA
    col_kr = col_mla + Q_LORA + KV_LORA
    col_gate = col_kr + ROPE_DIM

    caches = []
    na_kv = [jnp.zeros((BATCH, DEPTH, NA_HEADS, SEQ, HEAD_DIM), F32) for _ in range(2)]
    y = None
    for l in range(DEPTH):
        mod_l = mod[l]
        wi = w_in[l]
        w_na = wi[:, :col_rw].astype(BF16)
        w_rw = wi[:, col_rw:col_mla].astype(BF16)
        w_mla = jnp.concatenate([wi[:, col_mla:col_kr], _pad_cols(wi[:, col_kr:col_gate], _ROPE_LO, LANES)],
                                axis=1).astype(BF16)
        w_gate = wi[:, col_gate:].astype(BF16)

        x, h = _ffn(x, norm_g[l, 0], mod_l, 0, ffn_wg[l, 0].astype(BF16), ffn_wu[l, 0].astype(BF16),
                    ffn_wd[l, 0].astype(BF16), post="mod", post_g=norm_g[l, 1], post_chunk=3)

        z_na = _mm(h, w_na, name="proj_na")
        na_ctx, *na_kv = _na_ctx(z_na, l, na_kv)
        na_lat = _na_latent(na_rpb[l].reshape(NA_HEADS * (2 * NA_WIN_H - 1), 2 * NA_WIN_W - 1), z_na,
                            cache_na_k, cache_na_v, l)

        lora_w = _block_diag([rw_w2[l, 0], rw_w2[l, 1], rw_a2[l, 0], rw_a2[l, 1]]).astype(BF16)
        lora_b = jnp.concatenate([rw_w0[l, 0], rw_w0[l, 1], rw_a0[l, 0], rw_a0[l, 1]])
        lp = {"kk": rw_kk[l], "ka": rw_ka[l], "rk": rw_rk[l].reshape(RW_DIM), "gn_w": rw_gn_w[l], "gn_b": rw_gn_b[l]}
        rw_ctx, rw_lat, sc_f, sc_b = _rwkv(h, w_rw, lora_w, lora_b, rw_g2[l].astype(BF16), lp, ones,
                                           state_rwkv_fwd[:, l], state_rwkv_bwd[:, l])

        w_ukv = _mla_ukv_layout(mla_w_ukv[l]).astype(BF16)
        q_m, kv_m, ckv, kr_m = _mla_prep(h, w_mla, mla_qn_g[l], mla_kvn_g[l],
                                         _mla_uq_layout(mla_w_uq[l]).astype(BF16), w_ukv)
        kv_cache = _mm(cache_mla_ckv[:, l].reshape(DEC_BATCH * PAST_LEN, KV_LORA), w_ukv, "mla_ukv_cache")
        kr_cache = _pad_cols(cache_mla_krope[:, l].reshape(DEC_BATCH * PAST_LEN, ROPE_DIM), _ROPE_LO, LANES)
        mla_ctx = _mla_ctx(q_m, kv_m, kr_m)
        mla_lat = _mla_latent(q_m, kv_m, kr_m, kv_cache, kr_cache, rope)

        x = _merge(x, mod_l, h, na_ctx, na_lat, rw_ctx, rw_lat, mla_ctx, mla_lat, w_gate,
                   w_br[l].astype(BF16), w_o[l].astype(BF16))

        ffn2 = (x, norm_g[l, 2], mod_l, 6, ffn_wg[l, 1].astype(BF16), ffn_wu[l, 1].astype(BF16),
                ffn_wd[l, 1].astype(BF16))
        if l == DEPTH - 1:
            y = _ffn(*ffn2, post="plain", post_g=final_g)
        else:
            x = _ffn(*ffn2)

        caches.append((ckv[:N_CTX].reshape(BATCH, SEQ, KV_LORA),
                       kr_m[:N_CTX, _ROPE_LO:_ROPE_LO + ROPE_DIM].reshape(BATCH, SEQ, ROPE_DIM),
                       sc_f, sc_b))

    y_prompt = y[0].reshape(BATCH, SEQ, D_MODEL)
    y_sample = y[1].reshape(DEC_BATCH, DEC_SEQ, D_MODEL)
    outs = [jnp.stack([cl[i] for cl in caches], axis=1) for i in range(4)]
    return (y_prompt, y_sample, *na_kv, *outs)
```

```python
import functools

import numpy as np
import jax
import jax.numpy as jnp
from jax import lax
from jax.experimental import pallas as pl
from jax.experimental.pallas import tpu as pltpu

F32 = jnp.float32
BF16 = jnp.bfloat16

D_MODEL = 1024
BATCH = 32
SEQ = 256
DEPTH = 2
DEC_BATCH = 8
DEC_SEQ = 1024
PAST_LEN = 256
GRID_W = 64
HEAD_DIM = 64
NA_HEADS = 4
NA_WIN_H = 8
NA_WIN_W = 16
RW_HEADS = 4
RW_DIM = RW_HEADS * HEAD_DIM
DECAY_LORA = 64
ICLR_LORA = 64
GATE_LORA = 128
GN_EPS = 64e-5
MLA_HEADS = 4
Q_LORA = 256
KV_LORA = 128
NOPE_DIM = 64
ROPE_DIM = 32
V_DIM = 64
MLA_SCALE = (NOPE_DIM + ROPE_DIM) ** -0.5
ROPE_BASE = 10000.0
N_BRANCH = 3
BRANCH_DIM = 256
D_FF = 2816
N_MOD = 9
NORM_EPS = 1e-6
NEG_INF = -1e30

N_CTX = BATCH * SEQ
N_LAT = DEC_BATCH * DEC_SEQ
N_TOK = N_CTX + N_LAT
N_COND = 16

LANES = 128
VMEM_LIMIT = 56 * 1024 * 1024

TM = 1024
TM_FFN = 512
SCAN_TC = 64


def _cond_row(i, tm):
    return jnp.where(i * tm < N_CTX, 0, 1 + (i * tm - N_CTX) // DEC_SEQ)


def _sigmoid(x):
    return 1.0 / (1.0 + jnp.exp(-x))


def _params(*sem):
    return pltpu.CompilerParams(dimension_semantics=sem, vmem_limit_bytes=VMEM_LIMIT)


def _mm_kernel(a_ref, w_ref, o_ref):
    o_ref[...] = jnp.dot(a_ref[...].astype(BF16), w_ref[...], preferred_element_type=F32)


def _mm(a, w, name):
    m, k = a.shape
    n = w.shape[1]
    tm = min(m, TM)
    assert m % tm == 0 and w.shape[0] == k
    return pl.pallas_call(
        _mm_kernel,
        out_shape=jax.ShapeDtypeStruct((m, n), F32),
        grid=(m // tm,),
        in_specs=[pl.BlockSpec((tm, k), lambda i: (i, 0)), pl.BlockSpec((k, n), lambda i: (0, 0))],
        out_specs=pl.BlockSpec((tm, n), lambda i: (i, 0)),
        compiler_params=_params("parallel"),
        name=name,
    )(a, w)


def _mod_kernel(c_ref, w_ref, b_ref, o_ref):
    c = c_ref[...]
    a = (c * _sigmoid(c)).astype(BF16)
    o_ref[...] = jnp.dot(a, w_ref[...].astype(BF16), preferred_element_type=F32) + b_ref[...]


def _modulation(cond, ada_w, ada_b):
    n = N_MOD * D_MODEL
    tn = 3 * D_MODEL
    out = pl.pallas_call(
        _mod_kernel,
        out_shape=jax.ShapeDtypeStruct((DEPTH, N_COND, n), F32),
        grid=(DEPTH, n // tn),
        in_specs=[pl.BlockSpec((N_COND, D_MODEL), lambda l, j: (0, 0)),
                  pl.BlockSpec((None, D_MODEL, tn), lambda l, j: (l, 0, j)),
                  pl.BlockSpec((None, 1, tn), lambda l, j: (l, 0, j))],
        out_specs=pl.BlockSpec((None, N_COND, tn), lambda l, j: (l, 0, j)),
        compiler_params=_params("parallel", "arbitrary"),
        name="modulation",
    )(cond, ada_w, ada_b.reshape(DEPTH, 1, n))
    return out.reshape(DEPTH, N_COND, 1, n)


def _rms(x, g):
    return x * lax.rsqrt(jnp.mean(x * x, axis=-1, keepdims=True) + NORM_EPS) * g


_FFN_CTX_TILES = N_CTX // TM_FFN


def _ffn_kernel(*refs, post, split_in):
    n_x = 2 if split_in else 1
    x_refs = refs[:n_x]
    g_ref, sh_ref, sc_ref, gt_ref, wg_ref, wu_ref, wd_ref = refs[n_x:n_x + 7]
    rest = refs[n_x + 7:]
    is_ctx = pl.program_id(0) < _FFN_CTX_TILES
    x = jnp.where(is_ctx, x_refs[0][...], x_refs[1][...]) if split_in else x_refs[0][...]
    h = (_rms(x, g_ref[...]) * (1.0 + sc_ref[...]) + sh_ref[...]).astype(BF16)
    gg = jnp.dot(h, wg_ref[...], preferred_element_type=F32)
    uu = jnp.dot(h, wu_ref[...], preferred_element_type=F32)
    a = (gg * _sigmoid(gg)) * uu
    xn = x + 0.5 * gt_ref[...] * jnp.dot(a.astype(BF16), wd_ref[...], preferred_element_type=F32)
    if post == "plain":
        g2_ref, ya_ref, yb_ref = rest
        y = _rms(xn, g2_ref[...])

        @pl.when(is_ctx)
        def _():
            ya_ref[...] = y

        @pl.when(jnp.logical_not(is_ctx))
        def _():
            yb_ref[...] = y
    elif post == "mod":
        g2_ref, sh2_ref, sc2_ref, o_ref, h2_ref = rest
        o_ref[...] = xn
        h2_ref[...] = (_rms(xn, g2_ref[...]) * (1.0 + sc2_ref[...]) + sh2_ref[...]).astype(BF16)
    else:
        (o_ref,) = rest
        o_ref[...] = xn


def _ffn(x, norm_g, mod_l, chunk0, weights, which, post=None, post_g=None, post_chunk=None):
    def mod_spec(c):
        return pl.BlockSpec((None, 1, D_MODEL), lambda i: (_cond_row(i, TM_FFN), 0, c))

    def weight(shape):
        return pl.BlockSpec((None, None) + shape, lambda i: which + (0, 0), pipeline_mode=pl.Buffered(1))

    row = pl.BlockSpec((TM_FFN, D_MODEL), lambda i: (i, 0))
    row_ctx = pl.BlockSpec((TM_FFN, D_MODEL), lambda i: (jnp.minimum(i, _FFN_CTX_TILES - 1), 0))
    row_lat = pl.BlockSpec((TM_FFN, D_MODEL), lambda i: (jnp.maximum(i - _FFN_CTX_TILES, 0), 0))
    vec = pl.BlockSpec((1, D_MODEL), lambda i: (0, 0))
    split_in = isinstance(x, tuple)
    in_specs = ([row_ctx, row_lat] if split_in else [row]) + [
        vec, mod_spec(chunk0), mod_spec(chunk0 + 1), mod_spec(chunk0 + 2),
        weight((D_MODEL, D_FF)), weight((D_MODEL, D_FF)), weight((D_FF, D_MODEL))]
    args = (list(x) if split_in else [x]) + [norm_g.reshape(1, D_MODEL), mod_l, mod_l, mod_l, *weights]
    xs = jax.ShapeDtypeStruct((N_TOK, D_MODEL), F32)
    if post == "mod":
        in_specs += [vec, mod_spec(post_chunk), mod_spec(post_chunk + 1)]
        args += [post_g.reshape(1, D_MODEL), mod_l, mod_l]
        out_shape, out_specs = (xs, jax.ShapeDtypeStruct((N_TOK, D_MODEL), BF16)), (row, row)
    elif post == "plain":
        in_specs += [vec]
        args += [post_g.reshape(1, D_MODEL)]
        out_shape = (jax.ShapeDtypeStruct((N_CTX, D_MODEL), F32), jax.ShapeDtypeStruct((N_LAT, D_MODEL), F32))
        out_specs = (row_ctx, row_lat)
    else:
        out_shape, out_specs = xs, row
    return pl.pallas_call(
        functools.partial(_ffn_kernel, post=post, split_in=split_in),
        out_shape=out_shape,
        grid=(N_TOK // TM_FFN,),
        in_specs=in_specs,
        out_specs=out_specs,
        compiler_params=_params("arbitrary"),
        name="ffn",
    )(*args)


_NT = (((1,), (1,)), ((), ()))
_LAT_ROWS = DEC_SEQ // GRID_W
_NA_KH = min(NA_WIN_H, _LAT_ROWS)
_NA_LOCAL = _NA_KH * GRID_W
_NA_PAIRS = 2 * NA_WIN_H - 2


def _head_mask(width, h, group):
    lane = lax.broadcasted_iota(jnp.int32, (1, width), 1)
    return (lane >= h * group) & (lane < (h + 1) * group)


def _stack_heads(q, nh, group):
    return jnp.concatenate([jnp.where(_head_mask(q.shape[1], h, group), q, 0.0) for h in range(nh)], axis=0)


def _pick_heads(r, nh, group):
    t = r.shape[0] // nh
    out = r[0:t]
    for h in range(1, nh):
        out = jnp.where(_head_mask(r.shape[1], h, group), r[h * t:(h + 1) * t], out)
    return out


def _softmax2(s_a, s_b):
    m = jnp.maximum(jnp.max(s_a, axis=-1, keepdims=True), jnp.max(s_b, axis=-1, keepdims=True))
    p_a = jnp.exp(s_a - m)
    p_b = jnp.exp(s_b - m)
    return p_a, p_b, jnp.sum(p_a, axis=-1, keepdims=True) + jnp.sum(p_b, axis=-1, keepdims=True)


def _na_ctx_kernel(q_ref, k_ref, v_ref, *rest):
    o_ref, ko_ref, vo_ref = rest[-3:]
    k = k_ref[...]
    v = v_ref[...]
    q_st = _stack_heads(q_ref[...] * HEAD_DIM ** -0.5, NA_HEADS, HEAD_DIM).astype(BF16)
    s = lax.dot_general(q_st, k.astype(BF16), _NT, preferred_element_type=F32)
    p = jnp.exp(s - jnp.max(s, axis=-1, keepdims=True))
    l = jnp.sum(p, axis=-1, keepdims=True)
    r = jnp.dot(p.astype(BF16), v.astype(BF16), preferred_element_type=F32) / l
    o_ref[...] = _pick_heads(r, NA_HEADS, HEAD_DIM)
    for h in range(NA_HEADS):
        ko_ref[h] = k[:, h * HEAD_DIM:(h + 1) * HEAD_DIM]
        vo_ref[h] = v[:, h * HEAD_DIM:(h + 1) * HEAD_DIM]


def _na_ctx(z_na, layer, caches):
    w = NA_HEADS * HEAD_DIM
    cache_shape = jax.ShapeDtypeStruct((BATCH, DEPTH, NA_HEADS, SEQ, HEAD_DIM), F32)
    cache_spec = pl.BlockSpec((None, None, NA_HEADS, SEQ, HEAD_DIM), lambda b: (b, layer, 0, 0, 0))
    return pl.pallas_call(
        _na_ctx_kernel,
        out_shape=(jax.ShapeDtypeStruct((N_CTX, w), F32), cache_shape, cache_shape),
        grid=(BATCH,),
        in_specs=[pl.BlockSpec((SEQ, w), lambda b: (b, 0)),
                  pl.BlockSpec((SEQ, w), lambda b: (b, 1)),
                  pl.BlockSpec((SEQ, w), lambda b: (b, 2)),
                  pl.BlockSpec(memory_space=pl.ANY), pl.BlockSpec(memory_space=pl.ANY)],
        out_specs=(pl.BlockSpec((SEQ, w), lambda b: (b, 0)), cache_spec, cache_spec),
        input_output_aliases={3: 1, 4: 2},
        compiler_params=_params("parallel"),
        name="na_ctx",
    )(z_na, z_na, z_na, *caches)


def _na_row_lo(qr):
    return min(max(qr - NA_WIN_H // 2, 0), _LAT_ROWS - _NA_KH)


def _na_lat_kernel(rpb_ref, q_ref, k_ref, v_ref, kc_ref, vc_ref, o_ref, bias_scr):
    scale = HEAD_DIM ** -0.5

    @pl.when(pl.program_id(0) == 0)
    def _():
        qc = lax.broadcasted_iota(jnp.int32, (GRID_W, 2 * GRID_W), 0)
        lane = lax.broadcasted_iota(jnp.int32, (GRID_W, 2 * GRID_W), 1)
        kc = lane & (GRID_W - 1)
        col_lo = jnp.clip(qc - NA_WIN_W // 2, 0, GRID_W - NA_WIN_W)
        rel = jnp.where((kc >= col_lo) & (kc < col_lo + NA_WIN_W), kc - qc + NA_WIN_W - 1, -1)
        second = lax.broadcasted_iota(jnp.int32, (1, 2 * GRID_W), 1) >= GRID_W

        def build(idx, carry):
            h = idx // _NA_PAIRS
            row = h * (2 * NA_WIN_H - 1) + (idx - h * _NA_PAIRS)
            tile = jnp.full((GRID_W, 2 * GRID_W), NEG_INF, F32)
            for j in range(2 * NA_WIN_W - 1):
                val = jnp.where(second, rpb_ref[row + 1, j], rpb_ref[row, j])
                tile = jnp.where(rel == j, val, tile)
            bias_scr[idx] = tile
            return carry

        lax.fori_loop(0, NA_HEADS * _NA_PAIRS, build, 0)

    kc = jnp.concatenate([kc_ref[h] for h in range(NA_HEADS)], axis=1).astype(BF16)
    vc = jnp.concatenate([vc_ref[h] for h in range(NA_HEADS)], axis=1).astype(BF16)
    for qr in range(_LAT_ROWS):
        lo = _na_row_lo(qr)
        dr0 = lo - qr + NA_WIN_H - 1
        q_st = _stack_heads(q_ref[qr * GRID_W:(qr + 1) * GRID_W, :] * scale, NA_HEADS, HEAD_DIM).astype(BF16)
        kl = k_ref[lo * GRID_W:lo * GRID_W + _NA_LOCAL, :].astype(BF16)
        vl = v_ref[lo * GRID_W:lo * GRID_W + _NA_LOCAL, :].astype(BF16)
        bias = jnp.concatenate(
            [jnp.concatenate([bias_scr[h * _NA_PAIRS + dr0 + 2 * i] for i in range(_NA_KH // 2)], axis=1)
             for h in range(NA_HEADS)], axis=0)
        s_loc = lax.dot_general(q_st, kl, _NT, preferred_element_type=F32) + bias
        s_ctx = lax.dot_general(q_st, kc, _NT, preferred_element_type=F32)
        p_loc, p_ctx, l = _softmax2(s_loc, s_ctx)
        r = (jnp.dot(p_loc.astype(BF16), vl, preferred_element_type=F32)
             + jnp.dot(p_ctx.astype(BF16), vc, preferred_element_type=F32)) / l
        o_ref[qr * GRID_W:(qr + 1) * GRID_W, :] = _pick_heads(r, NA_HEADS, HEAD_DIM)


def _na_latent(rpb, z_na, kc, vc, layer):
    w = NA_HEADS * HEAD_DIM
    lat0 = N_CTX // DEC_SEQ

    def col(c):
        return pl.BlockSpec((DEC_SEQ, w), lambda b: (lat0 + b, c))

    cache = pl.BlockSpec((None, None, NA_HEADS, PAST_LEN, HEAD_DIM), lambda b: (b, layer, 0, 0, 0))
    return pl.pallas_call(
        _na_lat_kernel,
        out_shape=jax.ShapeDtypeStruct((N_LAT, w), F32),
        grid=(DEC_BATCH,),
        in_specs=[pl.BlockSpec(memory_space=pltpu.SMEM), col(0), col(1), col(2), cache, cache],
        out_specs=pl.BlockSpec((DEC_SEQ, w), lambda b: (b, 0)),
        scratch_shapes=[pltpu.VMEM((NA_HEADS * _NA_PAIRS, GRID_W, 2 * GRID_W), F32)],
        compiler_params=_params("arbitrary"),
        name="na_latent",
    )(rpb, z_na, z_na, z_na, kc, vc)


_MLA_W = MLA_HEADS * LANES
_ROPE_LO = NOPE_DIM
_ROPE_HALF = ROPE_DIM // 2


def _rope_tables(t_len):
    n_freq = ROPE_DIM // 4
    inv = 1.0 / (ROPE_BASE ** (np.arange(n_freq) / n_freq))
    pos = np.arange(t_len)
    ang = np.concatenate([(pos // GRID_W)[:, None] * inv, (pos % GRID_W)[:, None] * inv], axis=-1)
    cos, sin = np.cos(ang), np.sin(ang)
    c = np.ones((t_len, LANES))
    s1 = np.zeros((t_len, LANES))
    s2 = np.zeros((t_len, LANES))
    a, b, e = _ROPE_LO, _ROPE_LO + _ROPE_HALF, _ROPE_LO + ROPE_DIM
    c[:, a:b] = cos
    c[:, b:e] = cos
    s1[:, a:b] = -sin
    s2[:, b:e] = sin
    return tuple(jnp.asarray(t, F32) for t in (c, s1, s2))


def _apply_rope(x, c, s1, s2):
    up = pltpu.roll(x, LANES - _ROPE_HALF, 1)
    dn = pltpu.roll(x, _ROPE_HALF, 1)
    return x * c + up * s1 + dn * s2


def _mla_prep_kernel(h_ref, w_ref, qg_ref, kvg_ref, wuq_ref, wukv_ref, q_ref, kv_ref, ckv_ref, kr_ref):
    z = jnp.dot(h_ref[...], w_ref[...], preferred_element_type=F32)
    cq = _rms(z[:, :Q_LORA], qg_ref[...])
    q_ref[...] = jnp.dot(cq.astype(BF16), wuq_ref[...], preferred_element_type=F32)
    ckv = _rms(z[:, Q_LORA:Q_LORA + KV_LORA], kvg_ref[...])
    ckv_ref[...] = ckv
    kv_ref[...] = jnp.dot(ckv.astype(BF16), wukv_ref[...], preferred_element_type=F32)
    kr_ref[...] = z[:, Q_LORA + KV_LORA:]


def _mla_prep(h, w_mla, qn_g, kvn_g, w_uq, w_ukv):
    wv = MLA_HEADS * V_DIM
    row = lambda n: pl.BlockSpec((TM, n), lambda i: (i, 0))
    full = lambda a: pl.BlockSpec(a.shape, lambda i: (0, 0))
    qn_g = qn_g.reshape(1, Q_LORA)
    kvn_g = kvn_g.reshape(1, KV_LORA)
    return pl.pallas_call(
        _mla_prep_kernel,
        out_shape=(jax.ShapeDtypeStruct((N_TOK, _MLA_W), F32), jax.ShapeDtypeStruct((N_TOK, _MLA_W + wv), F32),
                   jax.ShapeDtypeStruct((N_TOK, KV_LORA), F32), jax.ShapeDtypeStruct((N_TOK, LANES), F32)),
        grid=(N_TOK // TM,),
        in_specs=[row(D_MODEL), full(w_mla), full(qn_g), full(kvn_g), full(w_uq), full(w_ukv)],
        out_specs=(row(_MLA_W), row(_MLA_W + wv), row(KV_LORA), row(LANES)),
        compiler_params=_params("parallel"),
        name="mla_prep",
    )(h, w_mla, qn_g, kvn_g, w_uq, w_ukv)


def _mla_ctx_kernel(q_ref, kv_ref, kr_ref, o_ref):
    kr = kr_ref[...]
    v_all = kv_ref[:, _MLA_W:].astype(BF16)
    out = None
    for h in range(MLA_HEADS):
        qh = (q_ref[:, h * LANES:(h + 1) * LANES] * MLA_SCALE).astype(BF16)
        kh = (kv_ref[:, h * LANES:(h + 1) * LANES] + kr).astype(BF16)
        s = lax.dot_general(qh, kh, _NT, preferred_element_type=F32)
        p = jnp.exp(s - jnp.max(s, axis=-1, keepdims=True))
        l = jnp.sum(p, axis=-1, keepdims=True)
        r = jnp.dot(p.astype(BF16), v_all, preferred_element_type=F32) / l
        out = r if out is None else jnp.where(_head_mask(MLA_HEADS * V_DIM, h, V_DIM), r, out)
    o_ref[...] = out


def _mla_ctx(q_m, kv_m, kr_m):
    wv = MLA_HEADS * V_DIM
    return pl.pallas_call(
        _mla_ctx_kernel,
        out_shape=jax.ShapeDtypeStruct((N_CTX, wv), F32),
        grid=(BATCH,),
        in_specs=[pl.BlockSpec((SEQ, _MLA_W), lambda b: (b, 0)),
                  pl.BlockSpec((SEQ, _MLA_W + wv), lambda b: (b, 0)),
                  pl.BlockSpec((SEQ, LANES), lambda b: (b, 0))],
        out_specs=pl.BlockSpec((SEQ, wv), lambda b: (b, 0)),
        compiler_params=_params("parallel"),
        name="mla_ctx",
    )(q_m, kv_m, kr_m)


_MLA_BQ = 256


def _mla_lat_kernel(q_ref, kv_ref, kr_ref, kvc_ref, krc_ref, cq_ref, s1q_ref, s2q_ref, ck_ref, s1k_ref, s2k_ref,
                    o_ref):
    kr = _apply_rope(kr_ref[...], ck_ref[...], s1k_ref[...], s2k_ref[...])
    krc = krc_ref[...]
    v_lat = kv_ref[:, _MLA_W:].astype(BF16)
    v_ctx = kvc_ref[:, _MLA_W:].astype(BF16)
    out = None
    for h in range(MLA_HEADS):
        blk = slice(h * LANES, (h + 1) * LANES)
        qh = (_apply_rope(q_ref[:, blk], cq_ref[...], s1q_ref[...], s2q_ref[...]) * MLA_SCALE).astype(BF16)
        kl = (kv_ref[:, blk] + kr).astype(BF16)
        kc = (kvc_ref[:, blk] + krc).astype(BF16)
        s_l = lax.dot_general(qh, kl, _NT, preferred_element_type=F32)
        s_c = lax.dot_general(qh, kc, _NT, preferred_element_type=F32)
        p_l, p_c, l = _softmax2(s_l, s_c)
        r = (jnp.dot(p_l.astype(BF16), v_lat, preferred_element_type=F32)
             + jnp.dot(p_c.astype(BF16), v_ctx, preferred_element_type=F32)) / l
        out = r if out is None else jnp.where(_head_mask(MLA_HEADS * V_DIM, h, V_DIM), r, out)
    o_ref[...] = out


def _mla_latent(q_m, kv_m, kr_m, kv_cache, kr_cache, rope):
    wv = MLA_HEADS * V_DIM
    nq = DEC_SEQ // _MLA_BQ
    q0 = N_CTX // _MLA_BQ
    lat0 = N_CTX // DEC_SEQ
    c, s1, s2 = rope
    tq = pl.BlockSpec((_MLA_BQ, LANES), lambda b, j: (j, 0))
    tk = pl.BlockSpec((DEC_SEQ, LANES), lambda b, j: (0, 0))
    return pl.pallas_call(
        _mla_lat_kernel,
        out_shape=jax.ShapeDtypeStruct((N_LAT, wv), F32),
        grid=(DEC_BATCH, nq),
        in_specs=[pl.BlockSpec((_MLA_BQ, _MLA_W), lambda b, j: (q0 + b * nq + j, 0)),
                  pl.BlockSpec((DEC_SEQ, _MLA_W + wv), lambda b, j: (lat0 + b, 0)),
                  pl.BlockSpec((DEC_SEQ, LANES), lambda b, j: (lat0 + b, 0)),
                  pl.BlockSpec((PAST_LEN, _MLA_W + wv), lambda b, j: (b, 0)),
                  pl.BlockSpec((PAST_LEN, LANES), lambda b, j: (b, 0)),
                  tq, tq, tq, tk, tk, tk],
        out_specs=pl.BlockSpec((_MLA_BQ, wv), lambda b, j: (b * nq + j, 0)),
        compiler_params=_params("parallel", "arbitrary"),
        name="mla_lat",
    )(q_m, kv_m, kr_m, kv_cache, kr_cache, c, s1, s2, c, s1, s2)


def _head_sum(x, ones_ref):
    ones = ones_ref[...]
    hi = x.astype(BF16)
    rest = x - hi.astype(F32)
    mid = rest.astype(BF16)
    lo = (rest - mid.astype(F32)).astype(BF16)
    return (jnp.dot(hi, ones, preferred_element_type=F32) + jnp.dot(mid, ones, preferred_element_type=F32)
            + jnp.dot(lo, ones, preferred_element_type=F32))


SCAN_NB = LANES // (4 * RW_HEADS)
_PREP_ROWS = 512
_PREP_OUTS = 11


def _lora_act(a):
    col = lax.broadcasted_iota(jnp.int32, a.shape, 1)
    return jnp.where(col < 2 * DECAY_LORA, jnp.tanh(a), a)


def _rw_prep_kernel(h_ref, perm_ref, w_ref, lw_ref, lb_ref, g2_ref, kkw_ref, ka_ref, rk_ref, ones_ref, *out_refs,
                    nbb, tt):
    h = jnp.dot(perm_ref[...], h_ref[...].reshape(nbb * tt, D_MODEL), preferred_element_type=F32).astype(BF16)
    z = jnp.dot(h, w_ref[...], preferred_element_type=F32)
    r = z[:, 0:RW_DIM]
    k = z[:, RW_DIM:2 * RW_DIM]
    v = z[:, 2 * RW_DIM:3 * RW_DIM]
    lora_in = z[:, 3 * RW_DIM:4 * RW_DIM]
    gate_in = z[:, 4 * RW_DIM:4 * RW_DIM + GATE_LORA]
    wa = jnp.dot(_lora_act(lora_in).astype(BF16), lw_ref[...], preferred_element_type=F32) + lb_ref[...]
    g = jnp.dot(_sigmoid(gate_in).astype(BF16), g2_ref[...], preferred_element_type=F32)
    kk = k * kkw_ref[...]
    kk = kk / jnp.maximum(jnp.sqrt(_head_sum(kk * kk, ones_ref)), 1e-12)
    per_dir = []
    for d in range(2):
        decay = jnp.exp(-float(np.exp(-0.5)) * _sigmoid(wa[:, d * RW_DIM:(d + 1) * RW_DIM]))
        a = _sigmoid(wa[:, (2 + d) * RW_DIM:(3 + d) * RW_DIM])
        per_dir.append((decay, k * (1.0 + (a - 1.0) * ka_ref[...]), kk * a))
    bonus = _head_sum(r * rk_ref[...] * (per_dir[0][1] + per_dir[1][1]), ones_ref) * v
    outs = (per_dir[0][0], per_dir[1][0], per_dir[0][1], per_dir[1][1], per_dir[0][2], per_dir[1][2], kk, r, v,
            bonus, g)
    for ref, val in zip(out_refs, outs):
        ref[...] = val.reshape(tt, nbb, RW_DIM)


def _rw_prep(h_seq, seq0, nb, t, w_rw, lora_w, lora_b, g2, lp, ones):
    nbb = nb if nb * 16 <= _PREP_ROWS else SCAN_NB
    tt = _PREP_ROWS // nbb
    assert seq0 % nbb == 0 and nb % nbb == 0 and t % tt == 0 and tt % 16 == 0
    b0 = seq0 // nbb
    rows = np.arange(_PREP_ROWS)
    perm = np.zeros((_PREP_ROWS, _PREP_ROWS), np.float32)
    perm[rows, (rows % nbb) * tt + rows // nbb] = 1.0

    def full(shape):
        return pl.BlockSpec(shape, lambda i, c: (0,) * len(shape))

    return pl.pallas_call(
        functools.partial(_rw_prep_kernel, nbb=nbb, tt=tt),
        out_shape=(jax.ShapeDtypeStruct((t, nb, RW_DIM), F32),) * _PREP_OUTS,
        grid=(nb // nbb, t // tt),
        in_specs=[pl.BlockSpec((nbb, tt, D_MODEL), lambda i, c: (b0 + i, c, 0)), full(perm.shape),
                  full(w_rw.shape), full(lora_w.shape), full((1, 4 * RW_DIM)), full(g2.shape),
                  full((1, RW_DIM)), full((1, RW_DIM)), full((1, RW_DIM)), full((RW_DIM, RW_DIM))],
        out_specs=(pl.BlockSpec((tt, nbb, RW_DIM), lambda i, c: (c, i, 0)),) * _PREP_OUTS,
        compiler_params=_params("parallel", "arbitrary"),
        name="rwkv_prep",
    )(h_seq, jnp.asarray(perm, BF16), w_rw, lora_w, lora_b.reshape(1, 4 * RW_DIM), g2, lp["kk"].reshape(1, RW_DIM),
      lp["ka"].reshape(1, RW_DIM), lp["rk"].reshape(1, RW_DIM), ones)


_SCAN_OPS = 6


def _half_swap(x):
    return pltpu.roll(x, LANES // 2, 1)


def _rw_scan_kernel(*refs, n_src, n_keep, period, reverse):
    nb = SCAN_NB
    tc = SCAN_TC
    ins = [refs[s * _SCAN_OPS:(s + 1) * _SCAN_OPS] for s in range(n_src)]
    s0_ref = refs[n_src * _SCAN_OPS]
    y_refs = refs[n_src * _SCAN_OPS + 1:n_src * _SCAN_OPS + 1 + n_src]
    sf_keep_ref, sf_reset_ref, ops_scr, y_scr, s_scr = refs[n_src * _SCAN_OPS + 1 + n_src:]
    c = pl.program_id(0)
    keep_lanes = n_keep * RW_HEADS * nb
    left = lax.broadcasted_iota(jnp.int32, (nb, LANES), 1) < LANES // 2

    @pl.when(c == 0)
    def _():
        s_scr[...] = s0_ref[...]
        y_scr[...] = jnp.zeros_like(y_scr)

    @pl.when((c > 0) & (c % period == 0))
    def _():
        keep = lax.broadcasted_iota(jnp.int32, (1, 1, LANES), 2) < keep_lanes
        s_scr[...] = jnp.where(keep, s_scr[...], 0.0)

    def times(q, s):
        return (tc - 1 - 2 * q, tc - 2 - 2 * q) if reverse[s] else (2 * q, 2 * q + 1)

    def relayout_in(q, ops):
        for o in range(_SCAN_OPS):
            rows = []
            for s in range(n_src):
                t0, t1 = times(q, s)
                a0 = ins[s][o][t0]
                a1 = ins[s][o][t1]
                for h in range(RW_HEADS):
                    j = h // 2
                    p0 = a0[:, LANES * j:LANES * (j + 1)]
                    p1 = a1[:, LANES * j:LANES * (j + 1)]
                    if h % 2 == 0:
                        rows.append(jnp.where(left, p0, _half_swap(p1)))
                    else:
                        rows.append(jnp.where(left, _half_swap(p0), p1))
            tile = jnp.concatenate(rows, axis=0).T
            ops[o, 0] = tile[0:HEAD_DIM]
            ops[o, 1] = tile[HEAD_DIM:]

    def steps(ops, y):
        for t in range(2):
            for vi in range(HEAD_DIM):
                s = s_scr[vi]
                sa = jnp.sum(s * ops[1, t], axis=0, keepdims=True)
                s = s * ops[0, t] - sa * ops[2, t] + ops[5, t, pl.ds(vi, 1), :] * ops[3, t]
                s_scr[vi] = s
                y[t, pl.ds(vi, 1), :] = jnp.sum(s * ops[4, t], axis=0, keepdims=True)

    def relayout_out(q, y):
        tile = jnp.concatenate([y[0], y[1]], axis=0).T
        for s in range(n_src):
            t0, t1 = times(q, s)
            for j in range(RW_HEADS // 2):
                r0 = (s * RW_HEADS + 2 * j) * nb
                even = tile[r0:r0 + nb]
                odd = tile[r0 + nb:r0 + 2 * nb]
                y_refs[s][t0, :, LANES * j:LANES * (j + 1)] = jnp.where(left, even, _half_swap(odd))
                y_refs[s][t1, :, LANES * j:LANES * (j + 1)] = jnp.where(left, _half_swap(even), odd)

    n_pairs = tc // 2
    relayout_in(0, ops_scr.at[0])

    def body(q, carry):
        par = q % 2
        relayout_out(jnp.maximum(q - 1, 0), y_scr.at[1 - par])
        steps(ops_scr.at[par], y_scr.at[par])
        relayout_in(jnp.minimum(q + 1, n_pairs - 1), ops_scr.at[1 - par])
        return carry

    lax.fori_loop(0, n_pairs, body, 0)
    relayout_out(n_pairs - 1, y_scr.at[(n_pairs - 1) % 2])

    @pl.when(c % period == period - 1)
    def _():
        sf_reset_ref[...] = s_scr[...]

    @pl.when(c == pl.num_programs(0) - 1)
    def _():
        sf_keep_ref[...] = s_scr[...]


def _rw_scan(keep_srcs, reset_srcs, s0, t_keep, t_reset, reverse):
    tc, nb = SCAN_TC, SCAN_NB
    n_keep = len(keep_srcs)
    n_src = n_keep + len(reset_srcs)
    nc = t_keep // tc
    period = t_reset // tc
    n_phase = nc // period

    def idx(s, c):
        if s < n_keep:
            return 0, (nc - 1 - c if reverse[s] else c)
        cc = c % period
        return c // period, (period - 1 - cc if reverse[s] else cc)

    y_specs = tuple(pl.BlockSpec((tc, nb, RW_DIM), lambda c, s=s: (idx(s, c)[1], idx(s, c)[0], 0))
                    for s in range(n_src))
    in_specs, args = [], []
    for s, src in enumerate(list(keep_srcs) + list(reset_srcs)):
        in_specs += [y_specs[s]] * _SCAN_OPS
        args += list(src)
    st = pl.BlockSpec((HEAD_DIM, HEAD_DIM, LANES), lambda c: (0, 0, 0))
    y_shapes = tuple(jax.ShapeDtypeStruct((t_keep, nb, RW_DIM) if s < n_keep else (t_reset, nb * n_phase, RW_DIM), F32)
                     for s in range(n_src))
    out = pl.pallas_call(
        functools.partial(_rw_scan_kernel, n_src=n_src, n_keep=n_keep, period=period, reverse=tuple(reverse)),
        out_shape=y_shapes + (jax.ShapeDtypeStruct((HEAD_DIM, HEAD_DIM, LANES), F32),
                              jax.ShapeDtypeStruct((n_phase, HEAD_DIM, HEAD_DIM, LANES), F32)),
        grid=(nc,),
        in_specs=in_specs + [st],
        out_specs=y_specs + (st, pl.BlockSpec((None, HEAD_DIM, HEAD_DIM, LANES), lambda c: (c // period, 0, 0, 0))),
        scratch_shapes=[pltpu.VMEM((2, _SCAN_OPS, 2, HEAD_DIM, LANES), F32),
                        pltpu.VMEM((2, 2, HEAD_DIM, LANES), F32),
                        pltpu.VMEM((HEAD_DIM, HEAD_DIM, LANES), F32)],
        compiler_params=_params("arbitrary"),
        name="rwkv_scan",
    )(*args, s0)
    return out[:n_src], out[n_src], out[n_src + 1]


def _rw_post_kernel(yf_ref, yb_ref, bonus_ref, g_ref, gw_ref, gb_ref, ones_ref, o_ref, stage_scr, *, nb):
    tt = SCAN_TC
    inv = 1.0 / HEAD_DIM
    y = (yf_ref[...] + yb_ref[...]).reshape(tt * nb, RW_DIM)
    mu = _head_sum(y, ones_ref) * inv
    d = y - mu
    var = _head_sum(d * d, ones_ref) * inv
    yn = d * lax.rsqrt(var + GN_EPS) * gw_ref[...] + gb_ref[...]
    out = (yn + bonus_ref[...].reshape(tt * nb, RW_DIM)) * g_ref[...].reshape(tt * nb, RW_DIM)
    stage_scr[...] = out.reshape(tt, nb, RW_DIM)

    def body(b, carry):
        o_ref[b] = stage_scr[:, b, :]
        return carry

    lax.fori_loop(0, nb, body, 0)


def _rw_post(y_f, y_b, bonus, g, gn_w, gn_b, ones, nb, t):
    tt = SCAN_TC
    tmaj = pl.BlockSpec((tt, nb, RW_DIM), lambda c: (c, 0, 0))
    vec = pl.BlockSpec((1, RW_DIM), lambda c: (0, 0))
    out = pl.pallas_call(
        functools.partial(_rw_post_kernel, nb=nb),
        out_shape=jax.ShapeDtypeStruct((nb, t, RW_DIM), F32),
        grid=(t // tt,),
        in_specs=[tmaj, tmaj, tmaj, tmaj, vec, vec, pl.BlockSpec((RW_DIM, RW_DIM), lambda c: (0, 0))],
        out_specs=pl.BlockSpec((nb, tt, RW_DIM), lambda c: (0, c, 0)),
        scratch_shapes=[pltpu.VMEM((tt, nb, RW_DIM), F32)],
        compiler_params=_params("parallel"),
        name="rwkv_post",
    )(y_f, y_b, bonus, g, gn_w.reshape(1, RW_DIM), gn_b.reshape(1, RW_DIM), ones)
    return out.reshape(nb * t, RW_DIM)


def _scan_state(states):
    z = jnp.concatenate([s.transpose(2, 3, 1, 0).reshape(HEAD_DIM, HEAD_DIM, -1) for s in states], axis=-1)
    return jnp.pad(z, ((0, 0), (0, 0), (0, LANES - z.shape[-1])))


def _unscan_reset_state(sf, src):
    n = RW_HEADS * SCAN_NB
    z = sf[:, :, :, src * n:(src + 1) * n].reshape(-1, HEAD_DIM, HEAD_DIM, RW_HEADS, SCAN_NB)
    return z.transpose(0, 4, 3, 1, 2).reshape(-1, RW_HEADS, HEAD_DIM, HEAD_DIM)


def _rwkv_operands(h, seq0, nb, t, *args):
    dw_f, dw_b, kd_f, kd_b, b_f, b_b, kk, r, v, bonus, g = _rw_prep(h.reshape(-1, t, D_MODEL), seq0, nb, t, *args)
    return (dw_f, kk, b_f, kd_f, r, v), (dw_b, kk, b_b, kd_b, r, v), bonus, g


def _rwkv(h, w_rw, lora_w, lora_b, g2, lp, ones, s0_f, s0_b):
    assert DEC_BATCH == SCAN_NB and BATCH * SEQ == DEC_BATCH * DEC_SEQ
    args = (w_rw, lora_w, lora_b, g2, lp, ones)
    c_f, c_b, c_bonus, c_g = _rwkv_operands(h, 0, BATCH, SEQ, *args)
    l_f, l_b, l_bonus, l_g = _rwkv_operands(h, N_CTX // DEC_SEQ, DEC_BATCH, DEC_SEQ, *args)
    (yl_f, yl_b, yc_f, yc_b), _, sf_ctx = _rw_scan([l_f, l_b], [c_f, c_b], _scan_state([s0_f, s0_b]),
                                                   DEC_SEQ, SEQ, [False, True, False, True])
    post = (lp["gn_w"], lp["gn_b"], ones)
    rw_ctx = _rw_post(yc_f, yc_b, c_bonus, c_g, *post, BATCH, SEQ)
    rw_lat = _rw_post(yl_f, yl_b, l_bonus, l_g, *post, DEC_BATCH, DEC_SEQ)
    return rw_ctx, rw_lat, _unscan_reset_state(sf_ctx, 2), _unscan_reset_state(sf_ctx, 3)


_TM_MERGE = 512


def _merge_kernel(x_ref, gm_ref, h_ref, nac_ref, nal_ref, rwc_ref, rwl_ref, mlc_ref, mll_ref, wg_ref, wbr_ref, wo_ref,
                  o_ref):
    is_ctx = pl.program_id(0) < N_CTX // _TM_MERGE
    h = h_ref[...]
    branches = tuple(jnp.where(is_ctx, c_ref[...], l_ref[...])
                     for c_ref, l_ref in ((nac_ref, nal_ref), (rwc_ref, rwl_ref), (mlc_ref, mll_ref)))
    m = None
    for i, o_b in enumerate(branches):
        gate = jnp.dot(h, wg_ref[:, i * D_MODEL:(i + 1) * D_MODEL], preferred_element_type=F32)
        br = jnp.dot(o_b.astype(BF16), wbr_ref[i], preferred_element_type=F32)
        t = _sigmoid(gate) * br
        m = t if m is None else m + t
    mix = jnp.dot(m.astype(BF16), wo_ref[...], preferred_element_type=F32)
    o_ref[...] = x_ref[...] + gm_ref[...] * mix


def _merge(x, mod_l, h, na_ctx, na_lat, rw_ctx, rw_lat, mla_ctx, mla_lat, w_gate, w_br, w_o):
    tm = _TM_MERGE
    n_ctx = N_CTX // tm
    row = pl.BlockSpec((tm, D_MODEL), lambda i: (i, 0))
    br_ctx = pl.BlockSpec((tm, BRANCH_DIM), lambda i: (jnp.minimum(i, n_ctx - 1), 0))
    br_lat = pl.BlockSpec((tm, BRANCH_DIM), lambda i: (jnp.maximum(i - n_ctx, 0), 0))
    return pl.pallas_call(
        _merge_kernel,
        out_shape=jax.ShapeDtypeStruct((N_TOK, D_MODEL), F32),
        grid=(N_TOK // tm,),
        in_specs=[row, pl.BlockSpec((None, 1, D_MODEL), lambda i: (_cond_row(i, tm), 0, 5)),
                  row, br_ctx, br_lat, br_ctx, br_lat, br_ctx, br_lat,
                  pl.BlockSpec((D_MODEL, N_BRANCH * D_MODEL), lambda i: (0, 0)),
                  pl.BlockSpec((N_BRANCH, BRANCH_DIM, D_MODEL), lambda i: (0, 0, 0)),
                  pl.BlockSpec((D_MODEL, D_MODEL), lambda i: (0, 0))],
        out_specs=row,
        compiler_params=_params("arbitrary"),
        name="merge",
    )(x, mod_l, h, na_ctx, na_lat, rw_ctx, rw_lat, mla_ctx, mla_lat, w_gate, w_br, w_o)


def _block_diag(blocks):
    rows = sum(b.shape[0] for b in blocks)
    cols = sum(b.shape[1] for b in blocks)
    out = jnp.zeros((rows, cols), blocks[0].dtype)
    r = c = 0
    for b in blocks:
        out = lax.dynamic_update_slice(out, b, (r, c))
        r += b.shape[0]
        c += b.shape[1]
    return out


def _pad_cols(w, left, total):
    return jnp.pad(w, ((0, 0), (left, total - left - w.shape[1])))


def _mla_uq_layout(w_uq):
    per = NOPE_DIM + ROPE_DIM
    return jnp.concatenate([_pad_cols(w_uq[:, h * per:(h + 1) * per], 0, LANES) for h in range(MLA_HEADS)], axis=1)


def _mla_ukv_layout(w_ukv):
    per = NOPE_DIM + V_DIM
    k = [_pad_cols(w_ukv[:, h * per:h * per + NOPE_DIM], 0, LANES) for h in range(MLA_HEADS)]
    v = [w_ukv[:, h * per + NOPE_DIM:(h + 1) * per] for h in range(MLA_HEADS)]
    return jnp.concatenate(k + v, axis=1)


def kernel(x_prompt, x_sample, c, cache_na_k, cache_na_v, cache_mla_ckv, cache_mla_krope, state_rwkv_fwd, state_rwkv_bwd, c_ctx, ada_w, ada_b, norm_g, ffn_wg, ffn_wu, ffn_wd, w_in, na_rpb, rw_w0, rw_w2, rw_a0, rw_a2, rw_g2, rw_kk, rw_ka, rw_rk, rw_gn_w, rw_gn_b, mla_qn_g, mla_kvn_g, mla_w_uq, mla_w_ukv, w_br, w_o, final_g):
    x = (x_prompt.reshape(N_CTX, D_MODEL), x_sample.reshape(N_LAT, D_MODEL))
    cond = jnp.concatenate([c_ctx[None, :], c, jnp.zeros((N_COND - 1 - DEC_BATCH, D_MODEL), F32)], axis=0)
    mod = _modulation(cond, ada_w, ada_b)

    ones = jnp.asarray(np.kron(np.eye(RW_HEADS), np.ones((HEAD_DIM, HEAD_DIM))), BF16)
    rope = _rope_tables(DEC_SEQ)

    col_rw = 3 * NA_HEADS * HEAD_DIM
    col_mla = col_rw + 3 * RW_DIM + 2 * DECAY_LORA + 2 * ICLR_LORA + GATE_LORA
    col_kr = col_mla + Q_LORA + KV_LORA
    col_gate = col_kr + ROPE_DIM

    ffn_w = (ffn_wg.astype(BF16), ffn_wu.astype(BF16), ffn_wd.astype(BF16))
    caches = []
    na_kv = [jnp.zeros((BATCH, DEPTH, NA_HEADS, SEQ, HEAD_DIM), F32) for _ in range(2)]
    y = None
    for l in range(DEPTH):
        mod_l = mod[l]
        wi = w_in[l]
        w_na = wi[:, :col_rw].astype(BF16)
        w_rw = wi[:, col_rw:col_mla].astype(BF16)
        w_mla = jnp.concatenate([wi[:, col_mla:col_kr], _pad_cols(wi[:, col_kr:col_gate], _ROPE_LO, LANES)],
                                axis=1).astype(BF16)
        w_gate = wi[:, col_gate:].astype(BF16)

        x, h = _ffn(x, norm_g[l, 0], mod_l, 0, ffn_w, (l, 0), post="mod", post_g=norm_g[l, 1], post_chunk=3)

        z_na = _mm(h, w_na, name="proj_na")
        na_ctx, *na_kv = _na_ctx(z_na, l, na_kv)
        na_lat = _na_latent(na_rpb[l].reshape(NA_HEADS * (2 * NA_WIN_H - 1), 2 * NA_WIN_W - 1), z_na,
                            cache_na_k, cache_na_v, l)

        lora_w = _block_diag([rw_w2[l, 0], rw_w2[l, 1], rw_a2[l, 0], rw_a2[l, 1]]).astype(BF16)
        lora_b = jnp.concatenate([rw_w0[l, 0], rw_w0[l, 1], rw_a0[l, 0], rw_a0[l, 1]])
        lp = {"kk": rw_kk[l], "ka": rw_ka[l], "rk": rw_rk[l].reshape(RW_DIM), "gn_w": rw_gn_w[l], "gn_b": rw_gn_b[l]}
        rw_ctx, rw_lat, sc_f, sc_b = _rwkv(h, w_rw, lora_w, lora_b, rw_g2[l].astype(BF16), lp, ones,
                                           state_rwkv_fwd[:, l], state_rwkv_bwd[:, l])

        w_ukv = _mla_ukv_layout(mla_w_ukv[l]).astype(BF16)
        q_m, kv_m, ckv, kr_m = _mla_prep(h, w_mla, mla_qn_g[l], mla_kvn_g[l],
                                         _mla_uq_layout(mla_w_uq[l]).astype(BF16), w_ukv)
        kv_cache = _mm(cache_mla_ckv[:, l].reshape(DEC_BATCH * PAST_LEN, KV_LORA), w_ukv, "mla_ukv_cache")
        kr_cache = _pad_cols(cache_mla_krope[:, l].reshape(DEC_BATCH * PAST_LEN, ROPE_DIM), _ROPE_LO, LANES)
        mla_ctx = _mla_ctx(q_m, kv_m, kr_m)
        mla_lat = _mla_latent(q_m, kv_m, kr_m, kv_cache, kr_cache, rope)

        x = _merge(x, mod_l, h, na_ctx, na_lat, rw_ctx, rw_lat, mla_ctx, mla_lat, w_gate,
                   w_br[l].astype(BF16), w_o[l].astype(BF16))

        ffn2 = (x, norm_g[l, 2], mod_l, 6, ffn_w, (l, 1))
        if l == DEPTH - 1:
            y = _ffn(*ffn2, post="plain", post_g=final_g)
        else:
            x = _ffn(*ffn2)

        caches.append((ckv[:N_CTX].reshape(BATCH, SEQ, KV_LORA),
                       kr_m[:N_CTX, _ROPE_LO:_ROPE_LO + ROPE_DIM].reshape(BATCH, SEQ, ROPE_DIM),
                       sc_f, sc_b))

    y_prompt = y[0].reshape(BATCH, SEQ, D_MODEL)
    y_sample = y[1].reshape(DEC_BATCH, DEC_SEQ, D_MODEL)
    outs = [jnp.stack([cl[i] for cl in caches], axis=1) for i in range(4)]
    return (y_prompt, y_sample, *na_kv, *outs)
```

```python
import functools

import numpy as np
import jax
import jax.numpy as jnp
from jax import lax
from jax.experimental import pallas as pl
from jax.experimental.pallas import tpu as pltpu

F32 = jnp.float32
BF16 = jnp.bfloat16

D_MODEL = 1024
BATCH = 32
SEQ = 256
DEPTH = 2
DEC_BATCH = 8
DEC_SEQ = 1024
PAST_LEN = 256
GRID_W = 64
HEAD_DIM = 64
NA_HEADS = 4
NA_WIN_H = 8
NA_WIN_W = 16
RW_HEADS = 4
RW_DIM = RW_HEADS * HEAD_DIM
DECAY_LORA = 64
ICLR_LORA = 64
GATE_LORA = 128
GN_EPS = 64e-5
MLA_HEADS = 4
Q_LORA = 256
KV_LORA = 128
NOPE_DIM = 64
ROPE_DIM = 32
V_DIM = 64
MLA_SCALE = (NOPE_DIM + ROPE_DIM) ** -0.5
ROPE_BASE = 10000.0
N_BRANCH = 3
BRANCH_DIM = 256
D_FF = 2816
N_MOD = 9
NORM_EPS = 1e-6
NEG_INF = -1e30

N_CTX = BATCH * SEQ
N_LAT = DEC_BATCH * DEC_SEQ
N_TOK = N_CTX + N_LAT
N_COND = 16

LANES = 128
VMEM_LIMIT = 56 * 1024 * 1024

TM = 1024
TM_FFN = 512
SCAN_TC = 64


def _cond_row(i, tm):
    return jnp.where(i * tm < N_CTX, 0, 1 + (i * tm - N_CTX) // DEC_SEQ)


def _sigmoid(x):
    return 1.0 / (1.0 + jnp.exp(-x))


def _params(*sem):
    return pltpu.CompilerParams(dimension_semantics=sem, vmem_limit_bytes=VMEM_LIMIT)


def _mm_kernel(a_ref, w_ref, o_ref):
    o_ref[...] = jnp.dot(a_ref[...].astype(BF16), w_ref[...], preferred_element_type=F32)


def _mm(a, w, name):
    m, k = a.shape
    n = w.shape[1]
    tm = min(m, TM)
    assert m % tm == 0 and w.shape[0] == k
    return pl.pallas_call(
        _mm_kernel,
        out_shape=jax.ShapeDtypeStruct((m, n), F32),
        grid=(m // tm,),
        in_specs=[pl.BlockSpec((tm, k), lambda i: (i, 0)), pl.BlockSpec((k, n), lambda i: (0, 0))],
        out_specs=pl.BlockSpec((tm, n), lambda i: (i, 0)),
        compiler_params=_params("parallel"),
        name=name,
    )(a, w)


def _mod_kernel(c_ref, w_ref, b_ref, o_ref):
    c = c_ref[...]
    a = (c * _sigmoid(c)).astype(BF16)
    o_ref[...] = jnp.dot(a, w_ref[...].astype(BF16), preferred_element_type=F32) + b_ref[...]


def _modulation(cond, ada_w, ada_b):
    n = N_MOD * D_MODEL
    tn = 3 * D_MODEL
    out = pl.pallas_call(
        _mod_kernel,
        out_shape=jax.ShapeDtypeStruct((DEPTH, N_COND, n), F32),
        grid=(DEPTH, n // tn),
        in_specs=[pl.BlockSpec((N_COND, D_MODEL), lambda l, j: (0, 0)),
                  pl.BlockSpec((None, D_MODEL, tn), lambda l, j: (l, 0, j)),
                  pl.BlockSpec((None, 1, tn), lambda l, j: (l, 0, j))],
        out_specs=pl.BlockSpec((None, N_COND, tn), lambda l, j: (l, 0, j)),
        compiler_params=_params("parallel", "arbitrary"),
        name="modulation",
    )(cond, ada_w, ada_b.reshape(DEPTH, 1, n))
    return out.reshape(DEPTH, N_COND, 1, n)


def _rms(x, g):
    return x * lax.rsqrt(jnp.mean(x * x, axis=-1, keepdims=True) + NORM_EPS) * g


_FFN_CTX_TILES = N_CTX // TM_FFN


def _ffn_kernel(*refs, post, split_in):
    n_x = 2 if split_in else 1
    x_refs = refs[:n_x]
    g_ref, sh_ref, sc_ref, gt_ref, wg_ref, wu_ref, wd_ref = refs[n_x:n_x + 7]
    rest = refs[n_x + 7:]
    is_ctx = pl.program_id(0) < _FFN_CTX_TILES
    x = jnp.where(is_ctx, x_refs[0][...], x_refs[1][...]) if split_in else x_refs[0][...]
    h = (_rms(x, g_ref[...]) * (1.0 + sc_ref[...]) + sh_ref[...]).astype(BF16)
    gg = jnp.dot(h, wg_ref[...], preferred_element_type=F32)
    uu = jnp.dot(h, wu_ref[...], preferred_element_type=F32)
    a = (gg * _sigmoid(gg)) * uu
    xn = x + 0.5 * gt_ref[...] * jnp.dot(a.astype(BF16), wd_ref[...], preferred_element_type=F32)
    if post == "plain":
        g2_ref, ya_ref, yb_ref = rest
        y = _rms(xn, g2_ref[...])

        @pl.when(is_ctx)
        def _():
            ya_ref[...] = y

        @pl.when(jnp.logical_not(is_ctx))
        def _():
            yb_ref[...] = y
    elif post == "mod":
        g2_ref, sh2_ref, sc2_ref, o_ref, h2_ref = rest
        o_ref[...] = xn
        h2_ref[...] = (_rms(xn, g2_ref[...]) * (1.0 + sc2_ref[...]) + sh2_ref[...]).astype(BF16)
    else:
        (o_ref,) = rest
        o_ref[...] = xn


def _ffn(x, norm_g, mod_l, chunk0, weights, which, post=None, post_g=None, post_chunk=None):
    def mod_spec(c):
        return pl.BlockSpec((None, 1, D_MODEL), lambda i: (_cond_row(i, TM_FFN), 0, c))

    def weight(shape):
        return pl.BlockSpec((None, None) + shape, lambda i: which + (0, 0), pipeline_mode=pl.Buffered(1))

    row = pl.BlockSpec((TM_FFN, D_MODEL), lambda i: (i, 0))
    row_ctx = pl.BlockSpec((TM_FFN, D_MODEL), lambda i: (jnp.minimum(i, _FFN_CTX_TILES - 1), 0))
    row_lat = pl.BlockSpec((TM_FFN, D_MODEL), lambda i: (jnp.maximum(i - _FFN_CTX_TILES, 0), 0))
    vec = pl.BlockSpec((1, D_MODEL), lambda i: (0, 0))
    split_in = isinstance(x, tuple)
    in_specs = ([row_ctx, row_lat] if split_in else [row]) + [
        vec, mod_spec(chunk0), mod_spec(chunk0 + 1), mod_spec(chunk0 + 2),
        weight((D_MODEL, D_FF)), weight((D_MODEL, D_FF)), weight((D_FF, D_MODEL))]
    args = (list(x) if split_in else [x]) + [norm_g.reshape(1, D_MODEL), mod_l, mod_l, mod_l, *weights]
    xs = jax.ShapeDtypeStruct((N_TOK, D_MODEL), F32)
    if post == "mod":
        in_specs += [vec, mod_spec(post_chunk), mod_spec(post_chunk + 1)]
        args += [post_g.reshape(1, D_MODEL), mod_l, mod_l]
        out_shape, out_specs = (xs, jax.ShapeDtypeStruct((N_TOK, D_MODEL), BF16)), (row, row)
    elif post == "plain":
        in_specs += [vec]
        args += [post_g.reshape(1, D_MODEL)]
        out_shape = (jax.ShapeDtypeStruct((N_CTX, D_MODEL), F32), jax.ShapeDtypeStruct((N_LAT, D_MODEL), F32))
        out_specs = (row_ctx, row_lat)
    else:
        out_shape, out_specs = xs, row
    return pl.pallas_call(
        functools.partial(_ffn_kernel, post=post, split_in=split_in),
        out_shape=out_shape,
        grid=(N_TOK // TM_FFN,),
        in_specs=in_specs,
        out_specs=out_specs,
        compiler_params=_params("arbitrary"),
        name="ffn",
    )(*args)


_NT = (((1,), (1,)), ((), ()))
_LAT_ROWS = DEC_SEQ // GRID_W
_NA_KH = min(NA_WIN_H, _LAT_ROWS)
_NA_LOCAL = _NA_KH * GRID_W
_NA_PAIRS = 2 * NA_WIN_H - 2


def _head_mask(width, h, group):
    lane = lax.broadcasted_iota(jnp.int32, (1, width), 1)
    return (lane >= h * group) & (lane < (h + 1) * group)


def _stack_heads(q, nh, group):
    return jnp.concatenate([jnp.where(_head_mask(q.shape[1], h, group), q, 0.0) for h in range(nh)], axis=0)


def _pick_heads(r, nh, group):
    t = r.shape[0] // nh
    out = r[0:t]
    for h in range(1, nh):
        out = jnp.where(_head_mask(r.shape[1], h, group), r[h * t:(h + 1) * t], out)
    return out


def _softmax2(s_a, s_b):
    m = jnp.maximum(jnp.max(s_a, axis=-1, keepdims=True), jnp.max(s_b, axis=-1, keepdims=True))
    p_a = jnp.exp(s_a - m)
    p_b = jnp.exp(s_b - m)
    return p_a, p_b, jnp.sum(p_a, axis=-1, keepdims=True) + jnp.sum(p_b, axis=-1, keepdims=True)


def _na_ctx_kernel(q_ref, k_ref, v_ref, *rest):
    o_ref, ko_ref, vo_ref = rest[-3:]
    k = k_ref[...]
    v = v_ref[...]
    q_st = _stack_heads(q_ref[...] * HEAD_DIM ** -0.5, NA_HEADS, HEAD_DIM).astype(BF16)
    s = lax.dot_general(q_st, k.astype(BF16), _NT, preferred_element_type=F32)
    p = jnp.exp(s - jnp.max(s, axis=-1, keepdims=True))
    l = jnp.sum(p, axis=-1, keepdims=True)
    r = jnp.dot(p.astype(BF16), v.astype(BF16), preferred_element_type=F32) / l
    o_ref[...] = _pick_heads(r, NA_HEADS, HEAD_DIM)
    for h in range(NA_HEADS):
        ko_ref[h] = k[:, h * HEAD_DIM:(h + 1) * HEAD_DIM]
        vo_ref[h] = v[:, h * HEAD_DIM:(h + 1) * HEAD_DIM]


def _na_ctx(z_na, layer, caches):
    w = NA_HEADS * HEAD_DIM
    cache_shape = jax.ShapeDtypeStruct((BATCH, DEPTH, NA_HEADS, SEQ, HEAD_DIM), F32)
    cache_spec = pl.BlockSpec((None, None, NA_HEADS, SEQ, HEAD_DIM), lambda b: (b, layer, 0, 0, 0))
    return pl.pallas_call(
        _na_ctx_kernel,
        out_shape=(jax.ShapeDtypeStruct((N_CTX, w), F32), cache_shape, cache_shape),
        grid=(BATCH,),
        in_specs=[pl.BlockSpec((SEQ, w), lambda b: (b, 0)),
                  pl.BlockSpec((SEQ, w), lambda b: (b, 1)),
                  pl.BlockSpec((SEQ, w), lambda b: (b, 2)),
                  pl.BlockSpec(memory_space=pl.ANY), pl.BlockSpec(memory_space=pl.ANY)],
        out_specs=(pl.BlockSpec((SEQ, w), lambda b: (b, 0)), cache_spec, cache_spec),
        input_output_aliases={3: 1, 4: 2},
        compiler_params=_params("parallel"),
        name="na_ctx",
    )(z_na, z_na, z_na, *caches)


def _na_row_lo(qr):
    return min(max(qr - NA_WIN_H // 2, 0), _LAT_ROWS - _NA_KH)


def _na_lat_kernel(rpb_ref, q_ref, k_ref, v_ref, kc_ref, vc_ref, o_ref, bias_scr):
    scale = HEAD_DIM ** -0.5

    @pl.when(pl.program_id(0) == 0)
    def _():
        qc = lax.broadcasted_iota(jnp.int32, (GRID_W, 2 * GRID_W), 0)
        lane = lax.broadcasted_iota(jnp.int32, (GRID_W, 2 * GRID_W), 1)
        kc = lane & (GRID_W - 1)
        col_lo = jnp.clip(qc - NA_WIN_W // 2, 0, GRID_W - NA_WIN_W)
        rel = jnp.where((kc >= col_lo) & (kc < col_lo + NA_WIN_W), kc - qc + NA_WIN_W - 1, -1)
        second = lax.broadcasted_iota(jnp.int32, (1, 2 * GRID_W), 1) >= GRID_W

        def build(idx, carry):
            h = idx // _NA_PAIRS
            row = h * (2 * NA_WIN_H - 1) + (idx - h * _NA_PAIRS)
            tile = jnp.full((GRID_W, 2 * GRID_W), NEG_INF, F32)
            for j in range(2 * NA_WIN_W - 1):
                val = jnp.where(second, rpb_ref[row + 1, j], rpb_ref[row, j])
                tile = jnp.where(rel == j, val, tile)
            bias_scr[idx] = tile
            return carry

        lax.fori_loop(0, NA_HEADS * _NA_PAIRS, build, 0)

    kc = jnp.concatenate([kc_ref[h] for h in range(NA_HEADS)], axis=1).astype(BF16)
    vc = jnp.concatenate([vc_ref[h] for h in range(NA_HEADS)], axis=1).astype(BF16)
    for qr in range(_LAT_ROWS):
        lo = _na_row_lo(qr)
        dr0 = lo - qr + NA_WIN_H - 1
        q_st = _stack_heads(q_ref[qr * GRID_W:(qr + 1) * GRID_W, :] * scale, NA_HEADS, HEAD_DIM).astype(BF16)
        kl = k_ref[lo * GRID_W:lo * GRID_W + _NA_LOCAL, :].astype(BF16)
        vl = v_ref[lo * GRID_W:lo * GRID_W + _NA_LOCAL, :].astype(BF16)
        bias = jnp.concatenate(
            [jnp.concatenate([bias_scr[h * _NA_PAIRS + dr0 + 2 * i] for i in range(_NA_KH // 2)], axis=1)
             for h in range(NA_HEADS)], axis=0)
        s_loc = lax.dot_general(q_st, kl, _NT, preferred_element_type=F32) + bias
        s_ctx = lax.dot_general(q_st, kc, _NT, preferred_element_type=F32)
        p_loc, p_ctx, l = _softmax2(s_loc, s_ctx)
        r = (jnp.dot(p_loc.astype(BF16), vl, preferred_element_type=F32)
             + jnp.dot(p_ctx.astype(BF16), vc, preferred_element_type=F32)) / l
        o_ref[qr * GRID_W:(qr + 1) * GRID_W, :] = _pick_heads(r, NA_HEADS, HEAD_DIM)


def _na_latent(rpb, z_na, kc, vc, layer):
    w = NA_HEADS * HEAD_DIM
    lat0 = N_CTX // DEC_SEQ

    def col(c):
        return pl.BlockSpec((DEC_SEQ, w), lambda b: (lat0 + b, c))

    cache = pl.BlockSpec((None, None, NA_HEADS, PAST_LEN, HEAD_DIM), lambda b: (b, layer, 0, 0, 0))
    return pl.pallas_call(
        _na_lat_kernel,
        out_shape=jax.ShapeDtypeStruct((N_LAT, w), F32),
        grid=(DEC_BATCH,),
        in_specs=[pl.BlockSpec(memory_space=pltpu.SMEM), col(0), col(1), col(2), cache, cache],
        out_specs=pl.BlockSpec((DEC_SEQ, w), lambda b: (b, 0)),
        scratch_shapes=[pltpu.VMEM((NA_HEADS * _NA_PAIRS, GRID_W, 2 * GRID_W), F32)],
        compiler_params=_params("arbitrary"),
        name="na_latent",
    )(rpb, z_na, z_na, z_na, kc, vc)


_MLA_W = MLA_HEADS * LANES
_ROPE_LO = NOPE_DIM
_ROPE_HALF = ROPE_DIM // 2


def _rope_tables(t_len):
    n_freq = ROPE_DIM // 4
    inv = 1.0 / (ROPE_BASE ** (np.arange(n_freq) / n_freq))
    pos = np.arange(t_len)
    ang = np.concatenate([(pos // GRID_W)[:, None] * inv, (pos % GRID_W)[:, None] * inv], axis=-1)
    cos, sin = np.cos(ang), np.sin(ang)
    c = np.ones((t_len, LANES))
    s1 = np.zeros((t_len, LANES))
    s2 = np.zeros((t_len, LANES))
    a, b, e = _ROPE_LO, _ROPE_LO + _ROPE_HALF, _ROPE_LO + ROPE_DIM
    c[:, a:b] = cos
    c[:, b:e] = cos
    s1[:, a:b] = -sin
    s2[:, b:e] = sin
    return tuple(jnp.asarray(t, F32) for t in (c, s1, s2))


def _apply_rope(x, c, s1, s2):
    up = pltpu.roll(x, LANES - _ROPE_HALF, 1)
    dn = pltpu.roll(x, _ROPE_HALF, 1)
    return x * c + up * s1 + dn * s2


def _mla_prep_kernel(h_ref, w_ref, qg_ref, kvg_ref, wuq_ref, wukv_ref, q_ref, kv_ref, ckv_ref, kr_ref, zna_ref):
    z = jnp.dot(h_ref[...], w_ref[...], preferred_element_type=F32)
    zna_ref[...] = z[:, Q_LORA + KV_LORA + LANES:]
    z = z[:, :Q_LORA + KV_LORA + LANES]
    cq = _rms(z[:, :Q_LORA], qg_ref[...])
    q_ref[...] = jnp.dot(cq.astype(BF16), wuq_ref[...], preferred_element_type=F32)
    ckv = _rms(z[:, Q_LORA:Q_LORA + KV_LORA], kvg_ref[...])
    ckv_ref[...] = ckv
    kv_ref[...] = jnp.dot(ckv.astype(BF16), wukv_ref[...], preferred_element_type=F32)
    kr_ref[...] = z[:, Q_LORA + KV_LORA:]


def _mla_prep(h, w_mla, qn_g, kvn_g, w_uq, w_ukv):
    wv = MLA_HEADS * V_DIM
    row = lambda n: pl.BlockSpec((TM, n), lambda i: (i, 0))
    full = lambda a: pl.BlockSpec(a.shape, lambda i: (0, 0))
    qn_g = qn_g.reshape(1, Q_LORA)
    kvn_g = kvn_g.reshape(1, KV_LORA)
    return pl.pallas_call(
        _mla_prep_kernel,
        out_shape=(jax.ShapeDtypeStruct((N_TOK, _MLA_W), F32), jax.ShapeDtypeStruct((N_TOK, _MLA_W + wv), F32),
                   jax.ShapeDtypeStruct((N_TOK, KV_LORA), F32), jax.ShapeDtypeStruct((N_TOK, LANES), F32),
                   jax.ShapeDtypeStruct((N_TOK, 3 * NA_HEADS * HEAD_DIM), F32)),
        grid=(N_TOK // TM,),
        in_specs=[row(D_MODEL), full(w_mla), full(qn_g), full(kvn_g), full(w_uq), full(w_ukv)],
        out_specs=(row(_MLA_W), row(_MLA_W + wv), row(KV_LORA), row(LANES), row(3 * NA_HEADS * HEAD_DIM)),
        compiler_params=_params("parallel"),
        name="mla_prep",
    )(h, w_mla, qn_g, kvn_g, w_uq, w_ukv)


def _mla_ctx_kernel(q_ref, kv_ref, kr_ref, o_ref):
    kr = kr_ref[...]
    v_all = kv_ref[:, _MLA_W:].astype(BF16)
    out = None
    for h in range(MLA_HEADS):
        qh = (q_ref[:, h * LANES:(h + 1) * LANES] * MLA_SCALE).astype(BF16)
        kh = (kv_ref[:, h * LANES:(h + 1) * LANES] + kr).astype(BF16)
        s = lax.dot_general(qh, kh, _NT, preferred_element_type=F32)
        p = jnp.exp(s - jnp.max(s, axis=-1, keepdims=True))
        l = jnp.sum(p, axis=-1, keepdims=True)
        r = jnp.dot(p.astype(BF16), v_all, preferred_element_type=F32) / l
        out = r if out is None else jnp.where(_head_mask(MLA_HEADS * V_DIM, h, V_DIM), r, out)
    o_ref[...] = out


def _mla_ctx(q_m, kv_m, kr_m):
    wv = MLA_HEADS * V_DIM
    return pl.pallas_call(
        _mla_ctx_kernel,
        out_shape=jax.ShapeDtypeStruct((N_CTX, wv), F32),
        grid=(BATCH,),
        in_specs=[pl.BlockSpec((SEQ, _MLA_W), lambda b: (b, 0)),
                  pl.BlockSpec((SEQ, _MLA_W + wv), lambda b: (b, 0)),
                  pl.BlockSpec((SEQ, LANES), lambda b: (b, 0))],
        out_specs=pl.BlockSpec((SEQ, wv), lambda b: (b, 0)),
        compiler_params=_params("parallel"),
        name="mla_ctx",
    )(q_m, kv_m, kr_m)


_MLA_BQ = 256


def _mla_lat_kernel(q_ref, kv_ref, kr_ref, kvc_ref, krc_ref, cq_ref, s1q_ref, s2q_ref, ck_ref, s1k_ref, s2k_ref,
                    o_ref):
    kr = _apply_rope(kr_ref[...], ck_ref[...], s1k_ref[...], s2k_ref[...])
    krc = krc_ref[...]
    v_lat = kv_ref[:, _MLA_W:].astype(BF16)
    v_ctx = kvc_ref[:, _MLA_W:].astype(BF16)
    out = None
    for h in range(MLA_HEADS):
        blk = slice(h * LANES, (h + 1) * LANES)
        qh = (_apply_rope(q_ref[:, blk], cq_ref[...], s1q_ref[...], s2q_ref[...]) * MLA_SCALE).astype(BF16)
        kl = (kv_ref[:, blk] + kr).astype(BF16)
        kc = (kvc_ref[:, blk] + krc).astype(BF16)
        s_l = lax.dot_general(qh, kl, _NT, preferred_element_type=F32)
        s_c = lax.dot_general(qh, kc, _NT, preferred_element_type=F32)
        p_l, p_c, l = _softmax2(s_l, s_c)
        r = (jnp.dot(p_l.astype(BF16), v_lat, preferred_element_type=F32)
             + jnp.dot(p_c.astype(BF16), v_ctx, preferred_element_type=F32)) / l
        out = r if out is None else jnp.where(_head_mask(MLA_HEADS * V_DIM, h, V_DIM), r, out)
    o_ref[...] = out


def _mla_latent(q_m, kv_m, kr_m, kv_cache, kr_cache, rope):
    wv = MLA_HEADS * V_DIM
    nq = DEC_SEQ // _MLA_BQ
    q0 = N_CTX // _MLA_BQ
    lat0 = N_CTX // DEC_SEQ
    c, s1, s2 = rope
    tq = pl.BlockSpec((_MLA_BQ, LANES), lambda b, j: (j, 0))
    tk = pl.BlockSpec((DEC_SEQ, LANES), lambda b, j: (0, 0))
    return pl.pallas_call(
        _mla_lat_kernel,
        out_shape=jax.ShapeDtypeStruct((N_LAT, wv), F32),
        grid=(DEC_BATCH, nq),
        in_specs=[pl.BlockSpec((_MLA_BQ, _MLA_W), lambda b, j: (q0 + b * nq + j, 0)),
                  pl.BlockSpec((DEC_SEQ, _MLA_W + wv), lambda b, j: (lat0 + b, 0)),
                  pl.BlockSpec((DEC_SEQ, LANES), lambda b, j: (lat0 + b, 0)),
                  pl.BlockSpec((PAST_LEN, _MLA_W + wv), lambda b, j: (b, 0)),
                  pl.BlockSpec((PAST_LEN, LANES), lambda b, j: (b, 0)),
                  tq, tq, tq, tk, tk, tk],
        out_specs=pl.BlockSpec((_MLA_BQ, wv), lambda b, j: (b * nq + j, 0)),
        compiler_params=_params("parallel", "arbitrary"),
        name="mla_lat",
    )(q_m, kv_m, kr_m, kv_cache, kr_cache, c, s1, s2, c, s1, s2)


def _head_sum(x, ones_ref):
    ones = ones_ref[...]
    hi = x.astype(BF16)
    rest = x - hi.astype(F32)
    mid = rest.astype(BF16)
    lo = (rest - mid.astype(F32)).astype(BF16)
    return (jnp.dot(hi, ones, preferred_element_type=F32) + jnp.dot(mid, ones, preferred_element_type=F32)
            + jnp.dot(lo, ones, preferred_element_type=F32))


SCAN_NB = LANES // (4 * RW_HEADS)
_PREP_ROWS = 512
_PREP_OUTS = 11


def _lora_act(a):
    col = lax.broadcasted_iota(jnp.int32, a.shape, 1)
    return jnp.where(col < 2 * DECAY_LORA, jnp.tanh(a), a)


def _rw_prep_kernel(h_ref, perm_ref, w_ref, lw_ref, lb_ref, g2_ref, kkw_ref, ka_ref, rk_ref, ones_ref, *out_refs,
                    nbb, tt):
    h = jnp.dot(perm_ref[...], h_ref[...].reshape(nbb * tt, D_MODEL), preferred_element_type=F32).astype(BF16)
    z = jnp.dot(h, w_ref[...], preferred_element_type=F32)
    r = z[:, 0:RW_DIM]
    k = z[:, RW_DIM:2 * RW_DIM]
    v = z[:, 2 * RW_DIM:3 * RW_DIM]
    lora_in = z[:, 3 * RW_DIM:4 * RW_DIM]
    gate_in = z[:, 4 * RW_DIM:4 * RW_DIM + GATE_LORA]
    wa = jnp.dot(_lora_act(lora_in).astype(BF16), lw_ref[...], preferred_element_type=F32) + lb_ref[...]
    g = jnp.dot(_sigmoid(gate_in).astype(BF16), g2_ref[...], preferred_element_type=F32)
    kk = k * kkw_ref[...]
    kk = kk / jnp.maximum(jnp.sqrt(_head_sum(kk * kk, ones_ref)), 1e-12)
    per_dir = []
    for d in range(2):
        decay = jnp.exp(-float(np.exp(-0.5)) * _sigmoid(wa[:, d * RW_DIM:(d + 1) * RW_DIM]))
        a = _sigmoid(wa[:, (2 + d) * RW_DIM:(3 + d) * RW_DIM])
        per_dir.append((decay, k * (1.0 + (a - 1.0) * ka_ref[...]), kk * a))
    bonus = _head_sum(r * rk_ref[...] * (per_dir[0][1] + per_dir[1][1]), ones_ref) * v
    outs = (per_dir[0][0], per_dir[1][0], per_dir[0][1], per_dir[1][1], per_dir[0][2], per_dir[1][2], kk, r, v,
            bonus, g)
    for ref, val in zip(out_refs, outs):
        ref[...] = val.reshape(tt, nbb, RW_DIM)


def _rw_prep(h_seq, seq0, nb, t, w_rw, lora_w, lora_b, g2, lp, ones):
    nbb = nb if nb * 16 <= _PREP_ROWS else SCAN_NB
    tt = _PREP_ROWS // nbb
    assert seq0 % nbb == 0 and nb % nbb == 0 and t % tt == 0 and tt % 16 == 0
    b0 = seq0 // nbb
    rows = np.arange(_PREP_ROWS)
    perm = np.zeros((_PREP_ROWS, _PREP_ROWS), np.float32)
    perm[rows, (rows % nbb) * tt + rows // nbb] = 1.0

    def full(shape):
        return pl.BlockSpec(shape, lambda i, c: (0,) * len(shape))

    return pl.pallas_call(
        functools.partial(_rw_prep_kernel, nbb=nbb, tt=tt),
        out_shape=(jax.ShapeDtypeStruct((t, nb, RW_DIM), F32),) * _PREP_OUTS,
        grid=(nb // nbb, t // tt),
        in_specs=[pl.BlockSpec((nbb, tt, D_MODEL), lambda i, c: (b0 + i, c, 0)), full(perm.shape),
                  full(w_rw.shape), full(lora_w.shape), full((1, 4 * RW_DIM)), full(g2.shape),
                  full((1, RW_DIM)), full((1, RW_DIM)), full((1, RW_DIM)), full((RW_DIM, RW_DIM))],
        out_specs=(pl.BlockSpec((tt, nbb, RW_DIM), lambda i, c: (c, i, 0)),) * _PREP_OUTS,
        compiler_params=_params("parallel", "arbitrary"),
        name="rwkv_prep",
    )(h_seq, jnp.asarray(perm, BF16), w_rw, lora_w, lora_b.reshape(1, 4 * RW_DIM), g2, lp["kk"].reshape(1, RW_DIM),
      lp["ka"].reshape(1, RW_DIM), lp["rk"].reshape(1, RW_DIM), ones)


_SCAN_OPS = 6


def _half_swap(x):
    return pltpu.roll(x, LANES // 2, 1)


def _rw_scan_kernel(*refs, n_src, n_keep, period, reverse):
    nb = SCAN_NB
    tc = SCAN_TC
    ins = [refs[s * _SCAN_OPS:(s + 1) * _SCAN_OPS] for s in range(n_src)]
    s0_ref = refs[n_src * _SCAN_OPS]
    y_refs = refs[n_src * _SCAN_OPS + 1:n_src * _SCAN_OPS + 1 + n_src]
    sf_keep_ref, sf_reset_ref, ops_scr, y_scr, s_scr = refs[n_src * _SCAN_OPS + 1 + n_src:]
    c = pl.program_id(0)
    keep_lanes = n_keep * RW_HEADS * nb
    left = lax.broadcasted_iota(jnp.int32, (nb, LANES), 1) < LANES // 2

    @pl.when(c == 0)
    def _():
        s_scr[...] = s0_ref[...]
        y_scr[...] = jnp.zeros_like(y_scr)

    @pl.when((c > 0) & (c % period == 0))
    def _():
        keep = lax.broadcasted_iota(jnp.int32, (1, 1, LANES), 2) < keep_lanes
        s_scr[...] = jnp.where(keep, s_scr[...], 0.0)

    def times(q, s):
        return (tc - 1 - 2 * q, tc - 2 - 2 * q) if reverse[s] else (2 * q, 2 * q + 1)

    def relayout_in(q, ops):
        for o in range(_SCAN_OPS):
            rows = []
            for s in range(n_src):
                t0, t1 = times(q, s)
                a0 = ins[s][o][t0]
                a1 = ins[s][o][t1]
                for h in range(RW_HEADS):
                    j = h // 2
                    p0 = a0[:, LANES * j:LANES * (j + 1)]
                    p1 = a1[:, LANES * j:LANES * (j + 1)]
                    if h % 2 == 0:
                        rows.append(jnp.where(left, p0, _half_swap(p1)))
                    else:
                        rows.append(jnp.where(left, _half_swap(p0), p1))
            tile = jnp.concatenate(rows, axis=0).T
            ops[o, 0] = tile[0:HEAD_DIM]
            ops[o, 1] = tile[HEAD_DIM:]

    def steps(ops, y):
        for t in range(2):
            for vi in range(HEAD_DIM):
                s = s_scr[vi]
                sa = jnp.sum(s * ops[1, t], axis=0, keepdims=True)
                s = s * ops[0, t] - sa * ops[2, t] + ops[5, t, pl.ds(vi, 1), :] * ops[3, t]
                s_scr[vi] = s
                y[t, pl.ds(vi, 1), :] = jnp.sum(s * ops[4, t], axis=0, keepdims=True)

    def relayout_out(q, y):
        tile = jnp.concatenate([y[0], y[1]], axis=0).T
        for s in range(n_src):
            t0, t1 = times(q, s)
            for j in range(RW_HEADS // 2):
                r0 = (s * RW_HEADS + 2 * j) * nb
                even = tile[r0:r0 + nb]
                odd = tile[r0 + nb:r0 + 2 * nb]
                y_refs[s][t0, :, LANES * j:LANES * (j + 1)] = jnp.where(left, even, _half_swap(odd))
                y_refs[s][t1, :, LANES * j:LANES * (j + 1)] = jnp.where(left, _half_swap(even), odd)

    n_pairs = tc // 2
    relayout_in(0, ops_scr.at[0])

    def body(q, carry):
        par = q % 2
        relayout_out(jnp.maximum(q - 1, 0), y_scr.at[1 - par])
        steps(ops_scr.at[par], y_scr.at[par])
        relayout_in(jnp.minimum(q + 1, n_pairs - 1), ops_scr.at[1 - par])
        return carry

    lax.fori_loop(0, n_pairs, body, 0)
    relayout_out(n_pairs - 1, y_scr.at[(n_pairs - 1) % 2])

    @pl.when(c % period == period - 1)
    def _():
        sf_reset_ref[...] = s_scr[...]

    @pl.when(c == pl.num_programs(0) - 1)
    def _():
        sf_keep_ref[...] = s_scr[...]


def _rw_scan(keep_srcs, reset_srcs, s0, t_keep, t_reset, reverse):
    tc, nb = SCAN_TC, SCAN_NB
    n_keep = len(keep_srcs)
    n_src = n_keep + len(reset_srcs)
    nc = t_keep // tc
    period = t_reset // tc
    n_phase = nc // period

    def idx(s, c):
        if s < n_keep:
            return 0, (nc - 1 - c if reverse[s] else c)
        cc = c % period
        return c // period, (period - 1 - cc if reverse[s] else cc)

    y_specs = tuple(pl.BlockSpec((tc, nb, RW_DIM), lambda c, s=s: (idx(s, c)[1], idx(s, c)[0], 0))
                    for s in range(n_src))
    in_specs, args = [], []
    for s, src in enumerate(list(keep_srcs) + list(reset_srcs)):
        in_specs += [y_specs[s]] * _SCAN_OPS
        args += list(src)
    st = pl.BlockSpec((HEAD_DIM, HEAD_DIM, LANES), lambda c: (0, 0, 0))
    y_shapes = tuple(jax.ShapeDtypeStruct((t_keep, nb, RW_DIM) if s < n_keep else (t_reset, nb * n_phase, RW_DIM), F32)
                     for s in range(n_src))
    out = pl.pallas_call(
        functools.partial(_rw_scan_kernel, n_src=n_src, n_keep=n_keep, period=period, reverse=tuple(reverse)),
        out_shape=y_shapes + (jax.ShapeDtypeStruct((HEAD_DIM, HEAD_DIM, LANES), F32),
                              jax.ShapeDtypeStruct((n_phase, HEAD_DIM, HEAD_DIM, LANES), F32)),
        grid=(nc,),
        in_specs=in_specs + [st],
        out_specs=y_specs + (st, pl.BlockSpec((None, HEAD_DIM, HEAD_DIM, LANES), lambda c: (c // period, 0, 0, 0))),
        scratch_shapes=[pltpu.VMEM((2, _SCAN_OPS, 2, HEAD_DIM, LANES), F32),
                        pltpu.VMEM((2, 2, HEAD_DIM, LANES), F32),
                        pltpu.VMEM((HEAD_DIM, HEAD_DIM, LANES), F32)],
        compiler_params=_params("arbitrary"),
        name="rwkv_scan",
    )(*args, s0)
    return out[:n_src], out[n_src], out[n_src + 1]


def _rw_post_kernel(yf_ref, yb_ref, bonus_ref, g_ref, gw_ref, gb_ref, ones_ref, o_ref, stage_scr, *, nb):
    tt = SCAN_TC
    inv = 1.0 / HEAD_DIM
    y = (yf_ref[...] + yb_ref[...]).reshape(tt * nb, RW_DIM)
    mu = _head_sum(y, ones_ref) * inv
    d = y - mu
    var = _head_sum(d * d, ones_ref) * inv
    yn = d * lax.rsqrt(var + GN_EPS) * gw_ref[...] + gb_ref[...]
    out = (yn + bonus_ref[...].reshape(tt * nb, RW_DIM)) * g_ref[...].reshape(tt * nb, RW_DIM)
    stage_scr[...] = out.reshape(tt, nb, RW_DIM)

    def body(b, carry):
        o_ref[b] = stage_scr[:, b, :]
        return carry

    lax.fori_loop(0, nb, body, 0)


def _rw_post(y_f, y_b, bonus, g, gn_w, gn_b, ones, nb, t):
    tt = SCAN_TC
    tmaj = pl.BlockSpec((tt, nb, RW_DIM), lambda c: (c, 0, 0))
    vec = pl.BlockSpec((1, RW_DIM), lambda c: (0, 0))
    out = pl.pallas_call(
        functools.partial(_rw_post_kernel, nb=nb),
        out_shape=jax.ShapeDtypeStruct((nb, t, RW_DIM), F32),
        grid=(t // tt,),
        in_specs=[tmaj, tmaj, tmaj, tmaj, vec, vec, pl.BlockSpec((RW_DIM, RW_DIM), lambda c: (0, 0))],
        out_specs=pl.BlockSpec((nb, tt, RW_DIM), lambda c: (0, c, 0)),
        scratch_shapes=[pltpu.VMEM((tt, nb, RW_DIM), F32)],
        compiler_params=_params("parallel"),
        name="rwkv_post",
    )(y_f, y_b, bonus, g, gn_w.reshape(1, RW_DIM), gn_b.reshape(1, RW_DIM), ones)
    return out.reshape(nb * t, RW_DIM)


def _scan_state(states):
    z = jnp.concatenate([s.transpose(2, 3, 1, 0).reshape(HEAD_DIM, HEAD_DIM, -1) for s in states], axis=-1)
    return jnp.pad(z, ((0, 0), (0, 0), (0, LANES - z.shape[-1])))


def _unscan_reset_state(sf, src):
    n = RW_HEADS * SCAN_NB
    z = sf[:, :, :, src * n:(src + 1) * n].reshape(-1, HEAD_DIM, HEAD_DIM, RW_HEADS, SCAN_NB)
    return z.transpose(0, 4, 3, 1, 2).reshape(-1, RW_HEADS, HEAD_DIM, HEAD_DIM)


def _rwkv_operands(h, seq0, nb, t, *args):
    dw_f, dw_b, kd_f, kd_b, b_f, b_b, kk, r, v, bonus, g = _rw_prep(h.reshape(-1, t, D_MODEL), seq0, nb, t, *args)
    return (dw_f, kk, b_f, kd_f, r, v), (dw_b, kk, b_b, kd_b, r, v), bonus, g


def _rwkv(h, w_rw, lora_w, lora_b, g2, lp, ones, s0_f, s0_b):
    assert DEC_BATCH == SCAN_NB and BATCH * SEQ == DEC_BATCH * DEC_SEQ
    args = (w_rw, lora_w, lora_b, g2, lp, ones)
    c_f, c_b, c_bonus, c_g = _rwkv_operands(h, 0, BATCH, SEQ, *args)
    l_f, l_b, l_bonus, l_g = _rwkv_operands(h, N_CTX // DEC_SEQ, DEC_BATCH, DEC_SEQ, *args)
    (yl_f, yl_b, yc_f, yc_b), _, sf_ctx = _rw_scan([l_f, l_b], [c_f, c_b], _scan_state([s0_f, s0_b]),
                                                   DEC_SEQ, SEQ, [False, True, False, True])
    post = (lp["gn_w"], lp["gn_b"], ones)
    rw_ctx = _rw_post(yc_f, yc_b, c_bonus, c_g, *post, BATCH, SEQ)
    rw_lat = _rw_post(yl_f, yl_b, l_bonus, l_g, *post, DEC_BATCH, DEC_SEQ)
    return rw_ctx, rw_lat, _unscan_reset_state(sf_ctx, 2), _unscan_reset_state(sf_ctx, 3)


_TM_MERGE = 512


def _merge_kernel(x_ref, gm_ref, h_ref, nac_ref, nal_ref, rwc_ref, rwl_ref, mlc_ref, mll_ref, wg_ref, wbr_ref, wo_ref,
                  o_ref):
    is_ctx = pl.program_id(0) < N_CTX // _TM_MERGE
    h = h_ref[...]
    branches = tuple(jnp.where(is_ctx, c_ref[...], l_ref[...])
                     for c_ref, l_ref in ((nac_ref, nal_ref), (rwc_ref, rwl_ref), (mlc_ref, mll_ref)))
    m = None
    for i, o_b in enumerate(branches):
        gate = jnp.dot(h, wg_ref[:, i * D_MODEL:(i + 1) * D_MODEL], preferred_element_type=F32)
        br = jnp.dot(o_b.astype(BF16), wbr_ref[i], preferred_element_type=F32)
        t = _sigmoid(gate) * br
        m = t if m is None else m + t
    mix = jnp.dot(m.astype(BF16), wo_ref[...], preferred_element_type=F32)
    o_ref[...] = x_ref[...] + gm_ref[...] * mix


def _merge(x, mod_l, h, na_ctx, na_lat, rw_ctx, rw_lat, mla_ctx, mla_lat, w_gate, w_br, w_o):
    tm = _TM_MERGE
    n_ctx = N_CTX // tm
    row = pl.BlockSpec((tm, D_MODEL), lambda i: (i, 0))
    br_ctx = pl.BlockSpec((tm, BRANCH_DIM), lambda i: (jnp.minimum(i, n_ctx - 1), 0))
    br_lat = pl.BlockSpec((tm, BRANCH_DIM), lambda i: (jnp.maximum(i - n_ctx, 0), 0))
    return pl.pallas_call(
        _merge_kernel,
        out_shape=jax.ShapeDtypeStruct((N_TOK, D_MODEL), F32),
        grid=(N_TOK // tm,),
        in_specs=[row, pl.BlockSpec((None, 1, D_MODEL), lambda i: (_cond_row(i, tm), 0, 5)),
                  row, br_ctx, br_lat, br_ctx, br_lat, br_ctx, br_lat,
                  pl.BlockSpec((D_MODEL, N_BRANCH * D_MODEL), lambda i: (0, 0)),
                  pl.BlockSpec((N_BRANCH, BRANCH_DIM, D_MODEL), lambda i: (0, 0, 0)),
                  pl.BlockSpec((D_MODEL, D_MODEL), lambda i: (0, 0))],
        out_specs=row,
        compiler_params=_params("arbitrary"),
        name="merge",
    )(x, mod_l, h, na_ctx, na_lat, rw_ctx, rw_lat, mla_ctx, mla_lat, w_gate, w_br, w_o)


def _block_diag(blocks):
    rows = sum(b.shape[0] for b in blocks)
    cols = sum(b.shape[1] for b in blocks)
    out = jnp.zeros((rows, cols), blocks[0].dtype)
    r = c = 0
    for b in blocks:
        out = lax.dynamic_update_slice(out, b, (r, c))
        r += b.shape[0]
        c += b.shape[1]
    return out


def _pad_cols(w, left, total):
    return jnp.pad(w, ((0, 0), (left, total - left - w.shape[1])))


def _mla_uq_layout(w_uq):
    per = NOPE_DIM + ROPE_DIM
    return jnp.concatenate([_pad_cols(w_uq[:, h * per:(h + 1) * per], 0, LANES) for h in range(MLA_HEADS)], axis=1)


def _mla_ukv_layout(w_ukv):
    per = NOPE_DIM + V_DIM
    k = [_pad_cols(w_ukv[:, h * per:h * per + NOPE_DIM], 0, LANES) for h in range(MLA_HEADS)]
    v = [w_ukv[:, h * per + NOPE_DIM:(h + 1) * per] for h in range(MLA_HEADS)]
    return jnp.concatenate(k + v, axis=1)


def kernel(x_prompt, x_sample, c, cache_na_k, cache_na_v, cache_mla_ckv, cache_mla_krope, state_rwkv_fwd, state_rwkv_bwd, c_ctx, ada_w, ada_b, norm_g, ffn_wg, ffn_wu, ffn_wd, w_in, na_rpb, rw_w0, rw_w2, rw_a0, rw_a2, rw_g2, rw_kk, rw_ka, rw_rk, rw_gn_w, rw_gn_b, mla_qn_g, mla_kvn_g, mla_w_uq, mla_w_ukv, w_br, w_o, final_g):
    x = (x_prompt.reshape(N_CTX, D_MODEL), x_sample.reshape(N_LAT, D_MODEL))
    cond = jnp.concatenate([c_ctx[None, :], c, jnp.zeros((N_COND - 1 - DEC_BATCH, D_MODEL), F32)], axis=0)
    mod = _modulation(cond, ada_w, ada_b)

    ones = jnp.asarray(np.kron(np.eye(RW_HEADS), np.ones((HEAD_DIM, HEAD_DIM))), BF16)
    rope = _rope_tables(DEC_SEQ)

    col_rw = 3 * NA_HEADS * HEAD_DIM
    col_mla = col_rw + 3 * RW_DIM + 2 * DECAY_LORA + 2 * ICLR_LORA + GATE_LORA
    col_kr = col_mla + Q_LORA + KV_LORA
    col_gate = col_kr + ROPE_DIM

    ffn_w = (ffn_wg.astype(BF16), ffn_wu.astype(BF16), ffn_wd.astype(BF16))
    caches = []
    na_kv = [jnp.zeros((BATCH, DEPTH, NA_HEADS, SEQ, HEAD_DIM), F32) for _ in range(2)]
    y = None
    for l in range(DEPTH):
        mod_l = mod[l]
        wi = w_in[l]
        w_na = wi[:, :col_rw].astype(BF16)
        w_rw = wi[:, col_rw:col_mla].astype(BF16)
        w_mla = jnp.concatenate([wi[:, col_mla:col_kr], _pad_cols(wi[:, col_kr:col_gate], _ROPE_LO, LANES)],
                                axis=1).astype(BF16)
        w_gate = wi[:, col_gate:].astype(BF16)

        x, h = _ffn(x, norm_g[l, 0], mod_l, 0, ffn_w, (l, 0), post="mod", post_g=norm_g[l, 1], post_chunk=3)

        w_ukv = _mla_ukv_layout(mla_w_ukv[l]).astype(BF16)
        q_m, kv_m, ckv, kr_m, z_na = _mla_prep(h, jnp.concatenate([w_mla, w_na], axis=1), mla_qn_g[l], mla_kvn_g[l],
                                               _mla_uq_layout(mla_w_uq[l]).astype(BF16), w_ukv)

        na_ctx, *na_kv = _na_ctx(z_na, l, na_kv)
        na_lat = _na_latent(na_rpb[l].reshape(NA_HEADS * (2 * NA_WIN_H - 1), 2 * NA_WIN_W - 1), z_na,
                            cache_na_k, cache_na_v, l)

        lora_w = _block_diag([rw_w2[l, 0], rw_w2[l, 1], rw_a2[l, 0], rw_a2[l, 1]]).astype(BF16)
        lora_b = jnp.concatenate([rw_w0[l, 0], rw_w0[l, 1], rw_a0[l, 0], rw_a0[l, 1]])
        lp = {"kk": rw_kk[l], "ka": rw_ka[l], "rk": rw_rk[l].reshape(RW_DIM), "gn_w": rw_gn_w[l], "gn_b": rw_gn_b[l]}
        rw_ctx, rw_lat, sc_f, sc_b = _rwkv(h, w_rw, lora_w, lora_b, rw_g2[l].astype(BF16), lp, ones,
                                           state_rwkv_fwd[:, l], state_rwkv_bwd[:, l])

        kv_cache = _mm(cache_mla_ckv[:, l].reshape(DEC_BATCH * PAST_LEN, KV_LORA), w_ukv, "mla_ukv_cache")
        kr_cache = _pad_cols(cache_mla_krope[:, l].reshape(DEC_BATCH * PAST_LEN, ROPE_DIM), _ROPE_LO, LANES)
        mla_ctx = _mla_ctx(q_m, kv_m, kr_m)
        mla_lat = _mla_latent(q_m, kv_m, kr_m, kv_cache, kr_cache, rope)

        x = _merge(x, mod_l, h, na_ctx, na_lat, rw_ctx, rw_lat, mla_ctx, mla_lat, w_gate,
                   w_br[l].astype(BF16), w_o[l].astype(BF16))

        ffn2 = (x, norm_g[l, 2], mod_l, 6, ffn_w, (l, 1))
        if l == DEPTH - 1:
            y = _ffn(*ffn2, post="plain", post_g=final_g)
        else:
            x = _ffn(*ffn2)

        caches.append((ckv[:N_CTX].reshape(BATCH, SEQ, KV_LORA),
                       kr_m[:N_CTX, _ROPE_LO:_ROPE_LO + ROPE_DIM].reshape(BATCH, SEQ, ROPE_DIM),
                       sc_f, sc_b))

    y_prompt = y[0].reshape(BATCH, SEQ, D_MODEL)
    y_sample = y[1].reshape(DEC_BATCH, DEC_SEQ, D_MODEL)
    outs = [jnp.stack([cl[i] for cl in caches], axis=1) for i in range(4)]
    return (y_prompt, y_sample, *na_kv, *outs)
```
